```python
import jax, jax.numpy as jnp
from jax import lax
import numpy as np

D_MODEL = 1024
BATCH = 16
SEQ = 2048
DEPTH = 2
DEC_BATCH = 32
DEC_SEQ = 4
PAST_LEN = 16384
PAGE_SIZE = 128

CHUNK = 128
A_GROUPS = 8
A_WIDTH = D_MODEL
A_GROUP_DIM = A_WIDTH // A_GROUPS
N_HEADS = 8
HEAD_DIM = 128
N_KV_HEADS = 2
KV_GROUP = N_HEADS // N_KV_HEADS
IDX_HEADS = 8
IDX_DIM = 64
TOPK_MAX = 256
Q_BLOCK = 128
D_FF = 2816
EPS = 1e-6

SPLIT_SIZES = (A_WIDTH, A_WIDTH, N_HEADS * HEAD_DIM, N_KV_HEADS * HEAD_DIM, N_KV_HEADS * HEAD_DIM,
               IDX_HEADS * IDX_DIM, IDX_DIM, IDX_HEADS, D_MODEL, D_MODEL)
D_IN = sum(SPLIT_SIZES)
SPLIT_POINTS = tuple(int(c) for c in np.cumsum(SPLIT_SIZES)[:-1])

kernel_name = 'hybrid_gmlp_dsa_macaron_step'


def _rmsnorm(x, g):
    x32 = x.astype(jnp.float32)
    y = x32 * lax.rsqrt(jnp.mean(x32 * x32, axis=-1, keepdims=True) + EPS)
    return y.astype(x.dtype) * g


def _swiglu(x, w_up, w_down):
    gate, up = jnp.split(x @ w_up, 2, axis=-1)
    return (jax.nn.silu(gate) * up) @ w_down


_gather_rows = jax.vmap(lambda a, i: a[i])


def _chunk_spatial(vc, w_s, b_s):
    n = vc.shape[2]
    w = jnp.where(jnp.tril(jnp.ones((n, n), dtype=bool)), w_s[:, :n, :n], 0)
    return jnp.einsum('gts,bnsgc->bntgc', w, vc) + b_s[:, :n].T[None, None, :, :, None]


def _index_scores(qi, wi, ki):
    dots = jnp.einsum('bthd,bsd->bths', qi.astype(jnp.float32), ki.astype(jnp.float32)) * IDX_DIM ** -0.5
    return jnp.einsum('bth,bths->bts', wi.astype(jnp.float32), jax.nn.relu(dots))


def _sparse_attend(q, kg, vg, valid):
    B, T = q.shape[:2]
    qg = q.reshape(B, T, N_KV_HEADS, KV_GROUP, HEAD_DIM).astype(jnp.float32)
    s = jnp.einsum('btkgd,btskd->btkgs', qg, kg.astype(jnp.float32)) * HEAD_DIM ** -0.5
    s = jnp.where(valid[:, :, None, None, :], s, -jnp.inf)
    p = jax.nn.softmax(s, axis=-1)
    o = jnp.einsum('btkgs,btskd->btkgd', p, vg.astype(jnp.float32))
    return o.reshape(B, T, N_HEADS * HEAD_DIM).astype(q.dtype)


def _prompt_mixer(w_s, b_s):
    def mix(v, q, k, vv, qi, ki, wi):
        B, S = v.shape[:2]
        vc = v.reshape(B, S // CHUNK, CHUNK, A_GROUPS, A_GROUP_DIM)
        sv = _chunk_spatial(vc, w_s, b_s).reshape(B, S, A_WIDTH)
        k_sel = min(TOPK_MAX, S // 4)
        key_pos = jnp.arange(S)

        def block(n):
            t0 = n * Q_BLOCK
            qb = lax.dynamic_slice_in_dim(q, t0, Q_BLOCK, axis=1)
            qib = lax.dynamic_slice_in_dim(qi, t0, Q_BLOCK, axis=1)
            wib = lax.dynamic_slice_in_dim(wi, t0, Q_BLOCK, axis=1)
            q_pos = t0 + jnp.arange(Q_BLOCK)
            causal = key_pos[None, None, :] <= q_pos[None, :, None]
            scores = jnp.where(causal, _index_scores(qib, wib, ki), -jnp.inf)
            _, idx = lax.top_k(scores, k_sel)
            valid = idx <= q_pos[None, :, None]
            return _sparse_attend(qb, _gather_rows(k, idx), _gather_rows(vv, idx), valid)

        out = lax.map(block, jnp.arange(S // Q_BLOCK))
        attn = out.transpose(1, 0, 2, 3).reshape(B, S, N_HEADS * HEAD_DIM)
        return sv, attn
    return mix


def _sample_mixer(w_s, b_s, cache_k, cache_v, cache_kidx, page_table):
    def mix(v, q, k, vv, qi, ki, wi):
        DB, T = v.shape[:2]
        sv = _chunk_spatial(v.reshape(DB, 1, T, A_GROUPS, A_GROUP_DIM), w_s, b_s).reshape(DB, T, A_WIDTH)
        past = page_table.shape[1] * PAGE_SIZE
        L = past + T
        k_sel = min(TOPK_MAX, L // 4)
        ki_past = cache_kidx[page_table].reshape(DB, past, IDX_DIM)
        ki_all = jnp.concatenate([ki_past, ki.astype(ki_past.dtype)], axis=1)
        q_pos = past + jnp.arange(T)
        causal = jnp.arange(L)[None, None, :] <= q_pos[None, :, None]
        scores = jnp.where(causal, _index_scores(qi, wi, ki_all), -jnp.inf)
        _, idx = lax.top_k(scores, k_sel)
        pidx = jnp.minimum(idx, past - 1)
        phys = _gather_rows(page_table, pidx // PAGE_SIZE)
        off = pidx % PAGE_SIZE
        nidx = jnp.clip(idx - past, 0, T - 1)
        from_past = (idx < past)[..., None, None]
        kg = jnp.where(from_past, cache_k[phys, off], _gather_rows(k, nidx))
        vg = jnp.where(from_past, cache_v[phys, off], _gather_rows(vv, nidx))
        valid = idx <= q_pos[None, :, None]
        return sv, _sparse_attend(q, kg, vg, valid)
    return mix


def _layer(x, p, mixer):
    B, T = x.shape[:2]
    x = x + 0.5 * _swiglu(_rmsnorm(x, p['g_ffn1']), p['w_up1'], p['w_down1'])
    h = _rmsnorm(x, p['g_mix'])
    u, v, q, k, vv, qi, ki, wi, ga, gb = jnp.split(h @ p['w_in'], SPLIT_POINTS, axis=-1)
    u = jax.nn.gelu(u)
    v = _rmsnorm(jax.nn.gelu(v), p['g_v'])
    q = _rmsnorm(q.reshape(B, T, N_HEADS, HEAD_DIM), p['g_q'])
    k = _rmsnorm(k.reshape(B, T, N_KV_HEADS, HEAD_DIM), p['g_k'])
    vv = vv.reshape(B, T, N_KV_HEADS, HEAD_DIM)
    qi = qi.reshape(B, T, IDX_HEADS, IDX_DIM)
    wi = wi * IDX_HEADS ** -0.5
    sv, attn = mixer(v, q, k, vv, qi, ki, wi)
    merged = jax.nn.sigmoid(ga) * ((u * sv) @ p['w_pa']) + jax.nn.sigmoid(gb) * (attn @ p['w_pb'])
    x = x + merged @ p['w_out']
    x = x + 0.5 * _swiglu(_rmsnorm(x, p['g_ffn2']), p['w_up2'], p['w_down2'])
    return x, k, vv, ki, v


def setup_inputs(seed: int = 0) -> dict:
    key = jax.random.key(seed)
    ks = jax.random.split(key, 26)
    n_pages = PAST_LEN // PAGE_SIZE
    n_used = DEC_BATCH * n_pages
    n_pool = n_used + n_used // 4

    def nrm(k, shape, scale):
        return jax.random.normal(k, shape, jnp.float32) * scale

    def gain(k, shape):
        return 1.0 + 0.02 * jax.random.normal(k, shape, jnp.float32)

    page_table = jax.random.permutation(ks[0], n_pool)[:n_used].reshape(DEC_BATCH, n_pages).astype(jnp.int32)
    return {
        'x_prompt': nrm(ks[1], (BATCH, SEQ, D_MODEL), 1.0),
        'x_sample': nrm(ks[2], (DEC_BATCH, DEC_SEQ, D_MODEL), 1.0),
        'cache_k': nrm(ks[3], (DEPTH, n_pool, PAGE_SIZE, N_KV_HEADS, HEAD_DIM), 1.0),
        'cache_v': nrm(ks[4], (DEPTH, n_pool, PAGE_SIZE, N_KV_HEADS, HEAD_DIM), 1.0),
        'cache_kidx': nrm(ks[5], (DEPTH, n_pool, PAGE_SIZE, IDX_DIM), 1.0),
        'page_table': page_table,
        'g_ffn1': gain(ks[6], (DEPTH, D_MODEL)),
        'w_up1': nrm(ks[7], (DEPTH, D_MODEL, 2 * D_FF), D_MODEL ** -0.5),
        'w_down1': nrm(ks[8], (DEPTH, D_FF, D_MODEL), D_FF ** -0.5),
        'g_mix': gain(ks[9], (DEPTH, D_MODEL)),
        'w_in': nrm(ks[10], (DEPTH, D_MODEL, D_IN), D_MODEL ** -0.5),
        'g_v': gain(ks[11], (DEPTH, A_WIDTH)),
        'g_q': gain(ks[12], (DEPTH, HEAD_DIM)),
        'g_k': gain(ks[13], (DEPTH, HEAD_DIM)),
        'w_s': nrm(ks[14], (DEPTH, A_GROUPS, CHUNK, CHUNK), CHUNK ** -0.5),
        'b_s': 1.0 + 0.1 * jax.random.normal(ks[15], (DEPTH, A_GROUPS, CHUNK), jnp.float32),
        'w_pa': nrm(ks[16], (DEPTH, A_WIDTH, D_MODEL), A_WIDTH ** -0.5),
        'w_pb': nrm(ks[17], (DEPTH, N_HEADS * HEAD_DIM, D_MODEL), (N_HEADS * HEAD_DIM) ** -0.5),
        'w_out': nrm(ks[18], (DEPTH, D_MODEL, D_MODEL), D_MODEL ** -0.5),
        'g_ffn2': gain(ks[19], (DEPTH, D_MODEL)),
        'w_up2': nrm(ks[20], (DEPTH, D_MODEL, 2 * D_FF), D_MODEL ** -0.5),
        'w_down2': nrm(ks[21], (DEPTH, D_FF, D_MODEL), D_FF ** -0.5),
    }


def reference(x_prompt, x_sample, cache_k, cache_v, cache_kidx, page_table,
              g_ffn1, w_up1, w_down1, g_mix, w_in, g_v, g_q, g_k, w_s, b_s,
              w_pa, w_pb, w_out, g_ffn2, w_up2, w_down2):
    xp, xs = x_prompt, x_sample
    kp_l, vp_l, kip_l, ks_l, vs_l, kis_l, vas_l = [], [], [], [], [], [], []
    for l in range(DEPTH):
        p = {'g_ffn1': g_ffn1[l], 'w_up1': w_up1[l], 'w_down1': w_down1[l], 'g_mix': g_mix[l],
             'w_in': w_in[l], 'g_v': g_v[l], 'g_q': g_q[l], 'g_k': g_k[l],
             'w_pa': w_pa[l], 'w_pb': w_pb[l], 'w_out': w_out[l],
             'g_ffn2': g_ffn2[l], 'w_up2': w_up2[l], 'w_down2': w_down2[l]}
        xp, kp, vp, kip, _ = _layer(xp, p, _prompt_mixer(w_s[l], b_s[l]))
        xs, kss, vss, kis, vas = _layer(
            xs, p, _sample_mixer(w_s[l], b_s[l], cache_k[l], cache_v[l], cache_kidx[l], page_table))
        kp_l.append(kp); vp_l.append(vp); kip_l.append(kip)
        ks_l.append(kss); vs_l.append(vss); kis_l.append(kis); vas_l.append(vas)
    return (xp, xs, jnp.stack(kp_l), jnp.stack(vp_l), jnp.stack(kip_l),
            jnp.stack(ks_l), jnp.stack(vs_l), jnp.stack(kis_l), jnp.stack(vas_l))
```

```python
import functools

import jax
import jax.numpy as jnp
import numpy as np
from jax import lax
from jax.experimental import pallas as pl
from jax.experimental.pallas import tpu as pltpu

D_MODEL = 1024
D_FF = 2816
CHUNK = 128
A_GROUPS = 8
A_WIDTH = D_MODEL
N_HEADS = 8
HEAD_DIM = 128
N_KV_HEADS = 2
KV_GROUP = N_HEADS // N_KV_HEADS
IDX_HEADS = 8
IDX_DIM = 64
TOPK_MAX = 256
Q_BLOCK = 128
PAGE_SIZE = 128
EPS = 1e-6

KV_W = N_KV_HEADS * HEAD_DIM
QI_W = IDX_HEADS * IDX_DIM
WA_W = 3 * D_MODEL + 2 * KV_W + QI_W
SMALL_W = 128
F_CHUNK = 256
KEY_CHUNK = 256
NEG = -1e30
INT_MIN = -2 ** 31
VMEM_LIMIT = 56 * 1024 * 1024

BF16 = jnp.bfloat16
F32 = jnp.float32


def _dot(a, b):
    return jnp.dot(a, b, preferred_element_type=F32)


def _dot_nt(a, b):
    return lax.dot_general(a, b, (((1,), (1,)), ((), ())), preferred_element_type=F32)


def _rms(x, g):
    return x * lax.rsqrt(jnp.mean(x * x, axis=-1, keepdims=True) + EPS) * g


def _sigmoid(x):
    return 1.0 / (1.0 + jnp.exp(-x))


def _gelu(x):
    c = np.float32(np.sqrt(2.0 / np.pi))
    return x * (0.5 * (1.0 + jnp.tanh(c * (x + 0.044715 * (x * x * x)))))


def _const_spec(shape):
    nd = len(shape)
    return pl.BlockSpec(shape, lambda *_: (0,) * nd, pipeline_mode=pl.Buffered(1))


def _params(n_axes):
    return pltpu.CompilerParams(dimension_semantics=("arbitrary",) * n_axes, vmem_limit_bytes=VMEM_LIMIT)


def _ffn_apply(x, g_ref, wup_ref, wdn_ref, h_ref):
    xb = _rms(x, g_ref[...]).astype(BF16)
    for j in range(D_FF // F_CHUNK):
        gate = _dot(xb, wup_ref[:, j * F_CHUNK:(j + 1) * F_CHUNK])
        up = _dot(xb, wup_ref[:, D_FF + j * F_CHUNK:D_FF + (j + 1) * F_CHUNK])
        h_ref[:, j * F_CHUNK:(j + 1) * F_CHUNK] = (gate * _sigmoid(gate) * up).astype(BF16)
    return x + 0.5 * _dot(h_ref[...], wdn_ref[...])


def _ffn_kernel(x_ref, g_ref, wup_ref, wdn_ref, o_ref, h_ref):
    o_ref[...] = _ffn_apply(x_ref[...], g_ref, wup_ref, wdn_ref, h_ref)


def _ffn(x, g, wup, wdn, tm):
    n = x.shape[0]
    return pl.pallas_call(
        _ffn_kernel,
        grid=(n // tm,),
        in_specs=[pl.BlockSpec((tm, D_MODEL), lambda i: (i, 0)),
                  _const_spec((1, D_MODEL)),
                  _const_spec((D_MODEL, 2 * D_FF)),
                  _const_spec((D_FF, D_MODEL))],
        out_specs=pl.BlockSpec((tm, D_MODEL), lambda i: (i, 0)),
        out_shape=jax.ShapeDtypeStruct((n, D_MODEL), F32),
        scratch_shapes=[pltpu.VMEM((tm, D_FF), BF16)],
        compiler_params=_params(1),
        name="ffn",
    )(x, g, wup, wdn)


def _inproj_kernel(x_ref, g_ref, wa_ref, ws_ref, wg_ref, gv_ref, gq_ref, gk_ref, *out_refs, prompt):
    if prompt:
        (u_ref, v_ref, k_ref, vv_ref, small_ref, sga_ref, sgb_ref,
         qt_ref, qit_ref, wit_ref, vt_ref, kb_ref, smallb_ref) = out_refs
    else:
        (u_ref, v_ref, k_ref, vv_ref, small_ref, sga_ref, sgb_ref, q_ref, qi_ref) = out_refs
    tm = x_ref.shape[0]
    hb = _rms(x_ref[...], g_ref[...]).astype(BF16)

    u_ref[...] = _gelu(_dot(hb, wa_ref[:, 0:D_MODEL])).astype(u_ref.dtype)
    v = _gelu(_dot(hb, wa_ref[:, D_MODEL:2 * D_MODEL]))
    v_ref[...] = _rms(v, gv_ref[...]).astype(v_ref.dtype)

    q = _dot(hb, wa_ref[:, 2 * D_MODEL:3 * D_MODEL])
    gq = gq_ref[...]
    for h in range(N_HEADS):
        qh = _rms(q[:, h * HEAD_DIM:(h + 1) * HEAD_DIM], gq)
        if prompt:
            qt_ref[h * HEAD_DIM:(h + 1) * HEAD_DIM, :] = qh.T.astype(BF16)
        else:
            q_ref[:, h * HEAD_DIM:(h + 1) * HEAD_DIM] = qh.astype(BF16)

    o = 3 * D_MODEL
    kk = _dot(hb, wa_ref[:, o:o + KV_W])
    gk = gk_ref[...]
    for j in range(N_KV_HEADS):
        kj = _rms(kk[:, j * HEAD_DIM:(j + 1) * HEAD_DIM], gk)
        k_ref[:, j * HEAD_DIM:(j + 1) * HEAD_DIM] = kj
        if prompt:
            kb_ref[:, j * HEAD_DIM:(j + 1) * HEAD_DIM] = kj.astype(BF16)

    vv = _dot(hb, wa_ref[:, o + KV_W:o + 2 * KV_W])
    vv_ref[...] = vv
    if prompt:
        for r in range(tm // KEY_CHUNK):
            vt_ref[r] = vv[r * KEY_CHUNK:(r + 1) * KEY_CHUNK, :].T.astype(BF16)

    qi = _dot(hb, wa_ref[:, o + 2 * KV_W:o + 2 * KV_W + QI_W])
    if prompt:
        qit_ref[...] = qi.T.astype(BF16)
    else:
        qi_ref[...] = qi.astype(BF16)

    small = _dot(hb, ws_ref[...])
    lane = lax.broadcasted_iota(jnp.int32, small.shape, 1)
    is_w = (lane >= IDX_DIM) & (lane < IDX_DIM + IDX_HEADS)
    small = small * jnp.where(is_w, np.float32(IDX_HEADS ** -0.5), np.float32(1.0))
    small_ref[...] = small
    if prompt:
        smallb_ref[...] = small.astype(BF16)
        wit_ref[...] = small.T[IDX_DIM:IDX_DIM + IDX_HEADS, :]

    sga_ref[...] = _sigmoid(_dot(hb, wg_ref[:, 0:D_MODEL])).astype(BF16)
    sgb_ref[...] = _sigmoid(_dot(hb, wg_ref[:, D_MODEL:2 * D_MODEL])).astype(BF16)


def _inproj(x, g, wa, ws, wg, gv, gq, gk, tm, prompt):
    n = x.shape[0]
    row = lambda w: pl.BlockSpec((tm, w), lambda i: (i, 0))
    col = lambda h: pl.BlockSpec((h, tm), lambda i: (0, i))
    out_shape = [jax.ShapeDtypeStruct((n, D_MODEL), BF16),
                 jax.ShapeDtypeStruct((n, A_WIDTH), BF16 if prompt else F32),
                 jax.ShapeDtypeStruct((n, KV_W), F32),
                 jax.ShapeDtypeStruct((n, KV_W), F32),
                 jax.ShapeDtypeStruct((n, SMALL_W), F32),
                 jax.ShapeDtypeStruct((n, D_MODEL), BF16),
                 jax.ShapeDtypeStruct((n, D_MODEL), BF16)]
    out_specs = [row(D_MODEL), row(A_WIDTH), row(KV_W), row(KV_W), row(SMALL_W), row(D_MODEL), row(D_MODEL)]
    if prompt:
        out_shape += [jax.ShapeDtypeStruct((D_MODEL, n), BF16),
                      jax.ShapeDtypeStruct((QI_W, n), BF16),
                      jax.ShapeDtypeStruct((IDX_HEADS, n), F32),
                      jax.ShapeDtypeStruct((n // KEY_CHUNK, KV_W, KEY_CHUNK), BF16),
                      jax.ShapeDtypeStruct((n, KV_W), BF16),
                      jax.ShapeDtypeStruct((n, SMALL_W), BF16)]
        out_specs += [col(D_MODEL), col(QI_W), col(IDX_HEADS),
                      pl.BlockSpec((tm // KEY_CHUNK, KV_W, KEY_CHUNK), lambda i: (i, 0, 0)),
                      row(KV_W), row(SMALL_W)]
    else:
        out_shape += [jax.ShapeDtypeStruct((n, D_MODEL), BF16),
                      jax.ShapeDtypeStruct((n, QI_W), BF16)]
        out_specs += [row(D_MODEL), row(QI_W)]
    return pl.pallas_call(
        functools.partial(_inproj_kernel, prompt=prompt),
        grid=(n // tm,),
        in_specs=[row(D_MODEL), _const_spec((1, D_MODEL)),
                  _const_spec((D_MODEL, WA_W)), _const_spec((D_MODEL, SMALL_W)), _const_spec((D_MODEL, 2 * D_MODEL)),
                  _const_spec((1, A_WIDTH)), _const_spec((1, HEAD_DIM)), _const_spec((1, HEAD_DIM))],
        out_specs=out_specs,
        out_shape=out_shape,
        compiler_params=_params(1),
        name="inproj_prompt" if prompt else "inproj_sample",
    )(x, g, wa, ws, wg, gv, gq, gk)


def _sort_key(x):
    bits = pltpu.bitcast(x, jnp.int32)
    bits = jnp.where(bits == INT_MIN, 0, bits)
    return jnp.where(bits < 0, bits ^ 0x7FFFFFFF, bits)


def _kth_threshold(count_ge, k, shape):
    def body(i, t_u):
        bit = jnp.left_shift(jnp.int32(1), 31 - i)
        cand_u = t_u | bit
        cnt = count_ge(cand_u ^ INT_MIN)
        return jnp.where(cnt >= k, cand_u, t_u)
    t_u = lax.fori_loop(0, 32, body, jnp.zeros(shape, jnp.int32))
    return t_u ^ INT_MIN


def _mix_kernel(qt_ref, qit_ref, wit_ref, kb_ref, vt_ref, smallb_ref, o_ref,
                keys_ref, m_ref, l_ref, acc_ref, *, k_sel):
    n = pl.program_id(1)
    t0 = n * Q_BLOCK
    n_chunks = (n * Q_BLOCK) // KEY_CHUNK + 1
    row = lax.broadcasted_iota(jnp.int32, (KEY_CHUNK, Q_BLOCK), 0)
    col = lax.broadcasted_iota(jnp.int32, (KEY_CHUNK, Q_BLOCK), 1)

    qcat = jnp.concatenate([qit_ref[h * IDX_DIM:(h + 1) * IDX_DIM, :] for h in range(IDX_HEADS)], axis=1)
    w = wit_ref[...]

    def score_chunk(c, carry):
        r0 = pl.multiple_of(c * KEY_CHUNK, KEY_CHUNK)
        ki = smallb_ref[pl.ds(r0, KEY_CHUNK), :][:, 0:IDX_DIM]
        dots = _dot(ki, qcat) * np.float32(IDX_DIM ** -0.5)
        sc = jnp.zeros((KEY_CHUNK, Q_BLOCK), F32)
        for h in range(IDX_HEADS):
            sc = sc + w[h:h + 1, :] * jnp.maximum(dots[:, h * Q_BLOCK:(h + 1) * Q_BLOCK], 0.0)
        causal = (row + r0) <= (col + t0)
        keys_ref[pl.ds(r0, KEY_CHUNK), :] = _sort_key(jnp.where(causal, sc, -jnp.inf))
        return carry

    lax.fori_loop(0, n_chunks, score_chunk, 0)

    def count(pred):
        def body(c, acc):
            r0 = pl.multiple_of(c * KEY_CHUNK, KEY_CHUNK)
            hit = jnp.where(pred(keys_ref[pl.ds(r0, KEY_CHUNK), :]), 1.0, 0.0)
            return acc + jnp.sum(hit, axis=0, keepdims=True)
        return lax.fori_loop(0, n_chunks, body, jnp.zeros((1, Q_BLOCK), F32))

    thr = _kth_threshold(lambda t: count(lambda kk: kk >= t), k_sel, (1, Q_BLOCK))
    need = k_sel - count(lambda kk: kk > thr)

    m_ref[...] = jnp.full(m_ref.shape, NEG, F32)
    l_ref[...] = jnp.zeros(l_ref.shape, F32)
    acc_ref[...] = jnp.zeros(acc_ref.shape, F32)
    lower = (lax.broadcasted_iota(jnp.int32, (KEY_CHUNK, KEY_CHUNK), 1)
             < lax.broadcasted_iota(jnp.int32, (KEY_CHUNK, KEY_CHUNK), 0))
    lower = jnp.where(lower, 1.0, 0.0).astype(BF16)
    scale = np.float32(HEAD_DIM ** -0.5)

    def attend_chunk(c, tie_carry):
        r0 = pl.multiple_of(c * KEY_CHUNK, KEY_CHUNK)
        kk = keys_ref[pl.ds(r0, KEY_CHUNK), :]
        eq = jnp.where(kk == thr, 1.0, 0.0)
        before = _dot(lower, eq.astype(BF16)) + tie_carry
        keep_tie = jnp.where(before < need, eq, 0.0)
        sel = jnp.where(kk > thr, 1.0, keep_tie)
        sel = jnp.where((row + r0) <= (col + t0), sel, 0.0)
        sel4 = jnp.concatenate([sel] * KV_GROUP, axis=1) > 0.5
        for j in range(N_KV_HEADS):
            qj = jnp.concatenate(
                [qt_ref[(j * KV_GROUP + g) * HEAD_DIM:(j * KV_GROUP + g + 1) * HEAD_DIM, :] for g in range(KV_GROUP)],
                axis=1)
            kj = kb_ref[pl.ds(r0, KEY_CHUNK), j * HEAD_DIM:(j + 1) * HEAD_DIM]
            s = jnp.where(sel4, _dot(kj, qj) * scale, NEG)
            m_old = m_ref[j]
            m_new = jnp.maximum(m_old, jnp.max(s, axis=0, keepdims=True))
            alpha = jnp.exp(m_old - m_new)
            p = jnp.where(sel4, jnp.exp(s - m_new), 0.0)
            l_ref[j] = alpha * l_ref[j] + jnp.sum(p, axis=0, keepdims=True)
            vj = vt_ref[c, j * HEAD_DIM:(j + 1) * HEAD_DIM, :]
            acc_ref[j] = alpha * acc_ref[j] + _dot(vj, p.astype(BF16))
            m_ref[j] = m_new
        return tie_carry + jnp.sum(eq, axis=0, keepdims=True)

    lax.fori_loop(0, n_chunks, attend_chunk, jnp.zeros((1, Q_BLOCK), F32))

    for j in range(N_KV_HEADS):
        ot = acc_ref[j] / l_ref[j]
        for g in range(KV_GROUP):
            h = j * KV_GROUP + g
            o_ref[:, h * HEAD_DIM:(h + 1) * HEAD_DIM] = ot[:, g * Q_BLOCK:(g + 1) * Q_BLOCK].T.astype(BF16)


def _mix(qt, qit, wit, kb, vt, smallb, batch, seq):
    n = batch * seq
    nq = seq // Q_BLOCK
    k_sel = min(TOPK_MAX, seq // 4)
    gq = KV_GROUP * Q_BLOCK
    return pl.pallas_call(
        functools.partial(_mix_kernel, k_sel=k_sel),
        grid=(batch, nq),
        in_specs=[pl.BlockSpec((D_MODEL, Q_BLOCK), lambda b, i: (0, b * nq + i)),
                  pl.BlockSpec((QI_W, Q_BLOCK), lambda b, i: (0, b * nq + i)),
                  pl.BlockSpec((IDX_HEADS, Q_BLOCK), lambda b, i: (0, b * nq + i)),
                  pl.BlockSpec((seq, KV_W), lambda b, i: (b, 0)),
                  pl.BlockSpec((seq // KEY_CHUNK, KV_W, KEY_CHUNK), lambda b, i: (b, 0, 0)),
                  pl.BlockSpec((seq, SMALL_W), lambda b, i: (b, 0))],
        out_specs=pl.BlockSpec((Q_BLOCK, D_MODEL), lambda b, i: (b * nq + i, 0)),
        out_shape=jax.ShapeDtypeStruct((n, D_MODEL), BF16),
        scratch_shapes=[pltpu.VMEM((seq, Q_BLOCK), jnp.int32),
                        pltpu.VMEM((N_KV_HEADS, 1, gq), F32),
                        pltpu.VMEM((N_KV_HEADS, 1, gq), F32),
                        pltpu.VMEM((N_KV_HEADS, HEAD_DIM, gq), F32)],
        compiler_params=_params(2),
        name="mix_prompt",
    )(qt, qit, wit, kb, vt, smallb)


S_IDX_PAGES = 16
S_ATT_PAGES = 8
T_PAD = 8


def _s_idx_kernel(pt_ref, qi_ref, wb_ref, kin_ref, *refs):
    page_refs, (out_ref, outn_ref) = refs[:S_IDX_PAGES], refs[S_IDX_PAGES:]
    qi = qi_ref[0]
    wb = wb_ref[0]

    def score(ki):
        r = jnp.maximum(_dot_nt(qi, ki) * np.float32(IDX_DIM ** -0.5), 0.0) * wb
        sc = r[0:T_PAD, :]
        for h in range(1, IDX_HEADS):
            sc = sc + r[h * T_PAD:(h + 1) * T_PAD, :]
        return sc

    for p in range(S_IDX_PAGES):
        out_ref[p] = score(page_refs[p][...].astype(BF16))

    @pl.when(pl.program_id(1) == 0)
    def _():
        outn_ref[...] = score(kin_ref[0])


def _s_idx(page_table, qi_rows, wb_rows, ki_new, cache_kidx, layer):
    db, n_pages = page_table.shape
    steps = n_pages // S_IDX_PAGES
    page_specs = [pl.BlockSpec((None, None, PAGE_SIZE, IDX_DIM),
                               lambda b, i, pt, p=p: (layer, pt[b, i * S_IDX_PAGES + p], 0, 0))
                  for p in range(S_IDX_PAGES)]
    grid_spec = pltpu.PrefetchScalarGridSpec(
        num_scalar_prefetch=1,
        grid=(db, steps),
        in_specs=[pl.BlockSpec((1, IDX_HEADS * T_PAD, IDX_DIM), lambda b, i, pt: (b, 0, 0)),
                  pl.BlockSpec((1, IDX_HEADS * T_PAD, PAGE_SIZE), lambda b, i, pt: (b, 0, 0)),
                  pl.BlockSpec((1, PAGE_SIZE, IDX_DIM), lambda b, i, pt: (b, 0, 0))] + page_specs,
        out_specs=[pl.BlockSpec((S_IDX_PAGES, T_PAD, PAGE_SIZE), lambda b, i, pt: (i, b, 0)),
                   pl.BlockSpec((T_PAD, PAGE_SIZE), lambda b, i, pt: (b, 0))])
    return pl.pallas_call(
        _s_idx_kernel,
        grid_spec=grid_spec,
        out_shape=[jax.ShapeDtypeStruct((n_pages, db * T_PAD, PAGE_SIZE), F32),
                   jax.ShapeDtypeStruct((db * T_PAD, PAGE_SIZE), F32)],
        compiler_params=_params(2),
        name="s_idx",
    )(page_table, qi_rows, wb_rows, ki_new, *([cache_kidx] * S_IDX_PAGES))


def _s_sel_kernel(sc_ref, scn_ref, mask_ref, maskn_ref, keys_ref, *, k_sel, n_new):
    n_pages, rows, _ = sc_ref.shape
    keys_ref[0:n_pages] = _sort_key(sc_ref[...])
    t = lax.broadcasted_iota(jnp.int32, (rows, PAGE_SIZE), 0) % T_PAD
    j = lax.broadcasted_iota(jnp.int32, (rows, PAGE_SIZE), 1)
    new_ok = (j <= t) & (j < n_new)
    keys_ref[n_pages] = _sort_key(jnp.where(new_ok, scn_ref[...], -jnp.inf))

    def count(pred):
        hit = jnp.sum(jnp.where(pred(keys_ref[...]), 1.0, 0.0), axis=0)
        return jnp.sum(hit, axis=1, keepdims=True)

    thr = _kth_threshold(lambda c: count(lambda kk: kk >= c), k_sel, (rows, 1))
    need = k_sel - count(lambda kk: kk > thr)

    upper = (lax.broadcasted_iota(jnp.int32, (PAGE_SIZE, PAGE_SIZE), 0)
             < lax.broadcasted_iota(jnp.int32, (PAGE_SIZE, PAGE_SIZE), 1))
    upper = jnp.where(upper, 1.0, 0.0).astype(BF16)
    ones = jnp.ones((PAGE_SIZE, PAGE_SIZE), BF16)

    def tile(c, carry):
        kk = keys_ref[c]
        eq = jnp.where(kk == thr, 1.0, 0.0)
        eqb = eq.astype(BF16)
        before = _dot(eqb, upper) + carry
        keep_tie = jnp.where(before < need, eq, 0.0)
        return jnp.where(kk > thr, 1.0, keep_tie), carry + _dot(eqb, ones)

    def past_tile(c, carry):
        sel, carry = tile(c, carry)
        mask_ref[c] = sel
        return carry

    carry = lax.fori_loop(0, n_pages, past_tile, jnp.zeros((rows, PAGE_SIZE), F32))
    sel, _ = tile(n_pages, carry)
    maskn_ref[...] = jnp.where(new_ok, sel, 0.0)


def _s_sel(scores, scores_new, k_sel, n_new, rows_per_step):
    n_pages, rows, _ = scores.shape
    return pl.pallas_call(
        functools.partial(_s_sel_kernel, k_sel=k_sel, n_new=n_new),
        grid=(rows // rows_per_step,),
        in_specs=[pl.BlockSpec((n_pages, rows_per_step, PAGE_SIZE), lambda i: (0, i, 0)),
                  pl.BlockSpec((rows_per_step, PAGE_SIZE), lambda i: (i, 0))],
        out_specs=[pl.BlockSpec((n_pages, rows_per_step, PAGE_SIZE), lambda i: (0, i, 0)),
                   pl.BlockSpec((rows_per_step, PAGE_SIZE), lambda i: (i, 0))],
        out_shape=[jax.ShapeDtypeStruct((n_pages, rows, PAGE_SIZE), F32),
                   jax.ShapeDtypeStruct((rows, PAGE_SIZE), F32)],
        scratch_shapes=[pltpu.VMEM((n_pages + 1, rows_per_step, PAGE_SIZE), jnp.int32)],
        compiler_params=_params(1),
        name="s_sel",
    )(scores, scores_new)


def _s_att_kernel(pt_ref, q_ref, mask_ref, maskn_ref, kn_ref, vn_ref, *refs):
    k_refs = refs[:S_ATT_PAGES]
    v_refs = refs[S_ATT_PAGES:2 * S_ATT_PAGES]
    o_ref, m_ref, l_ref, acc_ref = refs[2 * S_ATT_PAGES:]
    i = pl.program_id(1)
    rows = KV_GROUP * T_PAD
    scale = np.float32(HEAD_DIM ** -0.5)

    def update(j, kj, vj, sel):
        sel4 = jnp.concatenate([sel] * KV_GROUP, axis=0) > 0.5
        s = jnp.where(sel4, _dot_nt(q_ref[0, j * rows:(j + 1) * rows, :], kj) * scale, NEG)
        m_old = m_ref[j]
        m_new = jnp.maximum(m_old, jnp.max(s, axis=1, keepdims=True))
        alpha = jnp.exp(m_old - m_new)
        p = jnp.where(sel4, jnp.exp(s - m_new), 0.0)
        l_ref[j] = alpha * l_ref[j] + jnp.sum(p, axis=1, keepdims=True)
        acc_ref[j] = alpha * acc_ref[j] + _dot(p.astype(BF16), vj)
        m_ref[j] = m_new

    @pl.when(i == 0)
    def _():
        m_ref[...] = jnp.full(m_ref.shape, NEG, F32)
        l_ref[...] = jnp.zeros(l_ref.shape, F32)
        acc_ref[...] = jnp.zeros(acc_ref.shape, F32)
        for j in range(N_KV_HEADS):
            update(j, kn_ref[0, :, j * HEAD_DIM:(j + 1) * HEAD_DIM],
                   vn_ref[0, :, j * HEAD_DIM:(j + 1) * HEAD_DIM], maskn_ref[...])

    sel = jnp.concatenate([mask_ref[p] for p in range(S_ATT_PAGES)], axis=1)
    for j in range(N_KV_HEADS):
        kj = jnp.concatenate([k_refs[p][:, j, :] for p in range(S_ATT_PAGES)], axis=0).astype(BF16)
        vj = jnp.concatenate([v_refs[p][:, j, :] for p in range(S_ATT_PAGES)], axis=0).astype(BF16)
        update(j, kj, vj, sel)

    @pl.when(i == pl.num_programs(1) - 1)
    def _():
        for j in range(N_KV_HEADS):
            o_ref[0, j] = acc_ref[j] / l_ref[j]


def _s_att(page_table, q_rows, mask, mask_new, k_new, v_new, cache_k, cache_v, layer):
    db, n_pages = page_table.shape
    steps = n_pages // S_ATT_PAGES
    rows = KV_GROUP * T_PAD
    page_spec = lambda p: pl.BlockSpec(
        (None, None, PAGE_SIZE, N_KV_HEADS, HEAD_DIM),
        lambda b, i, pt: (layer, pt[b, i * S_ATT_PAGES + p], 0, 0, 0))
    grid_spec = pltpu.PrefetchScalarGridSpec(
        num_scalar_prefetch=1,
        grid=(db, steps),
        in_specs=[pl.BlockSpec((1, N_KV_HEADS * rows, HEAD_DIM), lambda b, i, pt: (b, 0, 0)),
                  pl.BlockSpec((S_ATT_PAGES, T_PAD, PAGE_SIZE), lambda b, i, pt: (i, b, 0)),
                  pl.BlockSpec((T_PAD, PAGE_SIZE), lambda b, i, pt: (b, 0)),
                  pl.BlockSpec((1, PAGE_SIZE, KV_W), lambda b, i, pt: (b, 0, 0)),
                  pl.BlockSpec((1, PAGE_SIZE, KV_W), lambda b, i, pt: (b, 0, 0))]
                 + [page_spec(p) for p in range(S_ATT_PAGES)] * 2,
        out_specs=pl.BlockSpec((1, N_KV_HEADS, rows, HEAD_DIM), lambda b, i, pt: (b, 0, 0, 0)),
        scratch_shapes=[pltpu.VMEM((N_KV_HEADS, rows, 1), F32),
                        pltpu.VMEM((N_KV_HEADS, rows, 1), F32),
                        pltpu.VMEM((N_KV_HEADS, rows, HEAD_DIM), F32)])
    return pl.pallas_call(
        _s_att_kernel,
        grid_spec=grid_spec,
        out_shape=jax.ShapeDtypeStruct((db, N_KV_HEADS, rows, HEAD_DIM), F32),
        compiler_params=_params(2),
        name="s_att",
    )(page_table, q_rows, mask, mask_new, k_new, v_new,
      *([cache_k] * S_ATT_PAGES), *([cache_v] * S_ATT_PAGES))


def _sample_mixer(page_table, q_b, qi_b, small_f, k_f, vv_f, cache_k, cache_v, cache_kidx, layer, dec_seq):
    db, n_pages = page_table.shape
    t = dec_seq
    past = n_pages * PAGE_SIZE
    k_sel = min(TOPK_MAX, (past + t) // 4)
    pad_t = lambda a: jnp.pad(a, [(0, 0)] * (a.ndim - 2) + [(0, T_PAD - t), (0, 0)])

    qi_rows = pad_t(qi_b.reshape(db, t, IDX_HEADS, IDX_DIM).transpose(0, 2, 1, 3))
    qi_rows = qi_rows.reshape(db, IDX_HEADS * T_PAD, IDX_DIM)
    wi = small_f[:, IDX_DIM:IDX_DIM + IDX_HEADS].reshape(db, t, IDX_HEADS).transpose(0, 2, 1)
    wb_rows = jnp.pad(wi, ((0, 0), (0, 0), (0, T_PAD - t))).reshape(db, IDX_HEADS * T_PAD, 1)
    wb_rows = jnp.broadcast_to(wb_rows, (db, IDX_HEADS * T_PAD, PAGE_SIZE))
    row_pad = lambda a: jnp.pad(a.reshape(db, t, a.shape[-1]), ((0, 0), (0, PAGE_SIZE - t), (0, 0)))
    ki_new = row_pad(small_f[:, 0:IDX_DIM]).astype(BF16)

    scores, scores_new = _s_idx(page_table, qi_rows, wb_rows, ki_new, cache_kidx, layer)
    rows_per_step = min(db * T_PAD, 64)
    mask, mask_new = _s_sel(scores, scores_new, k_sel, t, rows_per_step)

    q_rows = pad_t(q_b.reshape(db, t, N_KV_HEADS, KV_GROUP, HEAD_DIM).transpose(0, 2, 3, 1, 4))
    q_rows = q_rows.reshape(db, N_KV_HEADS * KV_GROUP * T_PAD, HEAD_DIM)
    o = _s_att(page_table, q_rows, mask, mask_new,
               row_pad(k_f).astype(BF16), row_pad(vv_f).astype(BF16), cache_k, cache_v, layer)
    o = o.reshape(db, N_KV_HEADS, KV_GROUP, T_PAD, HEAD_DIM)[:, :, :, :t]
    return o.transpose(0, 3, 1, 2, 4).reshape(db * t, N_HEADS * HEAD_DIM).astype(BF16)


def _post_kernel(x_ref, u_ref, v_ref, attn_ref, sga_ref, sgb_ref, ws_ref, bias_ref, wpa_ref, wpb_ref, wout_ref,
                 o_ref, a_ref, *, block):
    tm = x_ref.shape[0]
    r = lax.broadcasted_iota(jnp.int32, (CHUNK, CHUNK), 0)
    c = lax.broadcasted_iota(jnp.int32, (CHUNK, CHUNK), 1)
    allowed = (r >= c) & ((r // block) == (c // block))
    bias = bias_ref[...]
    for g in range(A_GROUPS):
        wg = jnp.where(allowed, ws_ref[g], 0.0).astype(BF16)
        for i in range(tm // CHUNK):
            rows = slice(i * CHUNK, (i + 1) * CHUNK)
            cols = slice(g * CHUNK, (g + 1) * CHUNK)
            sv = _dot(wg, v_ref[rows, cols]) + bias[:, cols]
            a_ref[rows, cols] = (u_ref[rows, cols].astype(F32) * sv).astype(BF16)
    merged = (sga_ref[...].astype(F32) * _dot(a_ref[...], wpa_ref[...])
              + sgb_ref[...].astype(F32) * _dot(attn_ref[...], wpb_ref[...]))
    o_ref[...] = x_ref[...] + _dot(merged.astype(BF16), wout_ref[...])


def _post(x, u, v, attn, sga, sgb, ws, bias, wpa, wpb, wout, tm, block):
    n = x.shape[0]
    row = lambda dt: pl.BlockSpec((tm, D_MODEL), lambda i: (i, 0))
    return pl.pallas_call(
        functools.partial(_post_kernel, block=block),
        grid=(n // tm,),
        in_specs=[row(F32), row(BF16), row(BF16), row(BF16), row(BF16), row(BF16),
                  _const_spec((A_GROUPS, CHUNK, CHUNK)), _const_spec((CHUNK, A_WIDTH)),
                  _const_spec((A_WIDTH, D_MODEL)), _const_spec((N_HEADS * HEAD_DIM, D_MODEL)),
                  _const_spec((D_MODEL, D_MODEL))],
        out_specs=row(F32),
        out_shape=jax.ShapeDtypeStruct((n, D_MODEL), F32),
        scratch_shapes=[pltpu.VMEM((tm, A_WIDTH), BF16)],
        compiler_params=_params(1),
        name="post",
    )(x, u, v, attn, sga, sgb, ws, bias, wpa, wpb, wout)


def kernel(x_prompt, x_sample, cache_k, cache_v, cache_kidx, page_table, g_ffn1, w_up1, w_down1, g_mix, w_in, g_v, g_q, g_k, w_s, b_s, w_pa, w_pb, w_out, g_ffn2, w_up2, w_down2):
    batch, seq, _ = x_prompt.shape
    db, dec_seq, _ = x_sample.shape
    depth = w_in.shape[0]
    n_p, n_s = batch * seq, db * dec_seq
    assert seq % KEY_CHUNK == 0 and n_s % CHUNK == 0 and CHUNK % dec_seq == 0 and dec_seq <= T_PAD
    tm_p = 512 if n_p % 512 == 0 else KEY_CHUNK
    tm_s = CHUNK

    xp = x_prompt.reshape(n_p, D_MODEL)
    xs = x_sample.reshape(n_s, D_MODEL)
    outs = [[] for _ in range(7)]
    for l in range(depth):
        row = lambda a: a[l].reshape(1, -1)
        wup1, wdn1 = w_up1[l].astype(BF16), w_down1[l].astype(BF16)
        wup2, wdn2 = w_up2[l].astype(BF16), w_down2[l].astype(BF16)
        wa = w_in[l, :, 0:WA_W].astype(BF16)
        n_small = IDX_DIM + IDX_HEADS
        ws = jnp.pad(w_in[l, :, WA_W:WA_W + n_small], ((0, 0), (0, SMALL_W - n_small))).astype(BF16)
        wg = w_in[l, :, WA_W + n_small:].astype(BF16)
        wpa, wpb, wout = w_pa[l].astype(BF16), w_pb[l].astype(BF16), w_out[l].astype(BF16)
        ws_p = w_s[l]
        bias_p = jnp.repeat(b_s[l].T, CHUNK, axis=1)
        ws_s = jnp.tile(w_s[l, :, 0:dec_seq, 0:dec_seq], (1, CHUNK // dec_seq, CHUNK // dec_seq))
        bias_s = jnp.repeat(jnp.tile(b_s[l, :, 0:dec_seq].T, (CHUNK // dec_seq, 1)), CHUNK, axis=1)

        x1 = _ffn(xp, row(g_ffn1), wup1, wdn1, tm_p)
        (u, v, k_f, vv_f, small_f, sga, sgb, qt, qit, wit, vt, kb, smallb) = _inproj(
            x1, row(g_mix), wa, ws, wg, row(g_v), row(g_q), row(g_k), tm_p, True)
        attn = _mix(qt, qit, wit, kb, vt, smallb, batch, seq)
        x2 = _post(x1, u, v, attn, sga, sgb, ws_p, bias_p, wpa, wpb, wout, tm_p, CHUNK)
        xp = _ffn(x2, row(g_ffn2), wup2, wdn2, tm_p)
        outs[0].append(k_f.reshape(batch, seq, N_KV_HEADS, HEAD_DIM))
        outs[1].append(vv_f.reshape(batch, seq, N_KV_HEADS, HEAD_DIM))
        outs[2].append(small_f[:, 0:IDX_DIM].reshape(batch, seq, IDX_DIM))

        x1 = _ffn(xs, row(g_ffn1), wup1, wdn1, tm_s)
        (u, v, k_f, vv_f, small_f, sga, sgb, q_b, qi_b) = _inproj(
            x1, row(g_mix), wa, ws, wg, row(g_v), row(g_q), row(g_k), tm_s, False)
        attn = _sample_mixer(page_table, q_b, qi_b, small_f, k_f, vv_f, cache_k, cache_v, cache_kidx, l, dec_seq)
        x2 = _post(x1, u.astype(BF16), v.astype(BF16), attn, sga, sgb, ws_s, bias_s, wpa, wpb, wout, tm_s, dec_seq)
        xs = _ffn(x2, row(g_ffn2), wup2, wdn2, tm_s)
        outs[3].append(k_f.reshape(db, dec_seq, N_KV_HEADS, HEAD_DIM))
        outs[4].append(vv_f.reshape(db, dec_seq, N_KV_HEADS, HEAD_DIM))
        outs[5].append(small_f[:, 0:IDX_DIM].reshape(db, dec_seq, IDX_DIM))
        outs[6].append(v.reshape(db, dec_seq, A_WIDTH))

    return (xp.reshape(batch, seq, D_MODEL), xs.reshape(db, dec_seq, D_MODEL),
            *[jnp.stack(o) for o in outs])
```

```python
import functools

import jax
import jax.numpy as jnp
import numpy as np
from jax import lax
from jax.experimental import pallas as pl
from jax.experimental.pallas import tpu as pltpu

D_MODEL = 1024
D_FF = 2816
CHUNK = 128
A_GROUPS = 8
A_WIDTH = D_MODEL
N_HEADS = 8
HEAD_DIM = 128
N_KV_HEADS = 2
KV_GROUP = N_HEADS // N_KV_HEADS
IDX_HEADS = 8
IDX_DIM = 64
TOPK_MAX = 256
Q_BLOCK = 128
PAGE_SIZE = 128
EPS = 1e-6

KV_W = N_KV_HEADS * HEAD_DIM
QI_W = IDX_HEADS * IDX_DIM
WA_W = 3 * D_MODEL + 2 * KV_W + QI_W
SMALL_W = 128
F_CHUNK = 256
KEY_CHUNK = 256
SEARCH_CHUNK = 512
ONES_ROWS = 16
NEG = -1e30
INT_MIN = -2 ** 31
KEY_NEG_INF = INT_MIN + 0x7FFFFF
VMEM_LIMIT = 56 * 1024 * 1024

BF16 = jnp.bfloat16
F32 = jnp.float32


def _dot(a, b):
    return jnp.dot(a, b, preferred_element_type=F32)


def _dot_nt(a, b):
    return lax.dot_general(a, b, (((1,), (1,)), ((), ())), preferred_element_type=F32)


def _rms(x, g):
    return x * lax.rsqrt(jnp.mean(x * x, axis=-1, keepdims=True) + EPS) * g


def _sigmoid(x):
    return 1.0 / (1.0 + jnp.exp(-x))


def _gelu(x):
    c = np.float32(np.sqrt(2.0 / np.pi))
    return x * (0.5 * (1.0 + jnp.tanh(c * (x + 0.044715 * (x * x * x)))))


def _const_spec(shape):
    nd = len(shape)
    return pl.BlockSpec(shape, lambda *_: (0,) * nd, pipeline_mode=pl.Buffered(1))


def _params(n_axes):
    return pltpu.CompilerParams(dimension_semantics=("arbitrary",) * n_axes, vmem_limit_bytes=VMEM_LIMIT)


def _ffn_apply(x, g_ref, wup_ref, wdn_ref, h_ref):
    xb = _rms(x, g_ref[...]).astype(BF16)
    for j in range(D_FF // F_CHUNK):
        gate = _dot(xb, wup_ref[:, j * F_CHUNK:(j + 1) * F_CHUNK])
        up = _dot(xb, wup_ref[:, D_FF + j * F_CHUNK:D_FF + (j + 1) * F_CHUNK])
        h_ref[:, j * F_CHUNK:(j + 1) * F_CHUNK] = (gate * _sigmoid(gate) * up).astype(BF16)
    return x + 0.5 * _dot(h_ref[...], wdn_ref[...])


def _ffn_kernel(x_ref, g_ref, wup_ref, wdn_ref, o_ref, h_ref):
    o_ref[...] = _ffn_apply(x_ref[...], g_ref, wup_ref, wdn_ref, h_ref)


def _ffn(x, g, wup, wdn, tm):
    n = x.shape[0]
    return pl.pallas_call(
        _ffn_kernel,
        grid=(n // tm,),
        in_specs=[pl.BlockSpec((tm, D_MODEL), lambda i: (i, 0)),
                  _const_spec((1, D_MODEL)),
                  _const_spec((D_MODEL, 2 * D_FF)),
                  _const_spec((D_FF, D_MODEL))],
        out_specs=pl.BlockSpec((tm, D_MODEL), lambda i: (i, 0)),
        out_shape=jax.ShapeDtypeStruct((n, D_MODEL), F32),
        scratch_shapes=[pltpu.VMEM((tm, D_FF), BF16)],
        compiler_params=_params(1),
        name="ffn",
    )(x, g, wup, wdn)


def _inproj_kernel(x_ref, g_ref, wa_ref, ws_ref, wg_ref, gv_ref, gq_ref, gk_ref, *out_refs, prompt):
    if prompt:
        (u_ref, v_ref, k_ref, vv_ref, small_ref, sga_ref, sgb_ref,
         qt_ref, qit_ref, wit_ref, vt_ref, kb_ref, smallb_ref) = out_refs
    else:
        (u_ref, v_ref, k_ref, vv_ref, small_ref, sga_ref, sgb_ref, q_ref, qi_ref) = out_refs
    tm = x_ref.shape[0]
    hb = _rms(x_ref[...], g_ref[...]).astype(BF16)

    u_ref[...] = _gelu(_dot(hb, wa_ref[:, 0:D_MODEL])).astype(u_ref.dtype)
    v = _gelu(_dot(hb, wa_ref[:, D_MODEL:2 * D_MODEL]))
    v_ref[...] = _rms(v, gv_ref[...]).astype(v_ref.dtype)

    q = _dot(hb, wa_ref[:, 2 * D_MODEL:3 * D_MODEL])
    gq = gq_ref[...]
    for h in range(N_HEADS):
        qh = _rms(q[:, h * HEAD_DIM:(h + 1) * HEAD_DIM], gq)
        if prompt:
            qt_ref[h * HEAD_DIM:(h + 1) * HEAD_DIM, :] = qh.T.astype(BF16)
        else:
            q_ref[:, h * HEAD_DIM:(h + 1) * HEAD_DIM] = qh.astype(BF16)

    o = 3 * D_MODEL
    kk = _dot(hb, wa_ref[:, o:o + KV_W])
    gk = gk_ref[...]
    for j in range(N_KV_HEADS):
        kj = _rms(kk[:, j * HEAD_DIM:(j + 1) * HEAD_DIM], gk)
        k_ref[:, j * HEAD_DIM:(j + 1) * HEAD_DIM] = kj
        if prompt:
            kb_ref[:, j * HEAD_DIM:(j + 1) * HEAD_DIM] = kj.astype(BF16)

    vv = _dot(hb, wa_ref[:, o + KV_W:o + 2 * KV_W])
    vv_ref[...] = vv
    if prompt:
        for r in range(tm // KEY_CHUNK):
            vt_ref[r] = vv[r * KEY_CHUNK:(r + 1) * KEY_CHUNK, :].T.astype(BF16)

    qi = _dot(hb, wa_ref[:, o + 2 * KV_W:o + 2 * KV_W + QI_W])
    if prompt:
        qit_ref[...] = qi.T.astype(BF16)
    else:
        qi_ref[...] = qi.astype(BF16)

    small = _dot(hb, ws_ref[...])
    lane = lax.broadcasted_iota(jnp.int32, small.shape, 1)
    is_w = (lane >= IDX_DIM) & (lane < IDX_DIM + IDX_HEADS)
    small = small * jnp.where(is_w, np.float32(IDX_HEADS ** -0.5), np.float32(1.0))
    small_ref[...] = small
    if prompt:
        smallb_ref[...] = small.astype(BF16)
        wit_ref[...] = small.T[IDX_DIM:IDX_DIM + IDX_HEADS, :]

    sga_ref[...] = _sigmoid(_dot(hb, wg_ref[:, 0:D_MODEL])).astype(BF16)
    sgb_ref[...] = _sigmoid(_dot(hb, wg_ref[:, D_MODEL:2 * D_MODEL])).astype(BF16)


def _inproj(x, g, wa, ws, wg, gv, gq, gk, tm, prompt):
    n = x.shape[0]
    row = lambda w: pl.BlockSpec((tm, w), lambda i: (i, 0))
    col = lambda h: pl.BlockSpec((h, tm), lambda i: (0, i))
    out_shape = [jax.ShapeDtypeStruct((n, D_MODEL), BF16),
                 jax.ShapeDtypeStruct((n, A_WIDTH), BF16 if prompt else F32),
                 jax.ShapeDtypeStruct((n, KV_W), F32),
                 jax.ShapeDtypeStruct((n, KV_W), F32),
                 jax.ShapeDtypeStruct((n, SMALL_W), F32),
                 jax.ShapeDtypeStruct((n, D_MODEL), BF16),
                 jax.ShapeDtypeStruct((n, D_MODEL), BF16)]
    out_specs = [row(D_MODEL), row(A_WIDTH), row(KV_W), row(KV_W), row(SMALL_W), row(D_MODEL), row(D_MODEL)]
    if prompt:
        out_shape += [jax.ShapeDtypeStruct((D_MODEL, n), BF16),
                      jax.ShapeDtypeStruct((QI_W, n), BF16),
                      jax.ShapeDtypeStruct((IDX_HEADS, n), F32),
                      jax.ShapeDtypeStruct((n // KEY_CHUNK, KV_W, KEY_CHUNK), BF16),
                      jax.ShapeDtypeStruct((n, KV_W), BF16),
                      jax.ShapeDtypeStruct((n, SMALL_W), BF16)]
        out_specs += [col(D_MODEL), col(QI_W), col(IDX_HEADS),
                      pl.BlockSpec((tm // KEY_CHUNK, KV_W, KEY_CHUNK), lambda i: (i, 0, 0)),
                      row(KV_W), row(SMALL_W)]
    else:
        out_shape += [jax.ShapeDtypeStruct((n, D_MODEL), BF16),
                      jax.ShapeDtypeStruct((n, QI_W), BF16)]
        out_specs += [row(D_MODEL), row(QI_W)]
    return pl.pallas_call(
        functools.partial(_inproj_kernel, prompt=prompt),
        grid=(n // tm,),
        in_specs=[row(D_MODEL), _const_spec((1, D_MODEL)),
                  _const_spec((D_MODEL, WA_W)), _const_spec((D_MODEL, SMALL_W)), _const_spec((D_MODEL, 2 * D_MODEL)),
                  _const_spec((1, A_WIDTH)), _const_spec((1, HEAD_DIM)), _const_spec((1, HEAD_DIM))],
        out_specs=out_specs,
        out_shape=out_shape,
        compiler_params=_params(1),
        name="inproj_prompt" if prompt else "inproj_sample",
    )(x, g, wa, ws, wg, gv, gq, gk)


def _sort_key(x):
    bits = pltpu.bitcast(x, jnp.int32)
    bits = jnp.where(bits == INT_MIN, 0, bits)
    return jnp.where(bits < 0, bits ^ 0x7FFFFFFF, bits)


def _kth_threshold(count_ge, k, shape):
    def body(i, t_u):
        cand_u = t_u | jnp.left_shift(jnp.int32(1), 31 - i)
        return jnp.where(count_ge(cand_u ^ INT_MIN) >= k, cand_u, t_u)
    t_u = lax.fori_loop(0, 32, body, jnp.zeros(shape, jnp.int32))
    return t_u ^ INT_MIN


def _count_rows(keys_ref, rows, pred):
    lanes = keys_ref.shape[1]
    n_acc = 8
    accs = [jnp.zeros((8, lanes), F32) for _ in range(n_acc)]
    for i in range(rows // 8):
        hit = jnp.where(pred(keys_ref[i * 8:(i + 1) * 8, :]), 1.0, 0.0)
        accs[i % n_acc] = accs[i % n_acc] + hit
    while len(accs) > 1:
        accs = [a + b for a, b in zip(accs[0::2], accs[1::2])]
    return jnp.sum(accs[0], axis=0, keepdims=True)


def _mix_kernel(qt_ref, qit_ref, wit_ref, kb_ref, vt_ref, smallb_ref, o_ref,
                keys_ref, bias_ref, thr_ref, need_ref, flag_ref, m_ref, acc_ref, *, k_sel):
    seq = keys_ref.shape[0]
    n = pl.program_id(1)
    t0 = n * Q_BLOCK
    n_chunks = (n * Q_BLOCK) // KEY_CHUNK + 1
    n_search = (n * Q_BLOCK) // SEARCH_CHUNK + 1
    row = lax.broadcasted_iota(jnp.int32, (KEY_CHUNK, Q_BLOCK), 0)
    col = lax.broadcasted_iota(jnp.int32, (KEY_CHUNK, Q_BLOCK), 1)

    qcat = jnp.concatenate([qit_ref[h * IDX_DIM:(h + 1) * IDX_DIM, :] for h in range(IDX_HEADS)], axis=1)
    w = wit_ref[...] * np.float32(IDX_DIM ** -0.5)

    def score_chunk(c, carry):
        r0 = pl.multiple_of(c * KEY_CHUNK, KEY_CHUNK)
        ki = smallb_ref[pl.ds(r0, KEY_CHUNK), :][:, 0:IDX_DIM]
        dots = _dot(ki, qcat)
        sc = w[0:1, :] * jnp.maximum(dots[:, 0:Q_BLOCK], 0.0)
        for h in range(1, IDX_HEADS):
            sc = sc + w[h:h + 1, :] * jnp.maximum(dots[:, h * Q_BLOCK:(h + 1) * Q_BLOCK], 0.0)
        causal = (row + r0) <= (col + t0)
        keys_ref[pl.ds(r0, KEY_CHUNK), :] = _sort_key(jnp.where(causal, sc, -jnp.inf))
        return carry

    lax.fori_loop(0, n_chunks, score_chunk, 0)

    @pl.when(n_chunks * KEY_CHUNK < n_search * SEARCH_CHUNK)
    def _():
        r0 = pl.multiple_of(n_chunks * KEY_CHUNK, KEY_CHUNK)
        keys_ref[pl.ds(r0, KEY_CHUNK), :] = jnp.full((KEY_CHUNK, Q_BLOCK), KEY_NEG_INF, jnp.int32)

    for ns in range(1, seq // SEARCH_CHUNK + 1):
        @pl.when(n_search == ns)
        def _(ns=ns):
            rows = ns * SEARCH_CHUNK
            thr = _kth_threshold(lambda t: _count_rows(keys_ref, rows, lambda kk: kk >= t), k_sel, (1, Q_BLOCK))
            n_gt = _count_rows(keys_ref, rows, lambda kk: kk > thr)
            n_ge = _count_rows(keys_ref, rows, lambda kk: kk >= thr)
            need = k_sel - n_gt
            tie = jnp.where((thr > KEY_NEG_INF) & (n_ge - n_gt > need), 1.0, 0.0)
            thr_ref[...] = thr
            need_ref[...] = need
            flag_ref[0] = jnp.max(tie)

    thr = thr_ref[...]
    need = need_ref[...]
    n_bias = n_search * (SEARCH_CHUNK // KEY_CHUNK)

    @pl.when(flag_ref[0] == 0.0)
    def _():
        thr_eff = jnp.maximum(thr, KEY_NEG_INF + 1)

        def body(c, carry):
            r0 = pl.multiple_of(c * KEY_CHUNK, KEY_CHUNK)
            bias_ref[pl.ds(r0, KEY_CHUNK), :] = jnp.where(keys_ref[pl.ds(r0, KEY_CHUNK), :] >= thr_eff, 0.0, NEG)
            return carry
        lax.fori_loop(0, n_bias, body, 0)

    @pl.when(flag_ref[0] != 0.0)
    def _():
        lower = (lax.broadcasted_iota(jnp.int32, (KEY_CHUNK, KEY_CHUNK), 1)
                 < lax.broadcasted_iota(jnp.int32, (KEY_CHUNK, KEY_CHUNK), 0))
        lower = jnp.where(lower, 1.0, 0.0).astype(BF16)

        def body(c, tie_carry):
            r0 = pl.multiple_of(c * KEY_CHUNK, KEY_CHUNK)
            kk = keys_ref[pl.ds(r0, KEY_CHUNK), :]
            eq = jnp.where(kk == thr, 1.0, 0.0)
            before = _dot(lower, eq.astype(BF16)) + tie_carry
            keep_tie = jnp.where(before < need, eq, 0.0)
            sel = jnp.where(kk > thr, 1.0, keep_tie)
            sel = jnp.where((row + r0) <= (col + t0), sel, 0.0)
            bias_ref[pl.ds(r0, KEY_CHUNK), :] = jnp.where(sel > 0.5, 0.0, NEG)
            return tie_carry + jnp.sum(eq, axis=0, keepdims=True)
        lax.fori_loop(0, n_bias, body, jnp.zeros((1, Q_BLOCK), F32))

    c2 = np.float32(HEAD_DIM ** -0.5 * np.log2(np.e))
    m_ref[...] = jnp.full(m_ref.shape, NEG, F32)
    acc_ref[...] = jnp.zeros(acc_ref.shape, F32)
    ones_rows = jnp.ones((ONES_ROWS, KEY_CHUNK), BF16)
    sub = SEARCH_CHUNK // KEY_CHUNK

    def attend(c, carry):
        r0 = pl.multiple_of(c * SEARCH_CHUNK, SEARCH_CHUNK)
        bias = bias_ref[pl.ds(r0, SEARCH_CHUNK), :]
        bias4 = jnp.concatenate([bias] * KV_GROUP, axis=1)
        for j in range(N_KV_HEADS):
            qj = jnp.concatenate(
                [qt_ref[(j * KV_GROUP + g) * HEAD_DIM:(j * KV_GROUP + g + 1) * HEAD_DIM, :]
                 for g in range(KV_GROUP)], axis=1)
            kj = kb_ref[pl.ds(r0, SEARCH_CHUNK), j * HEAD_DIM:(j + 1) * HEAD_DIM]
            s = _dot(kj, qj) + bias4
            m_old = m_ref[j]
            m_new = jnp.maximum(m_old, jnp.max(s, axis=0, keepdims=True))
            alpha = jnp.exp2((m_old - m_new) * c2)
            p = jnp.exp2((s - m_new) * c2).astype(BF16)
            acc = alpha * acc_ref[j]
            for r in range(sub):
                vj = jnp.concatenate([vt_ref[c * sub + r, j * HEAD_DIM:(j + 1) * HEAD_DIM, :], ones_rows], axis=0)
                acc = acc + _dot(vj, p[r * KEY_CHUNK:(r + 1) * KEY_CHUNK, :])
            acc_ref[j] = acc
            m_ref[j] = m_new
        return carry

    lax.fori_loop(0, n_search, attend, 0)

    for j in range(N_KV_HEADS):
        ot = acc_ref[j, 0:HEAD_DIM, :] / acc_ref[j, HEAD_DIM:HEAD_DIM + 1, :]
        for g in range(KV_GROUP):
            h = j * KV_GROUP + g
            o_ref[:, h * HEAD_DIM:(h + 1) * HEAD_DIM] = ot[:, g * Q_BLOCK:(g + 1) * Q_BLOCK].T.astype(BF16)


def _mix(qt, qit, wit, kb, vt, smallb, batch, seq):
    n = batch * seq
    nq = seq // Q_BLOCK
    k_sel = min(TOPK_MAX, seq // 4)
    gq = KV_GROUP * Q_BLOCK
    return pl.pallas_call(
        functools.partial(_mix_kernel, k_sel=k_sel),
        grid=(batch, nq),
        in_specs=[pl.BlockSpec((D_MODEL, Q_BLOCK), lambda b, i: (0, b * nq + i)),
                  pl.BlockSpec((QI_W, Q_BLOCK), lambda b, i: (0, b * nq + i)),
                  pl.BlockSpec((IDX_HEADS, Q_BLOCK), lambda b, i: (0, b * nq + i)),
                  pl.BlockSpec((seq, KV_W), lambda b, i: (b, 0)),
                  pl.BlockSpec((seq // KEY_CHUNK, KV_W, KEY_CHUNK), lambda b, i: (b, 0, 0)),
                  pl.BlockSpec((seq, SMALL_W), lambda b, i: (b, 0))],
        out_specs=pl.BlockSpec((Q_BLOCK, D_MODEL), lambda b, i: (b * nq + i, 0)),
        out_shape=jax.ShapeDtypeStruct((n, D_MODEL), BF16),
        scratch_shapes=[pltpu.VMEM((seq, Q_BLOCK), jnp.int32),
                        pltpu.VMEM((seq, Q_BLOCK), F32),
                        pltpu.VMEM((1, Q_BLOCK), jnp.int32),
                        pltpu.VMEM((1, Q_BLOCK), F32),
                        pltpu.SMEM((1,), F32),
                        pltpu.VMEM((N_KV_HEADS, 1, gq), F32),
                        pltpu.VMEM((N_KV_HEADS, HEAD_DIM + ONES_ROWS, gq), F32)],
        compiler_params=_params(2),
        name="mix_prompt",
    )(qt, qit, wit, kb, vt, smallb)


S_IDX_PAGES = 32
S_ATT_PAGES = 32
S_ATT_STREAMS = 1
T_PAD = 8


def _s_idx_kernel(pt_ref, qi_ref, wb_ref, kin_ref, *refs):
    page_refs, (out_ref, outn_ref) = refs[:S_IDX_PAGES], refs[S_IDX_PAGES:]
    qi = qi_ref[0]
    wb = wb_ref[0]

    def score(kit):
        r = jnp.maximum(_dot(qi, kit) * np.float32(IDX_DIM ** -0.5), 0.0) * wb
        sc = r[0:T_PAD, :]
        for h in range(1, IDX_HEADS):
            sc = sc + r[h * T_PAD:(h + 1) * T_PAD, :]
        return sc

    for p in range(S_IDX_PAGES):
        out_ref[p] = score(page_refs[p][...].astype(BF16))

    @pl.when(pl.program_id(1) == 0)
    def _():
        outn_ref[...] = score(kin_ref[0])


def _s_idx(page_table, qi_rows, wb_rows, kit_new, cache_kidx_t, layer):
    db, n_pages = page_table.shape
    steps = n_pages // S_IDX_PAGES
    page_specs = [pl.BlockSpec((None, None, IDX_DIM, PAGE_SIZE),
                               lambda b, i, pt, p=p: (layer, pt[b, i * S_IDX_PAGES + p], 0, 0))
                  for p in range(S_IDX_PAGES)]
    grid_spec = pltpu.PrefetchScalarGridSpec(
        num_scalar_prefetch=1,
        grid=(db, steps),
        in_specs=[pl.BlockSpec((1, IDX_HEADS * T_PAD, IDX_DIM), lambda b, i, pt: (b, 0, 0)),
                  pl.BlockSpec((1, IDX_HEADS * T_PAD, PAGE_SIZE), lambda b, i, pt: (b, 0, 0)),
                  pl.BlockSpec((1, IDX_DIM, PAGE_SIZE), lambda b, i, pt: (b, 0, 0))] + page_specs,
        out_specs=[pl.BlockSpec((S_IDX_PAGES, T_PAD, PAGE_SIZE), lambda b, i, pt: (i, b, 0)),
                   pl.BlockSpec((T_PAD, PAGE_SIZE), lambda b, i, pt: (b, 0))])
    return pl.pallas_call(
        _s_idx_kernel,
        grid_spec=grid_spec,
        out_shape=[jax.ShapeDtypeStruct((n_pages, db * T_PAD, PAGE_SIZE), F32),
                   jax.ShapeDtypeStruct((db * T_PAD, PAGE_SIZE), F32)],
        compiler_params=_params(2),
        name="s_idx",
    )(page_table, qi_rows, wb_rows, kit_new, *([cache_kidx_t] * S_IDX_PAGES))


def _s_sel_kernel(sc_ref, scn_ref, mask_ref, maskn_ref, keys_ref, *, k_sel, n_new):
    n_pages, rows, _ = sc_ref.shape
    keys_ref[0:n_pages] = _sort_key(sc_ref[...])
    t = lax.broadcasted_iota(jnp.int32, (rows, PAGE_SIZE), 0) % T_PAD
    j = lax.broadcasted_iota(jnp.int32, (rows, PAGE_SIZE), 1)
    new_ok = (j <= t) & (j < n_new)
    keys_ref[n_pages] = _sort_key(jnp.where(new_ok, scn_ref[...], -jnp.inf))

    def count(pred):
        hit = jnp.sum(jnp.where(pred(keys_ref[...]), 1.0, 0.0), axis=0)
        return jnp.sum(hit, axis=1, keepdims=True)

    thr = _kth_threshold(lambda c: count(lambda kk: kk >= c), k_sel, (rows, 1))
    need = k_sel - count(lambda kk: kk > thr)

    upper = (lax.broadcasted_iota(jnp.int32, (PAGE_SIZE, PAGE_SIZE), 0)
             < lax.broadcasted_iota(jnp.int32, (PAGE_SIZE, PAGE_SIZE), 1))
    upper = jnp.where(upper, 1.0, 0.0).astype(BF16)
    ones = jnp.ones((PAGE_SIZE, PAGE_SIZE), BF16)

    def tile(c, carry):
        kk = keys_ref[c]
        eq = jnp.where(kk == thr, 1.0, 0.0)
        eqb = eq.astype(BF16)
        before = _dot(eqb, upper) + carry
        keep_tie = jnp.where(before < need, eq, 0.0)
        return jnp.where(kk > thr, 1.0, keep_tie), carry + _dot(eqb, ones)

    def past_tile(c, carry):
        sel, carry = tile(c, carry)
        mask_ref[c] = sel
        return carry

    carry = lax.fori_loop(0, n_pages, past_tile, jnp.zeros((rows, PAGE_SIZE), F32))
    sel, _ = tile(n_pages, carry)
    maskn_ref[...] = jnp.where(new_ok, sel, 0.0)


def _s_sel(scores, scores_new, k_sel, n_new, rows_per_step):
    n_pages, rows, _ = scores.shape
    return pl.pallas_call(
        functools.partial(_s_sel_kernel, k_sel=k_sel, n_new=n_new),
        grid=(rows // rows_per_step,),
        in_specs=[pl.BlockSpec((n_pages, rows_per_step, PAGE_SIZE), lambda i: (0, i, 0)),
                  pl.BlockSpec((rows_per_step, PAGE_SIZE), lambda i: (i, 0))],
        out_specs=[pl.BlockSpec((n_pages, rows_per_step, PAGE_SIZE), lambda i: (0, i, 0)),
                   pl.BlockSpec((rows_per_step, PAGE_SIZE), lambda i: (i, 0))],
        out_shape=[jax.ShapeDtypeStruct((n_pages, rows, PAGE_SIZE), F32),
                   jax.ShapeDtypeStruct((rows, PAGE_SIZE), F32)],
        scratch_shapes=[pltpu.VMEM((n_pages + 1, rows_per_step, PAGE_SIZE), jnp.int32)],
        compiler_params=_params(1),
        name="s_sel",
    )(scores, scores_new)


def _s_att_kernel(pt_ref, q_ref, mask_ref, maskn_ref, kn_ref, vn_ref, *refs):
    k_refs = refs[:S_ATT_PAGES]
    v_refs = refs[S_ATT_PAGES:2 * S_ATT_PAGES]
    o_ref, m_ref, acc_ref = refs[2 * S_ATT_PAGES:]
    i = pl.program_id(1)
    flat = N_KV_HEADS * PAGE_SIZE
    c2 = np.float32(HEAD_DIM ** -0.5 * np.log2(np.e))
    q = q_ref[0]
    spread = (lax.broadcasted_iota(jnp.int32, (PAGE_SIZE, flat), 1) // N_KV_HEADS
              == lax.broadcasted_iota(jnp.int32, (PAGE_SIZE, flat), 0))
    spread = jnp.where(spread, 1.0, 0.0).astype(BF16)

    def update(g, kf, vf, sel, n_pg):
        sel2 = _dot(sel.astype(BF16), spread)
        sel2 = jnp.concatenate([sel2[p * T_PAD:(p + 1) * T_PAD, :] for p in range(n_pg)], axis=1)
        bias_t = jnp.where(sel2 > 0.5, 0.0, NEG)
        bias_g = jnp.concatenate([bias_t] * KV_GROUP, axis=0)
        parity = lax.broadcasted_iota(jnp.int32, (1, n_pg * flat), 1) % N_KV_HEADS
        bias = jnp.concatenate([bias_g + jnp.where(parity == j, 0.0, NEG) for j in range(N_KV_HEADS)], axis=0)
        s = _dot_nt(q, kf) + bias
        m_old = m_ref[g]
        m_new = jnp.maximum(m_old, jnp.max(s, axis=1, keepdims=True))
        alpha = jnp.exp2((m_old - m_new) * c2)
        p = jnp.exp2((s - m_new) * c2).astype(BF16)
        v1 = jnp.concatenate([vf, jnp.ones(vf.shape, BF16)], axis=1)
        acc_ref[g] = alpha * acc_ref[g] + _dot(p, v1)
        m_ref[g] = m_new

    @pl.when(i == 0)
    def _():
        m_ref[...] = jnp.full(m_ref.shape, NEG, F32)
        acc_ref[...] = jnp.zeros(acc_ref.shape, F32)
        update(0, kn_ref[0], vn_ref[0], maskn_ref[...], 1)

    per = S_ATT_PAGES // S_ATT_STREAMS
    for g in range(S_ATT_STREAMS):
        pages = range(g * per, (g + 1) * per)
        kf = jnp.concatenate([k_refs[p][...] for p in pages], axis=0).astype(BF16)
        vf = jnp.concatenate([v_refs[p][...] for p in pages], axis=0).astype(BF16)
        update(g, kf, vf, mask_ref[g * per:(g + 1) * per].reshape(per * T_PAD, PAGE_SIZE), per)

    @pl.when(i == pl.num_programs(1) - 1)
    def _():
        m_all = m_ref[0]
        for g in range(1, S_ATT_STREAMS):
            m_all = jnp.maximum(m_all, m_ref[g])
        acc = acc_ref[0] * jnp.exp2((m_ref[0] - m_all) * c2)
        for g in range(1, S_ATT_STREAMS):
            acc = acc + acc_ref[g] * jnp.exp2((m_ref[g] - m_all) * c2)
        o_ref[0] = acc[:, 0:HEAD_DIM] / acc[:, HEAD_DIM:HEAD_DIM + 1]


def _s_att(page_table, q_rows, mask, mask_new, k_new, v_new, cache_k, cache_v, layer):
    db, n_pages = page_table.shape
    steps = n_pages // S_ATT_PAGES
    rows = N_KV_HEADS * KV_GROUP * T_PAD
    flat = N_KV_HEADS * PAGE_SIZE
    page_spec = lambda p: pl.BlockSpec(
        (None, None, flat, HEAD_DIM), lambda b, i, pt: (layer, pt[b, i * S_ATT_PAGES + p], 0, 0))
    grid_spec = pltpu.PrefetchScalarGridSpec(
        num_scalar_prefetch=1,
        grid=(db, steps),
        in_specs=[pl.BlockSpec((1, rows, HEAD_DIM), lambda b, i, pt: (b, 0, 0)),
                  pl.BlockSpec((S_ATT_PAGES, T_PAD, PAGE_SIZE), lambda b, i, pt: (i, b, 0)),
                  pl.BlockSpec((T_PAD, PAGE_SIZE), lambda b, i, pt: (b, 0)),
                  pl.BlockSpec((1, flat, HEAD_DIM), lambda b, i, pt: (b, 0, 0)),
                  pl.BlockSpec((1, flat, HEAD_DIM), lambda b, i, pt: (b, 0, 0))]
                 + [page_spec(p) for p in range(S_ATT_PAGES)] * 2,
        out_specs=pl.BlockSpec((1, rows, HEAD_DIM), lambda b, i, pt: (b, 0, 0)),
        scratch_shapes=[pltpu.VMEM((S_ATT_STREAMS, rows, 1), F32),
                        pltpu.VMEM((S_ATT_STREAMS, rows, 2 * HEAD_DIM), F32)])
    return pl.pallas_call(
        _s_att_kernel,
        grid_spec=grid_spec,
        out_shape=jax.ShapeDtypeStruct((db, rows, HEAD_DIM), F32),
        compiler_params=_params(2),
        name="s_att",
    )(page_table, q_rows, mask, mask_new, k_new, v_new,
      *([cache_k] * S_ATT_PAGES), *([cache_v] * S_ATT_PAGES))


def _sample_mixer(page_table, q_b, qi_b, small_f, k_f, vv_f, cache_k_flat, cache_v_flat, cache_kidx_t, layer, dec_seq):
    db, n_pages = page_table.shape
    t = dec_seq
    past = n_pages * PAGE_SIZE
    k_sel = min(TOPK_MAX, (past + t) // 4)
    pad_t = lambda a: jnp.pad(a, [(0, 0)] * (a.ndim - 2) + [(0, T_PAD - t), (0, 0)])

    qi_rows = pad_t(qi_b.reshape(db, t, IDX_HEADS, IDX_DIM).transpose(0, 2, 1, 3))
    qi_rows = qi_rows.reshape(db, IDX_HEADS * T_PAD, IDX_DIM)
    wi = small_f[:, IDX_DIM:IDX_DIM + IDX_HEADS].reshape(db, t, IDX_HEADS).transpose(0, 2, 1)
    wb_rows = jnp.pad(wi, ((0, 0), (0, 0), (0, T_PAD - t))).reshape(db, IDX_HEADS * T_PAD, 1)
    wb_rows = jnp.broadcast_to(wb_rows, (db, IDX_HEADS * T_PAD, PAGE_SIZE))
    kit_new = small_f[:, 0:IDX_DIM].reshape(db, t, IDX_DIM).transpose(0, 2, 1)
    kit_new = jnp.pad(kit_new, ((0, 0), (0, 0), (0, PAGE_SIZE - t))).astype(BF16)

    scores, scores_new = _s_idx(page_table, qi_rows, wb_rows, kit_new, cache_kidx_t, layer)
    rows_per_step = min(db * T_PAD, 64)
    mask, mask_new = _s_sel(scores, scores_new, k_sel, t, rows_per_step)

    q_rows = pad_t(q_b.reshape(db, t, N_KV_HEADS, KV_GROUP, HEAD_DIM).transpose(0, 2, 3, 1, 4))
    q_rows = q_rows.reshape(db, N_KV_HEADS * KV_GROUP * T_PAD, HEAD_DIM)
    flat_new = lambda a: jnp.pad(a.reshape(db, t * N_KV_HEADS, HEAD_DIM),
                                 ((0, 0), (0, (PAGE_SIZE - t) * N_KV_HEADS), (0, 0))).astype(BF16)
    o = _s_att(page_table, q_rows, mask, mask_new, flat_new(k_f), flat_new(vv_f), cache_k_flat, cache_v_flat, layer)
    o = o.reshape(db, N_KV_HEADS, KV_GROUP, T_PAD, HEAD_DIM)[:, :, :, :t]
    return o.transpose(0, 3, 1, 2, 4).reshape(db * t, N_HEADS * HEAD_DIM).astype(BF16)


def _post_kernel(x_ref, u_ref, v_ref, attn_ref, sga_ref, sgb_ref, ws_ref, bias_ref, wpa_ref, wpb_ref, wout_ref,
                 o_ref, a_ref, *, block):
    tm = x_ref.shape[0]
    r = lax.broadcasted_iota(jnp.int32, (CHUNK, CHUNK), 0)
    c = lax.broadcasted_iota(jnp.int32, (CHUNK, CHUNK), 1)
    allowed = (r >= c) & ((r // block) == (c // block))
    bias = bias_ref[...]
    for g in range(A_GROUPS):
        wg = jnp.where(allowed, ws_ref[g], 0.0).astype(BF16)
        for i in range(tm // CHUNK):
            rows = slice(i * CHUNK, (i + 1) * CHUNK)
            cols = slice(g * CHUNK, (g + 1) * CHUNK)
            sv = _dot(wg, v_ref[rows, cols]) + bias[:, cols]
            a_ref[rows, cols] = (u_ref[rows, cols].astype(F32) * sv).astype(BF16)
    merged = (sga_ref[...].astype(F32) * _dot(a_ref[...], wpa_ref[...])
              + sgb_ref[...].astype(F32) * _dot(attn_ref[...], wpb_ref[...]))
    o_ref[...] = x_ref[...] + _dot(merged.astype(BF16), wout_ref[...])


def _post(x, u, v, attn, sga, sgb, ws, bias, wpa, wpb, wout, tm, block):
    n = x.shape[0]
    row = lambda dt: pl.BlockSpec((tm, D_MODEL), lambda i: (i, 0))
    return pl.pallas_call(
        functools.partial(_post_kernel, block=block),
        grid=(n // tm,),
        in_specs=[row(F32), row(BF16), row(BF16), row(BF16), row(BF16), row(BF16),
                  _const_spec((A_GROUPS, CHUNK, CHUNK)), _const_spec((CHUNK, A_WIDTH)),
                  _const_spec((A_WIDTH, D_MODEL)), _const_spec((N_HEADS * HEAD_DIM, D_MODEL)),
                  _const_spec((D_MODEL, D_MODEL))],
        out_specs=row(F32),
        out_shape=jax.ShapeDtypeStruct((n, D_MODEL), F32),
        scratch_shapes=[pltpu.VMEM((tm, A_WIDTH), BF16)],
        compiler_params=_params(1),
        name="post",
    )(x, u, v, attn, sga, sgb, ws, bias, wpa, wpb, wout)


def kernel(x_prompt, x_sample, cache_k, cache_v, cache_kidx, page_table, g_ffn1, w_up1, w_down1, g_mix, w_in, g_v, g_q, g_k, w_s, b_s, w_pa, w_pb, w_out, g_ffn2, w_up2, w_down2):
    batch, seq, _ = x_prompt.shape
    db, dec_seq, _ = x_sample.shape
    depth = w_in.shape[0]
    n_p, n_s = batch * seq, db * dec_seq
    assert seq % SEARCH_CHUNK == 0 and n_s % CHUNK == 0 and CHUNK % dec_seq == 0 and dec_seq <= T_PAD
    tm_p = 512 if n_p % 512 == 0 else KEY_CHUNK
    tm_s = CHUNK

    xp = x_prompt.reshape(n_p, D_MODEL)
    xs = x_sample.reshape(n_s, D_MODEL)
    n_pool = cache_k.shape[1]
    cache_k_flat = cache_k.reshape(depth, n_pool, PAGE_SIZE * N_KV_HEADS, HEAD_DIM)
    cache_v_flat = cache_v.reshape(depth, n_pool, PAGE_SIZE * N_KV_HEADS, HEAD_DIM)
    cache_kidx_t = jnp.swapaxes(cache_kidx, 2, 3)
    outs = [[] for _ in range(7)]
    for l in range(depth):
        row = lambda a: a[l].reshape(1, -1)
        wup1, wdn1 = w_up1[l].astype(BF16), w_down1[l].astype(BF16)
        wup2, wdn2 = w_up2[l].astype(BF16), w_down2[l].astype(BF16)
        wa = w_in[l, :, 0:WA_W].astype(BF16)
        n_small = IDX_DIM + IDX_HEADS
        ws = jnp.pad(w_in[l, :, WA_W:WA_W + n_small], ((0, 0), (0, SMALL_W - n_small))).astype(BF16)
        wg = w_in[l, :, WA_W + n_small:].astype(BF16)
        wpa, wpb, wout = w_pa[l].astype(BF16), w_pb[l].astype(BF16), w_out[l].astype(BF16)
        ws_p = w_s[l]
        bias_p = jnp.repeat(b_s[l].T, CHUNK, axis=1)
        ws_s = jnp.tile(w_s[l, :, 0:dec_seq, 0:dec_seq], (1, CHUNK // dec_seq, CHUNK // dec_seq))
        bias_s = jnp.repeat(jnp.tile(b_s[l, :, 0:dec_seq].T, (CHUNK // dec_seq, 1)), CHUNK, axis=1)

        x1 = _ffn(xp, row(g_ffn1), wup1, wdn1, tm_p)
        (u, v, k_f, vv_f, small_f, sga, sgb, qt, qit, wit, vt, kb, smallb) = _inproj(
            x1, row(g_mix), wa, ws, wg, row(g_v), row(g_q), row(g_k), tm_p, True)
        attn = _mix(qt, qit, wit, kb, vt, smallb, batch, seq)
        x2 = _post(x1, u, v, attn, sga, sgb, ws_p, bias_p, wpa, wpb, wout, tm_p, CHUNK)
        xp = _ffn(x2, row(g_ffn2), wup2, wdn2, tm_p)
        outs[0].append(k_f.reshape(batch, seq, N_KV_HEADS, HEAD_DIM))
        outs[1].append(vv_f.reshape(batch, seq, N_KV_HEADS, HEAD_DIM))
        outs[2].append(small_f[:, 0:IDX_DIM].reshape(batch, seq, IDX_DIM))

        x1 = _ffn(xs, row(g_ffn1), wup1, wdn1, tm_s)
        (u, v, k_f, vv_f, small_f, sga, sgb, q_b, qi_b) = _inproj(
            x1, row(g_mix), wa, ws, wg, row(g_v), row(g_q), row(g_k), tm_s, False)
        attn = _sample_mixer(page_table, q_b, qi_b, small_f, k_f, vv_f,
                             cache_k_flat, cache_v_flat, cache_kidx_t, l, dec_seq)
        x2 = _post(x1, u.astype(BF16), v.astype(BF16), attn, sga, sgb, ws_s, bias_s, wpa, wpb, wout, tm_s, dec_seq)
        xs = _ffn(x2, row(g_ffn2), wup2, wdn2, tm_s)
        outs[3].append(k_f.reshape(db, dec_seq, N_KV_HEADS, HEAD_DIM))
        outs[4].append(vv_f.reshape(db, dec_seq, N_KV_HEADS, HEAD_DIM))
        outs[5].append(small_f[:, 0:IDX_DIM].reshape(db, dec_seq, IDX_DIM))
        outs[6].append(v.reshape(db, dec_seq, A_WIDTH))

    return (xp.reshape(batch, seq, D_MODEL), xs.reshape(db, dec_seq, D_MODEL),
            *[jnp.stack(o) for o in outs])
```

```python
import functools

import jax
import jax.numpy as jnp
import numpy as np
from jax import lax
from jax.experimental import pallas as pl
from jax.experimental.pallas import tpu as pltpu

D_MODEL = 1024
D_FF = 2816
CHUNK = 128
A_GROUPS = 8
A_WIDTH = D_MODEL
N_HEADS = 8
HEAD_DIM = 128
N_KV_HEADS = 2
KV_GROUP = N_HEADS // N_KV_HEADS
IDX_HEADS = 8
IDX_DIM = 64
TOPK_MAX = 256
Q_BLOCK = 128
PAGE_SIZE = 128
EPS = 1e-6

KV_W = N_KV_HEADS * HEAD_DIM
QI_W = IDX_HEADS * IDX_DIM
WA_W = 3 * D_MODEL + 2 * KV_W + QI_W
SMALL_W = 128
F_CHUNK = 256
KEY_CHUNK = 256
ATT_CHUNK = 512
SEARCH_CHUNK = 256
ONES_ROWS = 16
NEG = -1e30
INT_MIN = -2 ** 31
KEY_NEG_INF = INT_MIN + 0x7FFFFF
VMEM_LIMIT = 56 * 1024 * 1024

BF16 = jnp.bfloat16
F32 = jnp.float32


def _dot(a, b):
    return jnp.dot(a, b, preferred_element_type=F32)


def _dot_nt(a, b):
    return lax.dot_general(a, b, (((1,), (1,)), ((), ())), preferred_element_type=F32)


def _rms(x, g):
    return x * lax.rsqrt(jnp.mean(x * x, axis=-1, keepdims=True) + EPS) * g


def _sigmoid(x):
    return 1.0 / (1.0 + jnp.exp(-x))


def _gelu(x):
    c = np.float32(np.sqrt(2.0 / np.pi))
    return x * (0.5 * (1.0 + jnp.tanh(c * (x + 0.044715 * (x * x * x)))))


def _const_spec(shape):
    nd = len(shape)
    return pl.BlockSpec(shape, lambda *_: (0,) * nd, pipeline_mode=pl.Buffered(1))


def _params(n_axes):
    return pltpu.CompilerParams(dimension_semantics=("arbitrary",) * n_axes, vmem_limit_bytes=VMEM_LIMIT)


def _ffn_apply(x, g_ref, wup_ref, wdn_ref, h_ref):
    xb = _rms(x, g_ref[...]).astype(BF16)
    for j in range(D_FF // F_CHUNK):
        gate = _dot(xb, wup_ref[:, j * F_CHUNK:(j + 1) * F_CHUNK])
        up = _dot(xb, wup_ref[:, D_FF + j * F_CHUNK:D_FF + (j + 1) * F_CHUNK])
        h_ref[:, j * F_CHUNK:(j + 1) * F_CHUNK] = (gate * _sigmoid(gate) * up).astype(BF16)
    return x + 0.5 * _dot(h_ref[...], wdn_ref[...])


def _ffn_kernel(x_ref, g_ref, wup_ref, wdn_ref, o_ref, h_ref):
    o_ref[...] = _ffn_apply(x_ref[...], g_ref, wup_ref, wdn_ref, h_ref)


def _ffn(x, g, wup, wdn, tm):
    n = x.shape[0]
    return pl.pallas_call(
        _ffn_kernel,
        grid=(n // tm,),
        in_specs=[pl.BlockSpec((tm, D_MODEL), lambda i: (i, 0)),
                  _const_spec((1, D_MODEL)),
                  _const_spec((D_MODEL, 2 * D_FF)),
                  _const_spec((D_FF, D_MODEL))],
        out_specs=pl.BlockSpec((tm, D_MODEL), lambda i: (i, 0)),
        out_shape=jax.ShapeDtypeStruct((n, D_MODEL), F32),
        scratch_shapes=[pltpu.VMEM((tm, D_FF), BF16)],
        compiler_params=_params(1),
        name="ffn",
    )(x, g, wup, wdn)


def _inproj_kernel(x_ref, g_ref, wa_ref, ws_ref, wg_ref, gv_ref, gq_ref, gk_ref, *out_refs, prompt):
    if prompt:
        (u_ref, v_ref, k_ref, vv_ref, small_ref, sga_ref, sgb_ref,
         qt_ref, qit_ref, wit_ref, vt_ref, kb_ref, smallb_ref) = out_refs
    else:
        (u_ref, v_ref, k_ref, vv_ref, small_ref, sga_ref, sgb_ref, q_ref, qi_ref) = out_refs
    tm = x_ref.shape[0]
    hb = _rms(x_ref[...], g_ref[...]).astype(BF16)

    u_ref[...] = _gelu(_dot(hb, wa_ref[:, 0:D_MODEL])).astype(u_ref.dtype)
    v = _gelu(_dot(hb, wa_ref[:, D_MODEL:2 * D_MODEL]))
    v_ref[...] = _rms(v, gv_ref[...]).astype(v_ref.dtype)

    q = _dot(hb, wa_ref[:, 2 * D_MODEL:3 * D_MODEL])
    gq = gq_ref[...]
    for h in range(N_HEADS):
        qh = _rms(q[:, h * HEAD_DIM:(h + 1) * HEAD_DIM], gq)
        if prompt:
            qt_ref[h * HEAD_DIM:(h + 1) * HEAD_DIM, :] = qh.T.astype(BF16)
        else:
            q_ref[:, h * HEAD_DIM:(h + 1) * HEAD_DIM] = qh.astype(BF16)

    o = 3 * D_MODEL
    kk = _dot(hb, wa_ref[:, o:o + KV_W])
    gk = gk_ref[...]
    for j in range(N_KV_HEADS):
        kj = _rms(kk[:, j * HEAD_DIM:(j + 1) * HEAD_DIM], gk)
        k_ref[:, j * HEAD_DIM:(j + 1) * HEAD_DIM] = kj
        if prompt:
            kb_ref[:, j * HEAD_DIM:(j + 1) * HEAD_DIM] = kj.astype(BF16)

    vv = _dot(hb, wa_ref[:, o + KV_W:o + 2 * KV_W])
    vv_ref[...] = vv
    if prompt:
        for r in range(tm // KEY_CHUNK):
            vt_ref[r] = vv[r * KEY_CHUNK:(r + 1) * KEY_CHUNK, :].T.astype(BF16)

    qi = _dot(hb, wa_ref[:, o + 2 * KV_W:o + 2 * KV_W + QI_W])
    if prompt:
        qit_ref[...] = qi.T.astype(BF16)
    else:
        qi_ref[...] = qi.astype(BF16)

    small = _dot(hb, ws_ref[...])
    lane = lax.broadcasted_iota(jnp.int32, small.shape, 1)
    is_w = (lane >= IDX_DIM) & (lane < IDX_DIM + IDX_HEADS)
    small = small * jnp.where(is_w, np.float32(IDX_HEADS ** -0.5), np.float32(1.0))
    small_ref[...] = small
    if prompt:
        smallb_ref[...] = small.astype(BF16)
        wit_ref[...] = small.T[IDX_DIM:IDX_DIM + IDX_HEADS, :]

    sga_ref[...] = _sigmoid(_dot(hb, wg_ref[:, 0:D_MODEL])).astype(BF16)
    sgb_ref[...] = _sigmoid(_dot(hb, wg_ref[:, D_MODEL:2 * D_MODEL])).astype(BF16)


def _inproj(x, g, wa, ws, wg, gv, gq, gk, tm, prompt):
    n = x.shape[0]
    row = lambda w: pl.BlockSpec((tm, w), lambda i: (i, 0))
    col = lambda h: pl.BlockSpec((h, tm), lambda i: (0, i))
    out_shape = [jax.ShapeDtypeStruct((n, D_MODEL), BF16),
                 jax.ShapeDtypeStruct((n, A_WIDTH), BF16 if prompt else F32),
                 jax.ShapeDtypeStruct((n, KV_W), F32),
                 jax.ShapeDtypeStruct((n, KV_W), F32),
                 jax.ShapeDtypeStruct((n, SMALL_W), F32),
                 jax.ShapeDtypeStruct((n, D_MODEL), BF16),
                 jax.ShapeDtypeStruct((n, D_MODEL), BF16)]
    out_specs = [row(D_MODEL), row(A_WIDTH), row(KV_W), row(KV_W), row(SMALL_W), row(D_MODEL), row(D_MODEL)]
    if prompt:
        out_shape += [jax.ShapeDtypeStruct((D_MODEL, n), BF16),
                      jax.ShapeDtypeStruct((QI_W, n), BF16),
                      jax.ShapeDtypeStruct((IDX_HEADS, n), F32),
                      jax.ShapeDtypeStruct((n // KEY_CHUNK, KV_W, KEY_CHUNK), BF16),
                      jax.ShapeDtypeStruct((n, KV_W), BF16),
                      jax.ShapeDtypeStruct((n, SMALL_W), BF16)]
        out_specs += [col(D_MODEL), col(QI_W), col(IDX_HEADS),
                      pl.BlockSpec((tm // KEY_CHUNK, KV_W, KEY_CHUNK), lambda i: (i, 0, 0)),
                      row(KV_W), row(SMALL_W)]
    else:
        out_shape += [jax.ShapeDtypeStruct((n, D_MODEL), BF16),
                      jax.ShapeDtypeStruct((n, QI_W), BF16)]
        out_specs += [row(D_MODEL), row(QI_W)]
    return pl.pallas_call(
        functools.partial(_inproj_kernel, prompt=prompt),
        grid=(n // tm,),
        in_specs=[row(D_MODEL), _const_spec((1, D_MODEL)),
                  _const_spec((D_MODEL, WA_W)), _const_spec((D_MODEL, SMALL_W)), _const_spec((D_MODEL, 2 * D_MODEL)),
                  _const_spec((1, A_WIDTH)), _const_spec((1, HEAD_DIM)), _const_spec((1, HEAD_DIM))],
        out_specs=out_specs,
        out_shape=out_shape,
        compiler_params=_params(1),
        name="inproj_prompt" if prompt else "inproj_sample",
    )(x, g, wa, ws, wg, gv, gq, gk)


def _sort_key(x):
    bits = pltpu.bitcast(x, jnp.int32)
    bits = jnp.where(bits == INT_MIN, 0, bits)
    return jnp.where(bits < 0, bits ^ 0x7FFFFFFF, bits)


def _kth_threshold(count_ge, k, shape, total):
    def body(i, carry):
        t_u, cnt_t = carry
        cand_u = t_u | jnp.left_shift(jnp.int32(1), 31 - i)
        cnt = count_ge(cand_u ^ INT_MIN)
        ok = cnt >= k
        return jnp.where(ok, cand_u, t_u), jnp.where(ok, cnt, cnt_t)
    t_u, cnt_t = lax.fori_loop(0, 32, body, (jnp.zeros(shape, jnp.int32), jnp.full(shape, total, F32)))
    return t_u ^ INT_MIN, cnt_t


def _count_rows(keys_ref, rows, pred):
    lanes = keys_ref.shape[1]
    n_acc = 8
    accs = [jnp.zeros((8, lanes), F32) for _ in range(n_acc)]
    for i in range(rows // 8):
        hit = jnp.where(pred(keys_ref[i * 8:(i + 1) * 8, :]), 1.0, 0.0)
        accs[i % n_acc] = accs[i % n_acc] + hit
    while len(accs) > 1:
        accs = [a + b for a, b in zip(accs[0::2], accs[1::2])]
    return jnp.sum(accs[0], axis=0, keepdims=True)


def _mix_kernel(qt_ref, qit_ref, wit_ref, kb_ref, vt_ref, smallb_ref, o_ref,
                keys_ref, bias_ref, thr_ref, need_ref, flag_ref, m_ref, acc_ref, *, k_sel):
    seq = keys_ref.shape[0]
    n = pl.program_id(1)
    t0 = n * Q_BLOCK
    n_att = (n * Q_BLOCK) // ATT_CHUNK + 1
    n_search = (n * Q_BLOCK) // SEARCH_CHUNK + 1
    n_bias = n_att * (ATT_CHUNK // KEY_CHUNK)
    row = lax.broadcasted_iota(jnp.int32, (KEY_CHUNK, Q_BLOCK), 0)
    col = lax.broadcasted_iota(jnp.int32, (KEY_CHUNK, Q_BLOCK), 1)
    row_a = lax.broadcasted_iota(jnp.int32, (ATT_CHUNK, Q_BLOCK), 0)
    col_a = lax.broadcasted_iota(jnp.int32, (ATT_CHUNK, Q_BLOCK), 1)

    qcat = jnp.concatenate([qit_ref[h * IDX_DIM:(h + 1) * IDX_DIM, :] for h in range(IDX_HEADS)], axis=1)
    w = wit_ref[...] * np.float32(IDX_DIM ** -0.5)

    def score_chunk(c, carry):
        r0 = pl.multiple_of(c * ATT_CHUNK, ATT_CHUNK)
        ki = smallb_ref[pl.ds(r0, ATT_CHUNK), :][:, 0:IDX_DIM]
        dots = _dot(ki, qcat)
        sc = w[0:1, :] * jnp.maximum(dots[:, 0:Q_BLOCK], 0.0)
        for h in range(1, IDX_HEADS):
            sc = sc + w[h:h + 1, :] * jnp.maximum(dots[:, h * Q_BLOCK:(h + 1) * Q_BLOCK], 0.0)
        causal = (row_a + r0) <= (col_a + t0)
        keys_ref[pl.ds(r0, ATT_CHUNK), :] = _sort_key(jnp.where(causal, sc, -jnp.inf))
        return carry

    lax.fori_loop(0, n_att, score_chunk, 0)

    few = (n + 1) * Q_BLOCK <= k_sel

    @pl.when(few)
    def _():
        thr_ref[...] = jnp.full((1, Q_BLOCK), KEY_NEG_INF, jnp.int32)
        need_ref[...] = jnp.zeros((1, Q_BLOCK), F32)
        flag_ref[0] = 0.0

    for ns in range(1, seq // SEARCH_CHUNK + 1):
        @pl.when((n_search == ns) & jnp.logical_not(few))
        def _(ns=ns):
            rows = ns * SEARCH_CHUNK
            thr, n_ge = _kth_threshold(lambda t: _count_rows(keys_ref, rows, lambda kk: kk >= t),
                                       k_sel, (1, Q_BLOCK), rows)
            n_gt = _count_rows(keys_ref, rows, lambda kk: kk > thr)
            need = k_sel - n_gt
            tie = jnp.where((thr > KEY_NEG_INF) & (n_ge - n_gt > need), 1.0, 0.0)
            thr_ref[...] = thr
            need_ref[...] = need
            flag_ref[0] = jnp.max(tie)

    thr = thr_ref[...]
    need = need_ref[...]

    @pl.when(flag_ref[0] == 0.0)
    def _():
        thr_eff = jnp.maximum(thr, KEY_NEG_INF + 1)

        def body(c, carry):
            r0 = pl.multiple_of(c * KEY_CHUNK, KEY_CHUNK)
            bias_ref[pl.ds(r0, KEY_CHUNK), :] = jnp.where(keys_ref[pl.ds(r0, KEY_CHUNK), :] >= thr_eff, 0.0, NEG)
            return carry
        lax.fori_loop(0, n_bias, body, 0)

    @pl.when(flag_ref[0] != 0.0)
    def _():
        lower = (lax.broadcasted_iota(jnp.int32, (KEY_CHUNK, KEY_CHUNK), 1)
                 < lax.broadcasted_iota(jnp.int32, (KEY_CHUNK, KEY_CHUNK), 0))
        lower = jnp.where(lower, 1.0, 0.0).astype(BF16)

        def body(c, tie_carry):
            r0 = pl.multiple_of(c * KEY_CHUNK, KEY_CHUNK)
            kk = keys_ref[pl.ds(r0, KEY_CHUNK), :]
            eq = jnp.where(kk == thr, 1.0, 0.0)
            before = _dot(lower, eq.astype(BF16)) + tie_carry
            keep_tie = jnp.where(before < need, eq, 0.0)
            sel = jnp.where(kk > thr, 1.0, keep_tie)
            sel = jnp.where((row + r0) <= (col + t0), sel, 0.0)
            bias_ref[pl.ds(r0, KEY_CHUNK), :] = jnp.where(sel > 0.5, 0.0, NEG)
            return tie_carry + jnp.sum(eq, axis=0, keepdims=True)
        lax.fori_loop(0, n_bias, body, jnp.zeros((1, Q_BLOCK), F32))

    c2 = np.float32(HEAD_DIM ** -0.5 * np.log2(np.e))
    m_ref[...] = jnp.full(m_ref.shape, NEG, F32)
    acc_ref[...] = jnp.zeros(acc_ref.shape, F32)
    ones_rows = jnp.ones((ONES_ROWS, KEY_CHUNK), BF16)
    sub = ATT_CHUNK // KEY_CHUNK

    def attend(c, carry):
        r0 = pl.multiple_of(c * ATT_CHUNK, ATT_CHUNK)
        bias = bias_ref[pl.ds(r0, ATT_CHUNK), :]
        bias4 = jnp.concatenate([bias] * KV_GROUP, axis=1)
        for j in range(N_KV_HEADS):
            qj = jnp.concatenate(
                [qt_ref[(j * KV_GROUP + g) * HEAD_DIM:(j * KV_GROUP + g + 1) * HEAD_DIM, :]
                 for g in range(KV_GROUP)], axis=1)
            kj = kb_ref[pl.ds(r0, ATT_CHUNK), j * HEAD_DIM:(j + 1) * HEAD_DIM]
            s = _dot(kj, qj) + bias4
            m_old = m_ref[j]
            m_new = jnp.maximum(m_old, jnp.max(s, axis=0, keepdims=True))
            alpha = jnp.exp2((m_old - m_new) * c2)
            p = jnp.exp2((s - m_new) * c2).astype(BF16)
            acc = alpha * acc_ref[j]
            for r in range(sub):
                vj = jnp.concatenate([vt_ref[c * sub + r, j * HEAD_DIM:(j + 1) * HEAD_DIM, :], ones_rows], axis=0)
                acc = acc + _dot(vj, p[r * KEY_CHUNK:(r + 1) * KEY_CHUNK, :])
            acc_ref[j] = acc
            m_ref[j] = m_new
        return carry

    lax.fori_loop(0, n_att, attend, 0)

    for j in range(N_KV_HEADS):
        ot = acc_ref[j, 0:HEAD_DIM, :] / acc_ref[j, HEAD_DIM:HEAD_DIM + 1, :]
        for g in range(KV_GROUP):
            h = j * KV_GROUP + g
            o_ref[:, h * HEAD_DIM:(h + 1) * HEAD_DIM] = ot[:, g * Q_BLOCK:(g + 1) * Q_BLOCK].T.astype(BF16)


def _mix(qt, qit, wit, kb, vt, smallb, batch, seq):
    n = batch * seq
    nq = seq // Q_BLOCK
    k_sel = min(TOPK_MAX, seq // 4)
    gq = KV_GROUP * Q_BLOCK
    return pl.pallas_call(
        functools.partial(_mix_kernel, k_sel=k_sel),
        grid=(batch, nq),
        in_specs=[pl.BlockSpec((D_MODEL, Q_BLOCK), lambda b, i: (0, b * nq + i)),
                  pl.BlockSpec((QI_W, Q_BLOCK), lambda b, i: (0, b * nq + i)),
                  pl.BlockSpec((IDX_HEADS, Q_BLOCK), lambda b, i: (0, b * nq + i)),
                  pl.BlockSpec((seq, KV_W), lambda b, i: (b, 0)),
                  pl.BlockSpec((seq // KEY_CHUNK, KV_W, KEY_CHUNK), lambda b, i: (b, 0, 0)),
                  pl.BlockSpec((seq, SMALL_W), lambda b, i: (b, 0))],
        out_specs=pl.BlockSpec((Q_BLOCK, D_MODEL), lambda b, i: (b * nq + i, 0)),
        out_shape=jax.ShapeDtypeStruct((n, D_MODEL), BF16),
        scratch_shapes=[pltpu.VMEM((seq, Q_BLOCK), jnp.int32),
                        pltpu.VMEM((seq, Q_BLOCK), F32),
                        pltpu.VMEM((1, Q_BLOCK), jnp.int32),
                        pltpu.VMEM((1, Q_BLOCK), F32),
                        pltpu.SMEM((1,), F32),
                        pltpu.VMEM((N_KV_HEADS, 1, gq), F32),
                        pltpu.VMEM((N_KV_HEADS, HEAD_DIM + ONES_ROWS, gq), F32)],
        compiler_params=_params(2),
        name="mix_prompt",
    )(qt, qit, wit, kb, vt, smallb)


S_IDX_PAGES = 32
S_ATT_PAGES = 32
T_PAD = 8


def _s_idx_kernel(pt_ref, qi_ref, wb_ref, kin_ref, *refs):
    page_refs, (out_ref, outn_ref) = refs[:S_IDX_PAGES], refs[S_IDX_PAGES:]
    qi = qi_ref[0]
    wb = wb_ref[0]

    def score(kit, n_pg):
        r = jnp.maximum(_dot(qi, kit) * np.float32(IDX_DIM ** -0.5), 0.0) * jnp.concatenate([wb] * n_pg, axis=1)
        sc = r[0:T_PAD, :]
        for h in range(1, IDX_HEADS):
            sc = sc + r[h * T_PAD:(h + 1) * T_PAD, :]
        return sc

    sc = score(jnp.concatenate([page_refs[p][...] for p in range(S_IDX_PAGES)], axis=1).astype(BF16), S_IDX_PAGES)
    for p in range(S_IDX_PAGES):
        out_ref[p] = sc[:, p * PAGE_SIZE:(p + 1) * PAGE_SIZE]

    @pl.when(pl.program_id(1) == 0)
    def _():
        outn_ref[...] = score(kin_ref[0], 1)


def _s_idx(page_table, qi_rows, wb_rows, kit_new, cache_kidx_t, layer):
    db, n_pages = page_table.shape
    steps = n_pages // S_IDX_PAGES
    page_specs = [pl.BlockSpec((None, None, IDX_DIM, PAGE_SIZE),
                               lambda b, i, pt, p=p: (layer, pt[b, i * S_IDX_PAGES + p], 0, 0))
                  for p in range(S_IDX_PAGES)]
    grid_spec = pltpu.PrefetchScalarGridSpec(
        num_scalar_prefetch=1,
        grid=(db, steps),
        in_specs=[pl.BlockSpec((1, IDX_HEADS * T_PAD, IDX_DIM), lambda b, i, pt: (b, 0, 0)),
                  pl.BlockSpec((1, IDX_HEADS * T_PAD, PAGE_SIZE), lambda b, i, pt: (b, 0, 0)),
                  pl.BlockSpec((1, IDX_DIM, PAGE_SIZE), lambda b, i, pt: (b, 0, 0))] + page_specs,
        out_specs=[pl.BlockSpec((S_IDX_PAGES, T_PAD, PAGE_SIZE), lambda b, i, pt: (i, b, 0)),
                   pl.BlockSpec((T_PAD, PAGE_SIZE), lambda b, i, pt: (b, 0))])
    return pl.pallas_call(
        _s_idx_kernel,
        grid_spec=grid_spec,
        out_shape=[jax.ShapeDtypeStruct((n_pages, db * T_PAD, PAGE_SIZE), F32),
                   jax.ShapeDtypeStruct((db * T_PAD, PAGE_SIZE), F32)],
        compiler_params=_params(2),
        name="s_idx",
    )(page_table, qi_rows, wb_rows, kit_new, *([cache_kidx_t] * S_IDX_PAGES))


def _s_sel_kernel(sc_ref, scn_ref, mask_ref, maskn_ref, keys_ref, *, k_sel, n_new):
    n_pages, rows, _ = sc_ref.shape
    keys_ref[0:n_pages] = _sort_key(sc_ref[...])
    t = lax.broadcasted_iota(jnp.int32, (rows, PAGE_SIZE), 0) % T_PAD
    j = lax.broadcasted_iota(jnp.int32, (rows, PAGE_SIZE), 1)
    new_ok = (j <= t) & (j < n_new)
    keys_ref[n_pages] = _sort_key(jnp.where(new_ok, scn_ref[...], -jnp.inf))

    def count(pred):
        acc = jnp.zeros((rows, PAGE_SIZE), F32)
        for c in range(n_pages + 1):
            acc = acc + jnp.where(pred(keys_ref[c]), 1.0, 0.0)
        return jnp.sum(acc, axis=1, keepdims=True)

    thr, n_ge = _kth_threshold(lambda c: count(lambda kk: kk >= c), k_sel, (rows, 1), (n_pages + 1) * PAGE_SIZE)
    n_gt = count(lambda kk: kk > thr)
    need = k_sel - n_gt
    surplus = jnp.max(jnp.where(n_ge - n_gt > need, 1.0, 0.0))

    @pl.when(surplus == 0.0)
    def _():
        mask_ref[...] = jnp.where(keys_ref[0:n_pages] >= thr, 1.0, 0.0)
        maskn_ref[...] = jnp.where(new_ok & (keys_ref[n_pages] >= thr), 1.0, 0.0)

    @pl.when(surplus != 0.0)
    def _():
        upper = (lax.broadcasted_iota(jnp.int32, (PAGE_SIZE, PAGE_SIZE), 0)
                 < lax.broadcasted_iota(jnp.int32, (PAGE_SIZE, PAGE_SIZE), 1))
        upper = jnp.where(upper, 1.0, 0.0).astype(BF16)
        ones = jnp.ones((PAGE_SIZE, PAGE_SIZE), BF16)

        def tile(c, carry):
            kk = keys_ref[c]
            eq = jnp.where(kk == thr, 1.0, 0.0)
            eqb = eq.astype(BF16)
            before = _dot(eqb, upper) + carry
            keep_tie = jnp.where(before < need, eq, 0.0)
            return jnp.where(kk > thr, 1.0, keep_tie), carry + _dot(eqb, ones)

        def past_tile(c, carry):
            sel, carry = tile(c, carry)
            mask_ref[c] = sel
            return carry

        carry = lax.fori_loop(0, n_pages, past_tile, jnp.zeros((rows, PAGE_SIZE), F32))
        sel, _ = tile(n_pages, carry)
        maskn_ref[...] = jnp.where(new_ok, sel, 0.0)


def _s_sel(scores, scores_new, k_sel, n_new, rows_per_step):
    n_pages, rows, _ = scores.shape
    return pl.pallas_call(
        functools.partial(_s_sel_kernel, k_sel=k_sel, n_new=n_new),
        grid=(rows // rows_per_step,),
        in_specs=[pl.BlockSpec((n_pages, rows_per_step, PAGE_SIZE), lambda i: (0, i, 0)),
                  pl.BlockSpec((rows_per_step, PAGE_SIZE), lambda i: (i, 0))],
        out_specs=[pl.BlockSpec((n_pages, rows_per_step, PAGE_SIZE), lambda i: (0, i, 0)),
                   pl.BlockSpec((rows_per_step, PAGE_SIZE), lambda i: (i, 0))],
        out_shape=[jax.ShapeDtypeStruct((n_pages, rows, PAGE_SIZE), F32),
                   jax.ShapeDtypeStruct((rows, PAGE_SIZE), F32)],
        scratch_shapes=[pltpu.VMEM((n_pages + 1, rows_per_step, PAGE_SIZE), jnp.int32)],
        compiler_params=_params(1),
        name="s_sel",
    )(scores, scores_new)


def _s_att_kernel(pt_ref, q_ref, mask_ref, maskn_ref, kn_ref, vn_ref, *refs):
    k_refs = refs[:S_ATT_PAGES]
    v_refs = refs[S_ATT_PAGES:2 * S_ATT_PAGES]
    o_ref, m_ref, l_ref, acc_ref = refs[2 * S_ATT_PAGES:]
    i = pl.program_id(1)
    flat = N_KV_HEADS * PAGE_SIZE
    c2 = np.float32(HEAD_DIM ** -0.5 * np.log2(np.e))
    q = q_ref[0]
    spread = (lax.broadcasted_iota(jnp.int32, (PAGE_SIZE, flat), 1) // N_KV_HEADS
              == lax.broadcasted_iota(jnp.int32, (PAGE_SIZE, flat), 0))
    spread = jnp.where(spread, 1.0, 0.0).astype(BF16)

    def update(kf, vf, sel, n_pg):
        sel2 = _dot(sel.astype(BF16), spread)
        sel2 = jnp.concatenate([sel2[p * T_PAD:(p + 1) * T_PAD, :] for p in range(n_pg)], axis=1)
        bias_t = jnp.where(sel2 > 0.5, 0.0, NEG)
        bias_g = jnp.concatenate([bias_t] * KV_GROUP, axis=0)
        parity = lax.broadcasted_iota(jnp.int32, (1, n_pg * flat), 1) % N_KV_HEADS
        bias = jnp.concatenate([bias_g + jnp.where(parity == j, 0.0, NEG) for j in range(N_KV_HEADS)], axis=0)
        s = _dot_nt(q, kf) + bias
        m_old = m_ref[...]
        m_new = jnp.maximum(m_old, jnp.max(s, axis=1, keepdims=True))
        alpha = jnp.exp2((m_old - m_new) * c2)
        p = jnp.exp2((s - m_new) * c2)
        l_ref[...] = alpha * l_ref[...] + jnp.sum(p, axis=1, keepdims=True)
        acc_ref[...] = alpha * acc_ref[...] + _dot(p.astype(BF16), vf)
        m_ref[...] = m_new

    @pl.when(i == 0)
    def _():
        m_ref[...] = jnp.full(m_ref.shape, NEG, F32)
        l_ref[...] = jnp.zeros(l_ref.shape, F32)
        acc_ref[...] = jnp.zeros(acc_ref.shape, F32)
        update(kn_ref[0], vn_ref[0], maskn_ref[...], 1)

    kf = jnp.concatenate([k_refs[p][...] for p in range(S_ATT_PAGES)], axis=0).astype(BF16)
    vf = jnp.concatenate([v_refs[p][...] for p in range(S_ATT_PAGES)], axis=0).astype(BF16)
    update(kf, vf, mask_ref[...].reshape(S_ATT_PAGES * T_PAD, PAGE_SIZE), S_ATT_PAGES)

    @pl.when(i == pl.num_programs(1) - 1)
    def _():
        o_ref[0] = acc_ref[...] / l_ref[...]


def _s_att(page_table, q_rows, mask, mask_new, k_new, v_new, cache_k, cache_v, layer):
    db, n_pages = page_table.shape
    steps = n_pages // S_ATT_PAGES
    rows = N_KV_HEADS * KV_GROUP * T_PAD
    flat = N_KV_HEADS * PAGE_SIZE
    page_spec = lambda p: pl.BlockSpec(
        (None, None, flat, HEAD_DIM), lambda b, i, pt: (layer, pt[b, i * S_ATT_PAGES + p], 0, 0))
    grid_spec = pltpu.PrefetchScalarGridSpec(
        num_scalar_prefetch=1,
        grid=(db, steps),
        in_specs=[pl.BlockSpec((1, rows, HEAD_DIM), lambda b, i, pt: (b, 0, 0)),
                  pl.BlockSpec((S_ATT_PAGES, T_PAD, PAGE_SIZE), lambda b, i, pt: (i, b, 0)),
                  pl.BlockSpec((T_PAD, PAGE_SIZE), lambda b, i, pt: (b, 0)),
                  pl.BlockSpec((1, flat, HEAD_DIM), lambda b, i, pt: (b, 0, 0)),
                  pl.BlockSpec((1, flat, HEAD_DIM), lambda b, i, pt: (b, 0, 0))]
                 + [page_spec(p) for p in range(S_ATT_PAGES)] * 2,
        out_specs=pl.BlockSpec((1, rows, HEAD_DIM), lambda b, i, pt: (b, 0, 0)),
        scratch_shapes=[pltpu.VMEM((rows, 1), F32),
                        pltpu.VMEM((rows, 1), F32),
                        pltpu.VMEM((rows, HEAD_DIM), F32)])
    return pl.pallas_call(
        _s_att_kernel,
        grid_spec=grid_spec,
        out_shape=jax.ShapeDtypeStruct((db, rows, HEAD_DIM), F32),
        compiler_params=_params(2),
        name="s_att",
    )(page_table, q_rows, mask, mask_new, k_new, v_new,
      *([cache_k] * S_ATT_PAGES), *([cache_v] * S_ATT_PAGES))


def _sample_mixer(page_table, q_b, qi_b, small_f, k_f, vv_f, cache_k_flat, cache_v_flat, cache_kidx_t, layer, dec_seq):
    db, n_pages = page_table.shape
    t = dec_seq
    past = n_pages * PAGE_SIZE
    k_sel = min(TOPK_MAX, (past + t) // 4)
    pad_t = lambda a: jnp.pad(a, [(0, 0)] * (a.ndim - 2) + [(0, T_PAD - t), (0, 0)])

    qi_rows = pad_t(qi_b.reshape(db, t, IDX_HEADS, IDX_DIM).transpose(0, 2, 1, 3))
    qi_rows = qi_rows.reshape(db, IDX_HEADS * T_PAD, IDX_DIM)
    wi = small_f[:, IDX_DIM:IDX_DIM + IDX_HEADS].reshape(db, t, IDX_HEADS).transpose(0, 2, 1)
    wb_rows = jnp.pad(wi, ((0, 0), (0, 0), (0, T_PAD - t))).reshape(db, IDX_HEADS * T_PAD, 1)
    wb_rows = jnp.broadcast_to(wb_rows, (db, IDX_HEADS * T_PAD, PAGE_SIZE))
    kit_new = small_f[:, 0:IDX_DIM].reshape(db, t, IDX_DIM).transpose(0, 2, 1)
    kit_new = jnp.pad(kit_new, ((0, 0), (0, 0), (0, PAGE_SIZE - t))).astype(BF16)

    scores, scores_new = _s_idx(page_table, qi_rows, wb_rows, kit_new, cache_kidx_t, layer)
    rows_per_step = min(db * T_PAD, 64)
    mask, mask_new = _s_sel(scores, scores_new, k_sel, t, rows_per_step)

    q_rows = pad_t(q_b.reshape(db, t, N_KV_HEADS, KV_GROUP, HEAD_DIM).transpose(0, 2, 3, 1, 4))
    q_rows = q_rows.reshape(db, N_KV_HEADS * KV_GROUP * T_PAD, HEAD_DIM)
    flat_new = lambda a: jnp.pad(a.reshape(db, t * N_KV_HEADS, HEAD_DIM),
                                 ((0, 0), (0, (PAGE_SIZE - t) * N_KV_HEADS), (0, 0))).astype(BF16)
    o = _s_att(page_table, q_rows, mask, mask_new, flat_new(k_f), flat_new(vv_f), cache_k_flat, cache_v_flat, layer)
    o = o.reshape(db, N_KV_HEADS, KV_GROUP, T_PAD, HEAD_DIM)[:, :, :, :t]
    return o.transpose(0, 3, 1, 2, 4).reshape(db * t, N_HEADS * HEAD_DIM).astype(BF16)


def _post_kernel(x_ref, u_ref, v_ref, attn_ref, sga_ref, sgb_ref, ws_ref, bias_ref, wpa_ref, wpb_ref, wout_ref,
                 o_ref, a_ref, *, block):
    tm = x_ref.shape[0]
    r = lax.broadcasted_iota(jnp.int32, (CHUNK, CHUNK), 0)
    c = lax.broadcasted_iota(jnp.int32, (CHUNK, CHUNK), 1)
    allowed = (r >= c) & ((r // block) == (c // block))
    bias = bias_ref[...]
    for g in range(A_GROUPS):
        wg = jnp.where(allowed, ws_ref[g], 0.0).astype(BF16)
        for i in range(tm // CHUNK):
            rows = slice(i * CHUNK, (i + 1) * CHUNK)
            cols = slice(g * CHUNK, (g + 1) * CHUNK)
            sv = _dot(wg, v_ref[rows, cols]) + bias[:, cols]
            a_ref[rows, cols] = (u_ref[rows, cols].astype(F32) * sv).astype(BF16)
    merged = (sga_ref[...].astype(F32) * _dot(a_ref[...], wpa_ref[...])
              + sgb_ref[...].astype(F32) * _dot(attn_ref[...], wpb_ref[...]))
    o_ref[...] = x_ref[...] + _dot(merged.astype(BF16), wout_ref[...])


def _post(x, u, v, attn, sga, sgb, ws, bias, wpa, wpb, wout, tm, block):
    n = x.shape[0]
    row = lambda dt: pl.BlockSpec((tm, D_MODEL), lambda i: (i, 0))
    return pl.pallas_call(
        functools.partial(_post_kernel, block=block),
        grid=(n // tm,),
        in_specs=[row(F32), row(BF16), row(BF16), row(BF16), row(BF16), row(BF16),
                  _const_spec((A_GROUPS, CHUNK, CHUNK)), _const_spec((CHUNK, A_WIDTH)),
                  _const_spec((A_WIDTH, D_MODEL)), _const_spec((N_HEADS * HEAD_DIM, D_MODEL)),
                  _const_spec((D_MODEL, D_MODEL))],
        out_specs=row(F32),
        out_shape=jax.ShapeDtypeStruct((n, D_MODEL), F32),
        scratch_shapes=[pltpu.VMEM((tm, A_WIDTH), BF16)],
        compiler_params=_params(1),
        name="post",
    )(x, u, v, attn, sga, sgb, ws, bias, wpa, wpb, wout)


def kernel(x_prompt, x_sample, cache_k, cache_v, cache_kidx, page_table, g_ffn1, w_up1, w_down1, g_mix, w_in, g_v, g_q, g_k, w_s, b_s, w_pa, w_pb, w_out, g_ffn2, w_up2, w_down2):
    batch, seq, _ = x_prompt.shape
    db, dec_seq, _ = x_sample.shape
    depth = w_in.shape[0]
    n_p, n_s = batch * seq, db * dec_seq
    assert seq % ATT_CHUNK == 0 and n_s % CHUNK == 0 and CHUNK % dec_seq == 0 and dec_seq <= T_PAD
    tm_p = 512 if n_p % 512 == 0 else KEY_CHUNK
    tm_s = CHUNK

    xp = x_prompt.reshape(n_p, D_MODEL)
    xs = x_sample.reshape(n_s, D_MODEL)
    n_pool = cache_k.shape[1]
    cache_k_flat = cache_k.reshape(depth, n_pool, PAGE_SIZE * N_KV_HEADS, HEAD_DIM)
    cache_v_flat = cache_v.reshape(depth, n_pool, PAGE_SIZE * N_KV_HEADS, HEAD_DIM)
    cache_kidx_t = jnp.swapaxes(cache_kidx, 2, 3)
    outs = [[] for _ in range(7)]
    for l in range(depth):
        row = lambda a: a[l].reshape(1, -1)
        wup1, wdn1 = w_up1[l].astype(BF16), w_down1[l].astype(BF16)
        wup2, wdn2 = w_up2[l].astype(BF16), w_down2[l].astype(BF16)
        wa = w_in[l, :, 0:WA_W].astype(BF16)
        n_small = IDX_DIM + IDX_HEADS
        ws = jnp.pad(w_in[l, :, WA_W:WA_W + n_small], ((0, 0), (0, SMALL_W - n_small))).astype(BF16)
        wg = w_in[l, :, WA_W + n_small:].astype(BF16)
        wpa, wpb, wout = w_pa[l].astype(BF16), w_pb[l].astype(BF16), w_out[l].astype(BF16)
        ws_p = w_s[l]
        bias_p = jnp.repeat(b_s[l].T, CHUNK, axis=1)
        ws_s = jnp.tile(w_s[l, :, 0:dec_seq, 0:dec_seq], (1, CHUNK // dec_seq, CHUNK // dec_seq))
        bias_s = jnp.repeat(jnp.tile(b_s[l, :, 0:dec_seq].T, (CHUNK // dec_seq, 1)), CHUNK, axis=1)

        x1 = _ffn(xp, row(g_ffn1), wup1, wdn1, tm_p)
        (u, v, k_f, vv_f, small_f, sga, sgb, qt, qit, wit, vt, kb, smallb) = _inproj(
            x1, row(g_mix), wa, ws, wg, row(g_v), row(g_q), row(g_k), tm_p, True)
        attn = _mix(qt, qit, wit, kb, vt, smallb, batch, seq)
        x2 = _post(x1, u, v, attn, sga, sgb, ws_p, bias_p, wpa, wpb, wout, tm_p, CHUNK)
        xp = _ffn(x2, row(g_ffn2), wup2, wdn2, tm_p)
        outs[0].append(k_f.reshape(batch, seq, N_KV_HEADS, HEAD_DIM))
        outs[1].append(vv_f.reshape(batch, seq, N_KV_HEADS, HEAD_DIM))
        outs[2].append(small_f[:, 0:IDX_DIM].reshape(batch, seq, IDX_DIM))

        x1 = _ffn(xs, row(g_ffn1), wup1, wdn1, tm_s)
        (u, v, k_f, vv_f, small_f, sga, sgb, q_b, qi_b) = _inproj(
            x1, row(g_mix), wa, ws, wg, row(g_v), row(g_q), row(g_k), tm_s, False)
        attn = _sample_mixer(page_table, q_b, qi_b, small_f, k_f, vv_f,
                             cache_k_flat, cache_v_flat, cache_kidx_t, l, dec_seq)
        x2 = _post(x1, u.astype(BF16), v.astype(BF16), attn, sga, sgb, ws_s, bias_s, wpa, wpb, wout, tm_s, dec_seq)
        xs = _ffn(x2, row(g_ffn2), wup2, wdn2, tm_s)
        outs[3].append(k_f.reshape(db, dec_seq, N_KV_HEADS, HEAD_DIM))
        outs[4].append(vv_f.reshape(db, dec_seq, N_KV_HEADS, HEAD_DIM))
        outs[5].append(small_f[:, 0:IDX_DIM].reshape(db, dec_seq, IDX_DIM))
        outs[6].append(v.reshape(db, dec_seq, A_WIDTH))

    return (xp.reshape(batch, seq, D_MODEL), xs.reshape(db, dec_seq, D_MODEL),
            *[jnp.stack(o) for o in outs])
```

```python
import functools

import jax
import jax.numpy as jnp
import numpy as np
from jax import lax
from jax.experimental import pallas as pl
from jax.experimental.pallas import tpu as pltpu

D_MODEL = 1024
D_FF = 2816
CHUNK = 128
A_GROUPS = 8
A_WIDTH = D_MODEL
N_HEADS = 8
HEAD_DIM = 128
N_KV_HEADS = 2
KV_GROUP = N_HEADS // N_KV_HEADS
IDX_HEADS = 8
IDX_DIM = 64
TOPK_MAX = 256
Q_BLOCK = 128
PAGE_SIZE = 128
EPS = 1e-6

KV_W = N_KV_HEADS * HEAD_DIM
QI_W = IDX_HEADS * IDX_DIM
WA_W = 3 * D_MODEL + 2 * KV_W + QI_W
SMALL_W = 128
F_CHUNK = 256
KEY_CHUNK = 256
ATT_CHUNK = 512
SEARCH_CHUNK = 256
ONES_ROWS = 16
NEG = -1e30
INT_MIN = -2 ** 31
KEY_NEG_INF = INT_MIN + 0x7FFFFF
VMEM_LIMIT = 56 * 1024 * 1024

BF16 = jnp.bfloat16
F32 = jnp.float32


def _dot(a, b):
    return jnp.dot(a, b, preferred_element_type=F32)


def _dot_nt(a, b):
    return lax.dot_general(a, b, (((1,), (1,)), ((), ())), preferred_element_type=F32)


def _rms(x, g):
    return x * lax.rsqrt(jnp.mean(x * x, axis=-1, keepdims=True) + EPS) * g


def _sigmoid(x):
    return 1.0 / (1.0 + jnp.exp(-x))


def _gelu(x):
    c = np.float32(np.sqrt(2.0 / np.pi))
    return x * (0.5 * (1.0 + jnp.tanh(c * (x + 0.044715 * (x * x * x)))))


def _const_spec(shape):
    nd = len(shape)
    return pl.BlockSpec(shape, lambda *_: (0,) * nd, pipeline_mode=pl.Buffered(1))


def _params(n_axes):
    return pltpu.CompilerParams(dimension_semantics=("arbitrary",) * n_axes, vmem_limit_bytes=VMEM_LIMIT)


def _ffn_apply(x, g_ref, wup_ref, wdn_ref, h_ref):
    xb = _rms(x, g_ref[...]).astype(BF16)
    for j in range(D_FF // F_CHUNK):
        gate = _dot(xb, wup_ref[:, j * F_CHUNK:(j + 1) * F_CHUNK])
        up = _dot(xb, wup_ref[:, D_FF + j * F_CHUNK:D_FF + (j + 1) * F_CHUNK])
        h_ref[:, j * F_CHUNK:(j + 1) * F_CHUNK] = (gate * _sigmoid(gate) * up).astype(BF16)
    return x + 0.5 * _dot(h_ref[...], wdn_ref[...])


def _ffn_kernel(x_ref, g_ref, wup_ref, wdn_ref, o_ref, h_ref):
    o_ref[...] = _ffn_apply(x_ref[...], g_ref, wup_ref, wdn_ref, h_ref)


def _ffn(x, g, wup, wdn, tm):
    n = x.shape[0]
    return pl.pallas_call(
        _ffn_kernel,
        grid=(n // tm,),
        in_specs=[pl.BlockSpec((tm, D_MODEL), lambda i: (i, 0)),
                  _const_spec((1, D_MODEL)),
                  _const_spec((D_MODEL, 2 * D_FF)),
                  _const_spec((D_FF, D_MODEL))],
        out_specs=pl.BlockSpec((tm, D_MODEL), lambda i: (i, 0)),
        out_shape=jax.ShapeDtypeStruct((n, D_MODEL), F32),
        scratch_shapes=[pltpu.VMEM((tm, D_FF), BF16)],
        compiler_params=_params(1),
        name="ffn",
    )(x, g, wup, wdn)


def _inproj_kernel(x_ref, g_ref, wa_ref, ws_ref, wg_ref, gv_ref, gq_ref, gk_ref, *out_refs, prompt):
    if prompt:
        (u_ref, v_ref, k_ref, vv_ref, small_ref, sga_ref, sgb_ref,
         qt_ref, qit_ref, wit_ref, vt_ref, kb_ref, smallb_ref) = out_refs
    else:
        (u_ref, v_ref, k_ref, vv_ref, small_ref, sga_ref, sgb_ref, q_ref, qi_ref) = out_refs
    tm = x_ref.shape[0]
    hb = _rms(x_ref[...], g_ref[...]).astype(BF16)

    u_ref[...] = _gelu(_dot(hb, wa_ref[:, 0:D_MODEL])).astype(u_ref.dtype)
    v = _gelu(_dot(hb, wa_ref[:, D_MODEL:2 * D_MODEL]))
    v_ref[...] = _rms(v, gv_ref[...]).astype(v_ref.dtype)

    q = _dot(hb, wa_ref[:, 2 * D_MODEL:3 * D_MODEL])
    gq = gq_ref[...]
    for h in range(N_HEADS):
        qh = _rms(q[:, h * HEAD_DIM:(h + 1) * HEAD_DIM], gq)
        if prompt:
            qt_ref[h * HEAD_DIM:(h + 1) * HEAD_DIM, :] = qh.T.astype(BF16)
        else:
            q_ref[:, h * HEAD_DIM:(h + 1) * HEAD_DIM] = qh.astype(BF16)

    o = 3 * D_MODEL
    kk = _dot(hb, wa_ref[:, o:o + KV_W])
    gk = gk_ref[...]
    for j in range(N_KV_HEADS):
        kj = _rms(kk[:, j * HEAD_DIM:(j + 1) * HEAD_DIM], gk)
        k_ref[:, j * HEAD_DIM:(j + 1) * HEAD_DIM] = kj
        if prompt:
            kb_ref[:, j * HEAD_DIM:(j + 1) * HEAD_DIM] = kj.astype(BF16)

    vv = _dot(hb, wa_ref[:, o + KV_W:o + 2 * KV_W])
    vv_ref[...] = vv
    if prompt:
        for r in range(tm // KEY_CHUNK):
            vt_ref[r] = vv[r * KEY_CHUNK:(r + 1) * KEY_CHUNK, :].T.astype(BF16)

    qi = _dot(hb, wa_ref[:, o + 2 * KV_W:o + 2 * KV_W + QI_W])
    if prompt:
        qit_ref[...] = qi.T.astype(BF16)
    else:
        qi_ref[...] = qi.astype(BF16)

    small = _dot(hb, ws_ref[...])
    lane = lax.broadcasted_iota(jnp.int32, small.shape, 1)
    is_w = (lane >= IDX_DIM) & (lane < IDX_DIM + IDX_HEADS)
    small = small * jnp.where(is_w, np.float32(IDX_HEADS ** -0.5), np.float32(1.0))
    small_ref[...] = small
    if prompt:
        smallb_ref[...] = small.astype(BF16)
        wit_ref[...] = small.T[IDX_DIM:IDX_DIM + IDX_HEADS, :]

    sga_ref[...] = _sigmoid(_dot(hb, wg_ref[:, 0:D_MODEL])).astype(BF16)
    sgb_ref[...] = _sigmoid(_dot(hb, wg_ref[:, D_MODEL:2 * D_MODEL])).astype(BF16)


def _inproj(x, g, wa, ws, wg, gv, gq, gk, tm, prompt):
    n = x.shape[0]
    row = lambda w: pl.BlockSpec((tm, w), lambda i: (i, 0))
    col = lambda h: pl.BlockSpec((h, tm), lambda i: (0, i))
    out_shape = [jax.ShapeDtypeStruct((n, D_MODEL), BF16),
                 jax.ShapeDtypeStruct((n, A_WIDTH), BF16 if prompt else F32),
                 jax.ShapeDtypeStruct((n, KV_W), F32),
                 jax.ShapeDtypeStruct((n, KV_W), F32),
                 jax.ShapeDtypeStruct((n, SMALL_W), F32),
                 jax.ShapeDtypeStruct((n, D_MODEL), BF16),
                 jax.ShapeDtypeStruct((n, D_MODEL), BF16)]
    out_specs = [row(D_MODEL), row(A_WIDTH), row(KV_W), row(KV_W), row(SMALL_W), row(D_MODEL), row(D_MODEL)]
    if prompt:
        out_shape += [jax.ShapeDtypeStruct((D_MODEL, n), BF16),
                      jax.ShapeDtypeStruct((QI_W, n), BF16),
                      jax.ShapeDtypeStruct((IDX_HEADS, n), F32),
                      jax.ShapeDtypeStruct((n // KEY_CHUNK, KV_W, KEY_CHUNK), BF16),
                      jax.ShapeDtypeStruct((n, KV_W), BF16),
                      jax.ShapeDtypeStruct((n, SMALL_W), BF16)]
        out_specs += [col(D_MODEL), col(QI_W), col(IDX_HEADS),
                      pl.BlockSpec((tm // KEY_CHUNK, KV_W, KEY_CHUNK), lambda i: (i, 0, 0)),
                      row(KV_W), row(SMALL_W)]
    else:
        out_shape += [jax.ShapeDtypeStruct((n, D_MODEL), BF16),
                      jax.ShapeDtypeStruct((n, QI_W), BF16)]
        out_specs += [row(D_MODEL), row(QI_W)]
    return pl.pallas_call(
        functools.partial(_inproj_kernel, prompt=prompt),
        grid=(n // tm,),
        in_specs=[row(D_MODEL), _const_spec((1, D_MODEL)),
                  _const_spec((D_MODEL, WA_W)), _const_spec((D_MODEL, SMALL_W)), _const_spec((D_MODEL, 2 * D_MODEL)),
                  _const_spec((1, A_WIDTH)), _const_spec((1, HEAD_DIM)), _const_spec((1, HEAD_DIM))],
        out_specs=out_specs,
        out_shape=out_shape,
        compiler_params=_params(1),
        name="inproj_prompt" if prompt else "inproj_sample",
    )(x, g, wa, ws, wg, gv, gq, gk)


def _sort_key(x):
    bits = pltpu.bitcast(x, jnp.int32)
    bits = jnp.where(bits == INT_MIN, 0, bits)
    return jnp.where(bits < 0, bits ^ 0x7FFFFFFF, bits)


def _kth_threshold(count_ge, k, shape, total):
    def body(i, carry):
        t_u, cnt_t = carry
        cand_u = t_u | jnp.left_shift(jnp.int32(1), 31 - i)
        cnt = count_ge(cand_u ^ INT_MIN)
        ok = cnt >= k
        return jnp.where(ok, cand_u, t_u), jnp.where(ok, cnt, cnt_t)
    t_u, cnt_t = lax.fori_loop(0, 32, body, (jnp.zeros(shape, jnp.int32), jnp.full(shape, total, F32)))
    return t_u ^ INT_MIN, cnt_t


def _count_rows(keys_ref, rows, pred):
    lanes = keys_ref.shape[1]
    n_acc = 8
    accs = [jnp.zeros((8, lanes), F32) for _ in range(n_acc)]
    for i in range(rows // 8):
        hit = jnp.where(pred(keys_ref[i * 8:(i + 1) * 8, :]), 1.0, 0.0)
        accs[i % n_acc] = accs[i % n_acc] + hit
    while len(accs) > 1:
        accs = [a + b for a, b in zip(accs[0::2], accs[1::2])]
    return jnp.sum(accs[0], axis=0, keepdims=True)


I16_MIN = -2 ** 15


def _count_rows16(ref, rows, pred):
    lanes = ref.shape[1]
    n_acc = 8
    one, zero = jnp.int16(1), jnp.int16(0)
    accs = [jnp.zeros((16, lanes), jnp.int16) for _ in range(n_acc)]
    for i in range(rows // 16):
        accs[i % n_acc] = accs[i % n_acc] + jnp.where(pred(ref[i * 16:(i + 1) * 16, :]), one, zero)
    while len(accs) > 1:
        accs = [a + b for a, b in zip(accs[0::2], accs[1::2])]
    return jnp.sum(accs[0].astype(F32), axis=0, keepdims=True)


def _tile16(x):
    return jnp.broadcast_to(x, (16, x.shape[1])).astype(jnp.int16)


def _kth_threshold16(count_ge, k, lanes, total):
    def body(i, carry):
        t_u, cnt_t = carry
        cand_u = t_u | jnp.left_shift(jnp.int32(1), 15 - i)
        cnt = count_ge(_tile16(cand_u + I16_MIN))
        ok = cnt >= k
        return jnp.where(ok, cand_u, t_u), jnp.where(ok, cnt, cnt_t)
    init = (jnp.zeros((1, lanes), jnp.int32), jnp.full((1, lanes), total, F32))
    t_u, cnt_t = lax.fori_loop(0, 16, body, init)
    return t_u + I16_MIN, cnt_t


def _mix_kernel(qt_ref, qit_ref, wit_ref, kb_ref, vt_ref, smallb_ref, o_ref,
                keys_ref, hi_ref, lo_ref, tie_lo_ref, bias_ref, thr_ref, need_ref, flag_ref, m_ref, acc_ref, *, k_sel):
    seq = keys_ref.shape[0]
    n = pl.program_id(1)
    t0 = n * Q_BLOCK
    n_att = (n * Q_BLOCK) // ATT_CHUNK + 1
    n_search = (n * Q_BLOCK) // SEARCH_CHUNK + 1
    n_bias = n_att * (ATT_CHUNK // KEY_CHUNK)
    row = lax.broadcasted_iota(jnp.int32, (KEY_CHUNK, Q_BLOCK), 0)
    col = lax.broadcasted_iota(jnp.int32, (KEY_CHUNK, Q_BLOCK), 1)
    row_a = lax.broadcasted_iota(jnp.int32, (ATT_CHUNK, Q_BLOCK), 0)
    col_a = lax.broadcasted_iota(jnp.int32, (ATT_CHUNK, Q_BLOCK), 1)

    qcat = jnp.concatenate([qit_ref[h * IDX_DIM:(h + 1) * IDX_DIM, :] for h in range(IDX_HEADS)], axis=1)
    w = wit_ref[...] * np.float32(IDX_DIM ** -0.5)

    def score_chunk(c, carry):
        r0 = pl.multiple_of(c * ATT_CHUNK, ATT_CHUNK)
        ki = smallb_ref[pl.ds(r0, ATT_CHUNK), :][:, 0:IDX_DIM]
        dots = _dot(ki, qcat)
        sc = w[0:1, :] * jnp.maximum(dots[:, 0:Q_BLOCK], 0.0)
        for h in range(1, IDX_HEADS):
            sc = sc + w[h:h + 1, :] * jnp.maximum(dots[:, h * Q_BLOCK:(h + 1) * Q_BLOCK], 0.0)
        causal = (row_a + r0) <= (col_a + t0)
        key = _sort_key(jnp.where(causal, sc, -jnp.inf))
        keys_ref[pl.ds(r0, ATT_CHUNK), :] = key
        hi_ref[pl.ds(r0, ATT_CHUNK), :] = (key >> 16).astype(jnp.int16)
        lo_ref[pl.ds(r0, ATT_CHUNK), :] = ((key & 0xFFFF) + I16_MIN).astype(jnp.int16)
        return carry

    lax.fori_loop(0, n_att, score_chunk, 0)

    few = (n + 1) * Q_BLOCK <= k_sel

    @pl.when(few)
    def _():
        thr_ref[...] = jnp.full((1, Q_BLOCK), KEY_NEG_INF, jnp.int32)
        need_ref[...] = jnp.zeros((1, Q_BLOCK), F32)
        flag_ref[0] = 0.0

    for ns in range(1, seq // SEARCH_CHUNK + 1):
        @pl.when((n_search == ns) & jnp.logical_not(few))
        def _(ns=ns):
            rows = ns * SEARCH_CHUNK
            t_hi, c_ge_hi = _kth_threshold16(lambda c: _count_rows16(hi_ref, rows, lambda b: b >= c),
                                             k_sel, Q_BLOCK, rows)
            hi16 = _tile16(t_hi)
            c_gt_hi = _count_rows16(hi_ref, rows, lambda b: b > hi16)
            for i in range(rows // 16):
                blk = slice(i * 16, (i + 1) * 16)
                tie_lo_ref[blk, :] = jnp.where(hi_ref[blk, :] == hi16, lo_ref[blk, :], jnp.int16(I16_MIN))
            t_lo, c_ge_lo = _kth_threshold16(lambda c: _count_rows16(tie_lo_ref, rows, lambda b: b >= c),
                                             k_sel - c_gt_hi, Q_BLOCK, rows)
            lo16 = _tile16(t_lo)
            thr = jnp.left_shift(t_hi, 16) | (t_lo - I16_MIN)
            n_gt = c_gt_hi + _count_rows16(tie_lo_ref, rows, lambda b: b > lo16)
            n_ge = c_gt_hi + jnp.where(t_lo > I16_MIN, c_ge_lo, c_ge_hi - c_gt_hi)
            need = k_sel - n_gt
            tie = jnp.where((thr > KEY_NEG_INF) & (n_ge - n_gt > need), 1.0, 0.0)
            thr_ref[...] = thr
            need_ref[...] = need
            flag_ref[0] = jnp.max(tie)

    thr = thr_ref[...]
    need = need_ref[...]

    @pl.when(flag_ref[0] == 0.0)
    def _():
        thr_eff = jnp.maximum(thr, KEY_NEG_INF + 1)

        def body(c, carry):
            r0 = pl.multiple_of(c * KEY_CHUNK, KEY_CHUNK)
            bias_ref[pl.ds(r0, KEY_CHUNK), :] = jnp.where(keys_ref[pl.ds(r0, KEY_CHUNK), :] >= thr_eff, 0.0, NEG)
            return carry
        lax.fori_loop(0, n_bias, body, 0)

    @pl.when(flag_ref[0] != 0.0)
    def _():
        lower = (lax.broadcasted_iota(jnp.int32, (KEY_CHUNK, KEY_CHUNK), 1)
                 < lax.broadcasted_iota(jnp.int32, (KEY_CHUNK, KEY_CHUNK), 0))
        lower = jnp.where(lower, 1.0, 0.0).astype(BF16)

        def body(c, tie_carry):
            r0 = pl.multiple_of(c * KEY_CHUNK, KEY_CHUNK)
            kk = keys_ref[pl.ds(r0, KEY_CHUNK), :]
            eq = jnp.where(kk == thr, 1.0, 0.0)
            before = _dot(lower, eq.astype(BF16)) + tie_carry
            keep_tie = jnp.where(before < need, eq, 0.0)
            sel = jnp.where(kk > thr, 1.0, keep_tie)
            sel = jnp.where((row + r0) <= (col + t0), sel, 0.0)
            bias_ref[pl.ds(r0, KEY_CHUNK), :] = jnp.where(sel > 0.5, 0.0, NEG)
            return tie_carry + jnp.sum(eq, axis=0, keepdims=True)
        lax.fori_loop(0, n_bias, body, jnp.zeros((1, Q_BLOCK), F32))

    c2 = np.float32(HEAD_DIM ** -0.5 * np.log2(np.e))
    m_ref[...] = jnp.full(m_ref.shape, NEG, F32)
    acc_ref[...] = jnp.zeros(acc_ref.shape, F32)
    ones_rows = jnp.ones((ONES_ROWS, KEY_CHUNK), BF16)
    sub = ATT_CHUNK // KEY_CHUNK

    def attend(c, carry):
        r0 = pl.multiple_of(c * ATT_CHUNK, ATT_CHUNK)
        bias = bias_ref[pl.ds(r0, ATT_CHUNK), :]
        bias4 = jnp.concatenate([bias] * KV_GROUP, axis=1)
        for j in range(N_KV_HEADS):
            qj = jnp.concatenate(
                [qt_ref[(j * KV_GROUP + g) * HEAD_DIM:(j * KV_GROUP + g + 1) * HEAD_DIM, :]
                 for g in range(KV_GROUP)], axis=1)
            kj = kb_ref[pl.ds(r0, ATT_CHUNK), j * HEAD_DIM:(j + 1) * HEAD_DIM]
            s = _dot(kj, qj) + bias4
            m_old = m_ref[j]
            m_new = jnp.maximum(m_old, jnp.max(s, axis=0, keepdims=True))
            alpha = jnp.exp2((m_old - m_new) * c2)
            p = jnp.exp2((s - m_new) * c2).astype(BF16)
            acc = alpha * acc_ref[j]
            for r in range(sub):
                vj = jnp.concatenate([vt_ref[c * sub + r, j * HEAD_DIM:(j + 1) * HEAD_DIM, :], ones_rows], axis=0)
                acc = acc + _dot(vj, p[r * KEY_CHUNK:(r + 1) * KEY_CHUNK, :])
            acc_ref[j] = acc
            m_ref[j] = m_new
        return carry

    lax.fori_loop(0, n_att, attend, 0)

    for j in range(N_KV_HEADS):
        ot = acc_ref[j, 0:HEAD_DIM, :] / acc_ref[j, HEAD_DIM:HEAD_DIM + 1, :]
        for g in range(KV_GROUP):
            h = j * KV_GROUP + g
            o_ref[:, h * HEAD_DIM:(h + 1) * HEAD_DIM] = ot[:, g * Q_BLOCK:(g + 1) * Q_BLOCK].T.astype(BF16)


def _mix(qt, qit, wit, kb, vt, smallb, batch, seq):
    n = batch * seq
    nq = seq // Q_BLOCK
    k_sel = min(TOPK_MAX, seq // 4)
    gq = KV_GROUP * Q_BLOCK
    return pl.pallas_call(
        functools.partial(_mix_kernel, k_sel=k_sel),
        grid=(batch, nq),
        in_specs=[pl.BlockSpec((D_MODEL, Q_BLOCK), lambda b, i: (0, b * nq + i)),
                  pl.BlockSpec((QI_W, Q_BLOCK), lambda b, i: (0, b * nq + i)),
                  pl.BlockSpec((IDX_HEADS, Q_BLOCK), lambda b, i: (0, b * nq + i)),
                  pl.BlockSpec((seq, KV_W), lambda b, i: (b, 0)),
                  pl.BlockSpec((seq // KEY_CHUNK, KV_W, KEY_CHUNK), lambda b, i: (b, 0, 0)),
                  pl.BlockSpec((seq, SMALL_W), lambda b, i: (b, 0))],
        out_specs=pl.BlockSpec((Q_BLOCK, D_MODEL), lambda b, i: (b * nq + i, 0)),
        out_shape=jax.ShapeDtypeStruct((n, D_MODEL), BF16),
        scratch_shapes=[pltpu.VMEM((seq, Q_BLOCK), jnp.int32),
                        pltpu.VMEM((seq, Q_BLOCK), jnp.int16),
                        pltpu.VMEM((seq, Q_BLOCK), jnp.int16),
                        pltpu.VMEM((seq, Q_BLOCK), jnp.int16),
                        pltpu.VMEM((seq, Q_BLOCK), F32),
                        pltpu.VMEM((1, Q_BLOCK), jnp.int32),
                        pltpu.VMEM((1, Q_BLOCK), F32),
                        pltpu.SMEM((1,), F32),
                        pltpu.VMEM((N_KV_HEADS, 1, gq), F32),
                        pltpu.VMEM((N_KV_HEADS, HEAD_DIM + ONES_ROWS, gq), F32)],
        compiler_params=_params(2),
        name="mix_prompt",
    )(qt, qit, wit, kb, vt, smallb)


S_IDX_PAGES = 32
S_ATT_PAGES = 16
T_PAD = 8


def _page_copies(pt_ref, b, first, count, layer, cache_ref, buf_ref, slot, sem):
    return [pltpu.make_async_copy(cache_ref.at[layer, pt_ref[b, first + p]], buf_ref.at[slot, p], sem)
            for p in range(count)]


def _paged_loop(n_groups, copies, compute):
    for c in copies(0, 0):
        c.start()

    def body(g, carry):
        slot = g % 2

        @pl.when(g + 1 < n_groups)
        def _():
            for c in copies(g + 1, 1 - slot):
                c.start()
        for c in copies(g, slot):
            c.wait()
        compute(g, slot)
        return carry

    lax.fori_loop(0, n_groups, body, 0)


def _s_idx_kernel(pt_ref, qi_ref, wb_ref, kin_ref, cache_ref, out_ref, outn_ref, buf_ref, sem_ref, *, layer):
    b = pl.program_id(0)
    n_pages = out_ref.shape[0]
    qi = qi_ref[0]
    wb = wb_ref[0]

    def score(kit, n_pg):
        r = jnp.maximum(_dot(qi, kit) * np.float32(IDX_DIM ** -0.5), 0.0) * jnp.concatenate([wb] * n_pg, axis=1)
        sc = r[0:T_PAD, :]
        for h in range(1, IDX_HEADS):
            sc = sc + r[h * T_PAD:(h + 1) * T_PAD, :]
        return sc

    outn_ref[...] = score(kin_ref[0], 1)

    def copies(g, slot):
        return _page_copies(pt_ref, b, g * S_IDX_PAGES, S_IDX_PAGES, layer, cache_ref, buf_ref, slot, sem_ref.at[slot])

    def compute(g, slot):
        kit = jnp.concatenate([buf_ref[slot, p] for p in range(S_IDX_PAGES)], axis=1).astype(BF16)
        sc = score(kit, S_IDX_PAGES)
        for p in range(S_IDX_PAGES):
            out_ref[g * S_IDX_PAGES + p] = sc[:, p * PAGE_SIZE:(p + 1) * PAGE_SIZE]

    _paged_loop(n_pages // S_IDX_PAGES, copies, compute)


def _s_idx(page_table, qi_rows, wb_rows, kit_new, cache_kidx_t, layer):
    db, n_pages = page_table.shape
    grid_spec = pltpu.PrefetchScalarGridSpec(
        num_scalar_prefetch=1,
        grid=(db,),
        in_specs=[pl.BlockSpec((1, IDX_HEADS * T_PAD, IDX_DIM), lambda b, pt: (b, 0, 0)),
                  pl.BlockSpec((1, IDX_HEADS * T_PAD, PAGE_SIZE), lambda b, pt: (b, 0, 0)),
                  pl.BlockSpec((1, IDX_DIM, PAGE_SIZE), lambda b, pt: (b, 0, 0)),
                  pl.BlockSpec(memory_space=pl.ANY)],
        out_specs=[pl.BlockSpec((n_pages, T_PAD, PAGE_SIZE), lambda b, pt: (0, b, 0)),
                   pl.BlockSpec((T_PAD, PAGE_SIZE), lambda b, pt: (b, 0))],
        scratch_shapes=[pltpu.VMEM((2, S_IDX_PAGES, IDX_DIM, PAGE_SIZE), F32),
                        pltpu.SemaphoreType.DMA((2,))])
    return pl.pallas_call(
        functools.partial(_s_idx_kernel, layer=layer),
        grid_spec=grid_spec,
        out_shape=[jax.ShapeDtypeStruct((n_pages, db * T_PAD, PAGE_SIZE), F32),
                   jax.ShapeDtypeStruct((db * T_PAD, PAGE_SIZE), F32)],
        compiler_params=_params(1),
        name="s_idx",
    )(page_table, qi_rows, wb_rows, kit_new, cache_kidx_t)


def _s_sel_kernel(sc_ref, scn_ref, mask_ref, maskn_ref, keys_ref, *, k_sel, n_new):
    n_pages, rows, _ = sc_ref.shape
    keys_ref[0:n_pages] = _sort_key(sc_ref[...])
    t = lax.broadcasted_iota(jnp.int32, (rows, PAGE_SIZE), 0) % T_PAD
    j = lax.broadcasted_iota(jnp.int32, (rows, PAGE_SIZE), 1)
    new_ok = (j <= t) & (j < n_new)
    keys_ref[n_pages] = _sort_key(jnp.where(new_ok, scn_ref[...], -jnp.inf))

    def count(pred):
        acc = jnp.zeros((rows, PAGE_SIZE), F32)
        for c in range(n_pages + 1):
            acc = acc + jnp.where(pred(keys_ref[c]), 1.0, 0.0)
        return jnp.sum(acc, axis=1, keepdims=True)

    thr, n_ge = _kth_threshold(lambda c: count(lambda kk: kk >= c), k_sel, (rows, 1), (n_pages + 1) * PAGE_SIZE)
    n_gt = count(lambda kk: kk > thr)
    need = k_sel - n_gt
    surplus = jnp.max(jnp.where((n_ge - n_gt > need) & (t[:, 0:1] < n_new), 1.0, 0.0))

    @pl.when(surplus == 0.0)
    def _():
        mask_ref[...] = jnp.where(keys_ref[0:n_pages] >= thr, 1.0, 0.0)
        maskn_ref[...] = jnp.where(new_ok & (keys_ref[n_pages] >= thr), 1.0, 0.0)

    @pl.when(surplus != 0.0)
    def _():
        upper = (lax.broadcasted_iota(jnp.int32, (PAGE_SIZE, PAGE_SIZE), 0)
                 < lax.broadcasted_iota(jnp.int32, (PAGE_SIZE, PAGE_SIZE), 1))
        upper = jnp.where(upper, 1.0, 0.0).astype(BF16)
        ones = jnp.ones((PAGE_SIZE, PAGE_SIZE), BF16)

        def tile(c, carry):
            kk = keys_ref[c]
            eq = jnp.where(kk == thr, 1.0, 0.0)
            eqb = eq.astype(BF16)
            before = _dot(eqb, upper) + carry
            keep_tie = jnp.where(before < need, eq, 0.0)
            return jnp.where(kk > thr, 1.0, keep_tie), carry + _dot(eqb, ones)

        def past_tile(c, carry):
            sel, carry = tile(c, carry)
            mask_ref[c] = sel
            return carry

        carry = lax.fori_loop(0, n_pages, past_tile, jnp.zeros((rows, PAGE_SIZE), F32))
        sel, _ = tile(n_pages, carry)
        maskn_ref[...] = jnp.where(new_ok, sel, 0.0)


def _s_sel(scores, scores_new, k_sel, n_new, rows_per_step):
    n_pages, rows, _ = scores.shape
    return pl.pallas_call(
        functools.partial(_s_sel_kernel, k_sel=k_sel, n_new=n_new),
        grid=(rows // rows_per_step,),
        in_specs=[pl.BlockSpec((n_pages, rows_per_step, PAGE_SIZE), lambda i: (0, i, 0)),
                  pl.BlockSpec((rows_per_step, PAGE_SIZE), lambda i: (i, 0))],
        out_specs=[pl.BlockSpec((n_pages, rows_per_step, PAGE_SIZE), lambda i: (0, i, 0)),
                   pl.BlockSpec((rows_per_step, PAGE_SIZE), lambda i: (i, 0))],
        out_shape=[jax.ShapeDtypeStruct((n_pages, rows, PAGE_SIZE), F32),
                   jax.ShapeDtypeStruct((rows, PAGE_SIZE), F32)],
        scratch_shapes=[pltpu.VMEM((n_pages + 1, rows_per_step, PAGE_SIZE), jnp.int32)],
        compiler_params=_params(1),
        name="s_sel",
    )(scores, scores_new)


def _s_att_kernel(pt_ref, q_ref, mask_ref, maskn_ref, kn_ref, vn_ref, ck_ref, cv_ref, o_ref,
                  kbuf_ref, vbuf_ref, sem_ref, m_ref, l_ref, acc_ref, *, layer):
    b = pl.program_id(0)
    n_pages = mask_ref.shape[0]
    flat = N_KV_HEADS * PAGE_SIZE
    c2 = np.float32(HEAD_DIM ** -0.5 * np.log2(np.e))
    q = q_ref[0]
    spread = (lax.broadcasted_iota(jnp.int32, (PAGE_SIZE, flat), 1) // N_KV_HEADS
              == lax.broadcasted_iota(jnp.int32, (PAGE_SIZE, flat), 0))
    spread = jnp.where(spread, 1.0, 0.0).astype(BF16)

    def update(kf, vf, sel, n_pg):
        sel2 = _dot(sel.astype(BF16), spread)
        sel2 = jnp.concatenate([sel2[p * T_PAD:(p + 1) * T_PAD, :] for p in range(n_pg)], axis=1)
        bias_t = jnp.where(sel2 > 0.5, 0.0, NEG)
        bias_g = jnp.concatenate([bias_t] * KV_GROUP, axis=0)
        parity = lax.broadcasted_iota(jnp.int32, (1, n_pg * flat), 1) % N_KV_HEADS
        bias = jnp.concatenate([bias_g + jnp.where(parity == j, 0.0, NEG) for j in range(N_KV_HEADS)], axis=0)
        s = _dot_nt(q, kf) + bias
        m_old = m_ref[...]
        m_new = jnp.maximum(m_old, jnp.max(s, axis=1, keepdims=True))
        alpha = jnp.exp2((m_old - m_new) * c2)
        p = jnp.exp2((s - m_new) * c2)
        l_ref[...] = alpha * l_ref[...] + jnp.sum(p, axis=1, keepdims=True)
        acc_ref[...] = alpha * acc_ref[...] + _dot(p.astype(BF16), vf)
        m_ref[...] = m_new

    m_ref[...] = jnp.full(m_ref.shape, NEG, F32)
    l_ref[...] = jnp.zeros(l_ref.shape, F32)
    acc_ref[...] = jnp.zeros(acc_ref.shape, F32)
    update(kn_ref[0], vn_ref[0], maskn_ref[...], 1)

    def copies(g, slot):
        first = g * S_ATT_PAGES
        return (_page_copies(pt_ref, b, first, S_ATT_PAGES, layer, ck_ref, kbuf_ref, slot, sem_ref.at[0, slot])
                + _page_copies(pt_ref, b, first, S_ATT_PAGES, layer, cv_ref, vbuf_ref, slot, sem_ref.at[1, slot]))

    def compute(g, slot):
        kf = kbuf_ref[slot].reshape(S_ATT_PAGES * flat, HEAD_DIM).astype(BF16)
        vf = vbuf_ref[slot].reshape(S_ATT_PAGES * flat, HEAD_DIM).astype(BF16)
        sel = mask_ref[pl.ds(g * S_ATT_PAGES, S_ATT_PAGES)].reshape(S_ATT_PAGES * T_PAD, PAGE_SIZE)
        update(kf, vf, sel, S_ATT_PAGES)

    _paged_loop(n_pages // S_ATT_PAGES, copies, compute)
    o_ref[0] = acc_ref[...] / l_ref[...]


def _s_att(page_table, q_rows, mask, mask_new, k_new, v_new, cache_k, cache_v, layer):
    db, n_pages = page_table.shape
    rows = N_KV_HEADS * KV_GROUP * T_PAD
    flat = N_KV_HEADS * PAGE_SIZE
    grid_spec = pltpu.PrefetchScalarGridSpec(
        num_scalar_prefetch=1,
        grid=(db,),
        in_specs=[pl.BlockSpec((1, rows, HEAD_DIM), lambda b, pt: (b, 0, 0)),
                  pl.BlockSpec((n_pages, T_PAD, PAGE_SIZE), lambda b, pt: (0, b, 0)),
                  pl.BlockSpec((T_PAD, PAGE_SIZE), lambda b, pt: (b, 0)),
                  pl.BlockSpec((1, flat, HEAD_DIM), lambda b, pt: (b, 0, 0)),
                  pl.BlockSpec((1, flat, HEAD_DIM), lambda b, pt: (b, 0, 0)),
                  pl.BlockSpec(memory_space=pl.ANY),
                  pl.BlockSpec(memory_space=pl.ANY)],
        out_specs=pl.BlockSpec((1, rows, HEAD_DIM), lambda b, pt: (b, 0, 0)),
        scratch_shapes=[pltpu.VMEM((2, S_ATT_PAGES, flat, HEAD_DIM), F32),
                        pltpu.VMEM((2, S_ATT_PAGES, flat, HEAD_DIM), F32),
                        pltpu.SemaphoreType.DMA((2, 2)),
                        pltpu.VMEM((rows, 1), F32),
                        pltpu.VMEM((rows, 1), F32),
                        pltpu.VMEM((rows, HEAD_DIM), F32)])
    return pl.pallas_call(
        functools.partial(_s_att_kernel, layer=layer),
        grid_spec=grid_spec,
        out_shape=jax.ShapeDtypeStruct((db, rows, HEAD_DIM), F32),
        compiler_params=_params(1),
        name="s_att",
    )(page_table, q_rows, mask, mask_new, k_new, v_new, cache_k, cache_v)


def _sample_mixer(page_table, q_b, qi_b, small_f, k_f, vv_f, cache_k_flat, cache_v_flat, cache_kidx_t, layer, dec_seq):
    db, n_pages = page_table.shape
    t = dec_seq
    past = n_pages * PAGE_SIZE
    k_sel = min(TOPK_MAX, (past + t) // 4)
    pad_t = lambda a: jnp.pad(a, [(0, 0)] * (a.ndim - 2) + [(0, T_PAD - t), (0, 0)])

    qi_rows = pad_t(qi_b.reshape(db, t, IDX_HEADS, IDX_DIM).transpose(0, 2, 1, 3))
    qi_rows = qi_rows.reshape(db, IDX_HEADS * T_PAD, IDX_DIM)
    wi = small_f[:, IDX_DIM:IDX_DIM + IDX_HEADS].reshape(db, t, IDX_HEADS).transpose(0, 2, 1)
    wb_rows = jnp.pad(wi, ((0, 0), (0, 0), (0, T_PAD - t))).reshape(db, IDX_HEADS * T_PAD, 1)
    wb_rows = jnp.broadcast_to(wb_rows, (db, IDX_HEADS * T_PAD, PAGE_SIZE))
    kit_new = small_f[:, 0:IDX_DIM].reshape(db, t, IDX_DIM).transpose(0, 2, 1)
    kit_new = jnp.pad(kit_new, ((0, 0), (0, 0), (0, PAGE_SIZE - t))).astype(BF16)

    scores, scores_new = _s_idx(page_table, qi_rows, wb_rows, kit_new, cache_kidx_t, layer)
    rows_per_step = min(db * T_PAD, 64)
    mask, mask_new = _s_sel(scores, scores_new, k_sel, t, rows_per_step)

    q_rows = pad_t(q_b.reshape(db, t, N_KV_HEADS, KV_GROUP, HEAD_DIM).transpose(0, 2, 3, 1, 4))
    q_rows = q_rows.reshape(db, N_KV_HEADS * KV_GROUP * T_PAD, HEAD_DIM)
    flat_new = lambda a: jnp.pad(a.reshape(db, t * N_KV_HEADS, HEAD_DIM),
                                 ((0, 0), (0, (PAGE_SIZE - t) * N_KV_HEADS), (0, 0))).astype(BF16)
    o = _s_att(page_table, q_rows, mask, mask_new, flat_new(k_f), flat_new(vv_f), cache_k_flat, cache_v_flat, layer)
    o = o.reshape(db, N_KV_HEADS, KV_GROUP, T_PAD, HEAD_DIM)[:, :, :, :t]
    return o.transpose(0, 3, 1, 2, 4).reshape(db * t, N_HEADS * HEAD_DIM).astype(BF16)


def _post_kernel(x_ref, u_ref, v_ref, attn_ref, sga_ref, sgb_ref, ws_ref, bias_ref, wpa_ref, wpb_ref, wout_ref,
                 o_ref, a_ref, *, block):
    tm = x_ref.shape[0]
    r = lax.broadcasted_iota(jnp.int32, (CHUNK, CHUNK), 0)
    c = lax.broadcasted_iota(jnp.int32, (CHUNK, CHUNK), 1)
    allowed = (r >= c) & ((r // block) == (c // block))
    bias = bias_ref[...]
    for g in range(A_GROUPS):
        wg = jnp.where(allowed, ws_ref[g], 0.0).astype(BF16)
        for i in range(tm // CHUNK):
            rows = slice(i * CHUNK, (i + 1) * CHUNK)
            cols = slice(g * CHUNK, (g + 1) * CHUNK)
            sv = _dot(wg, v_ref[rows, cols]) + bias[:, cols]
            a_ref[rows, cols] = (u_ref[rows, cols].astype(F32) * sv).astype(BF16)
    merged = (sga_ref[...].astype(F32) * _dot(a_ref[...], wpa_ref[...])
              + sgb_ref[...].astype(F32) * _dot(attn_ref[...], wpb_ref[...]))
    o_ref[...] = x_ref[...] + _dot(merged.astype(BF16), wout_ref[...])


def _post(x, u, v, attn, sga, sgb, ws, bias, wpa, wpb, wout, tm, block):
    n = x.shape[0]
    row = lambda dt: pl.BlockSpec((tm, D_MODEL), lambda i: (i, 0))
    return pl.pallas_call(
        functools.partial(_post_kernel, block=block),
        grid=(n // tm,),
        in_specs=[row(F32), row(BF16), row(BF16), row(BF16), row(BF16), row(BF16),
                  _const_spec((A_GROUPS, CHUNK, CHUNK)), _const_spec((CHUNK, A_WIDTH)),
                  _const_spec((A_WIDTH, D_MODEL)), _const_spec((N_HEADS * HEAD_DIM, D_MODEL)),
                  _const_spec((D_MODEL, D_MODEL))],
        out_specs=row(F32),
        out_shape=jax.ShapeDtypeStruct((n, D_MODEL), F32),
        scratch_shapes=[pltpu.VMEM((tm, A_WIDTH), BF16)],
        compiler_params=_params(1),
        name="post",
    )(x, u, v, attn, sga, sgb, ws, bias, wpa, wpb, wout)


def kernel(x_prompt, x_sample, cache_k, cache_v, cache_kidx, page_table, g_ffn1, w_up1, w_down1, g_mix, w_in, g_v, g_q, g_k, w_s, b_s, w_pa, w_pb, w_out, g_ffn2, w_up2, w_down2):
    batch, seq, _ = x_prompt.shape
    db, dec_seq, _ = x_sample.shape
    depth = w_in.shape[0]
    n_p, n_s = batch * seq, db * dec_seq
    assert seq % ATT_CHUNK == 0 and n_s % CHUNK == 0 and CHUNK % dec_seq == 0 and dec_seq <= T_PAD
    tm_p = 512 if n_p % 512 == 0 else KEY_CHUNK
    tm_s = CHUNK

    xp = x_prompt.reshape(n_p, D_MODEL)
    xs = x_sample.reshape(n_s, D_MODEL)
    n_pool = cache_k.shape[1]
    cache_k_flat = cache_k.reshape(depth, n_pool, PAGE_SIZE * N_KV_HEADS, HEAD_DIM)
    cache_v_flat = cache_v.reshape(depth, n_pool, PAGE_SIZE * N_KV_HEADS, HEAD_DIM)
    cache_kidx_t = jnp.swapaxes(cache_kidx, 2, 3)
    outs = [[] for _ in range(7)]
    for l in range(depth):
        row = lambda a: a[l].reshape(1, -1)
        wup1, wdn1 = w_up1[l].astype(BF16), w_down1[l].astype(BF16)
        wup2, wdn2 = w_up2[l].astype(BF16), w_down2[l].astype(BF16)
        wa = w_in[l, :, 0:WA_W].astype(BF16)
        n_small = IDX_DIM + IDX_HEADS
        ws = jnp.pad(w_in[l, :, WA_W:WA_W + n_small], ((0, 0), (0, SMALL_W - n_small))).astype(BF16)
        wg = w_in[l, :, WA_W + n_small:].astype(BF16)
        wpa, wpb, wout = w_pa[l].astype(BF16), w_pb[l].astype(BF16), w_out[l].astype(BF16)
        ws_p = w_s[l]
        bias_p = jnp.repeat(b_s[l].T, CHUNK, axis=1)
        ws_s = jnp.tile(w_s[l, :, 0:dec_seq, 0:dec_seq], (1, CHUNK // dec_seq, CHUNK // dec_seq))
        bias_s = jnp.repeat(jnp.tile(b_s[l, :, 0:dec_seq].T, (CHUNK // dec_seq, 1)), CHUNK, axis=1)

        x1 = _ffn(xp, row(g_ffn1), wup1, wdn1, tm_p)
        (u, v, k_f, vv_f, small_f, sga, sgb, qt, qit, wit, vt, kb, smallb) = _inproj(
            x1, row(g_mix), wa, ws, wg, row(g_v), row(g_q), row(g_k), tm_p, True)
        attn = _mix(qt, qit, wit, kb, vt, smallb, batch, seq)
        x2 = _post(x1, u, v, attn, sga, sgb, ws_p, bias_p, wpa, wpb, wout, tm_p, CHUNK)
        xp = _ffn(x2, row(g_ffn2), wup2, wdn2, tm_p)
        outs[0].append(k_f.reshape(batch, seq, N_KV_HEADS, HEAD_DIM))
        outs[1].append(vv_f.reshape(batch, seq, N_KV_HEADS, HEAD_DIM))
        outs[2].append(small_f[:, 0:IDX_DIM].reshape(batch, seq, IDX_DIM))

        x1 = _ffn(xs, row(g_ffn1), wup1, wdn1, tm_s)
        (u, v, k_f, vv_f, small_f, sga, sgb, q_b, qi_b) = _inproj(
            x1, row(g_mix), wa, ws, wg, row(g_v), row(g_q), row(g_k), tm_s, False)
        attn = _sample_mixer(page_table, q_b, qi_b, small_f, k_f, vv_f,
                             cache_k_flat, cache_v_flat, cache_kidx_t, l, dec_seq)
        x2 = _post(x1, u.astype(BF16), v.astype(BF16), attn, sga, sgb, ws_s, bias_s, wpa, wpb, wout, tm_s, dec_seq)
        xs = _ffn(x2, row(g_ffn2), wup2, wdn2, tm_s)
        outs[3].append(k_f.reshape(db, dec_seq, N_KV_HEADS, HEAD_DIM))
        outs[4].append(vv_f.reshape(db, dec_seq, N_KV_HEADS, HEAD_DIM))
        outs[5].append(small_f[:, 0:IDX_DIM].reshape(db, dec_seq, IDX_DIM))
        outs[6].append(v.reshape(db, dec_seq, A_WIDTH))

    return (xp.reshape(batch, seq, D_MODEL), xs.reshape(db, dec_seq, D_MODEL),
            *[jnp.stack(o) for o in outs])
```

```python
import functools

import jax
import jax.numpy as jnp
import numpy as np
from jax import lax
from jax.experimental import pallas as pl
from jax.experimental.pallas import tpu as pltpu

D_MODEL = 1024
D_FF = 2816
CHUNK = 128
A_GROUPS = 8
A_WIDTH = D_MODEL
N_HEADS = 8
HEAD_DIM = 128
N_KV_HEADS = 2
KV_GROUP = N_HEADS // N_KV_HEADS
IDX_HEADS = 8
IDX_DIM = 64
TOPK_MAX = 256
Q_BLOCK = 128
PAGE_SIZE = 128
EPS = 1e-6

KV_W = N_KV_HEADS * HEAD_DIM
QI_W = IDX_HEADS * IDX_DIM
WA_W = 3 * D_MODEL + 2 * KV_W + QI_W
SMALL_W = 128
F_CHUNK = 256
KEY_CHUNK = 256
ATT_CHUNK = 512
SEARCH_CHUNK = 256
ONES_ROWS = 16
NEG = -1e30
INT_MIN = -2 ** 31
KEY_NEG_INF = INT_MIN + 0x7FFFFF
VMEM_LIMIT = 56 * 1024 * 1024

BF16 = jnp.bfloat16
F32 = jnp.float32


def _dot(a, b):
    return jnp.dot(a, b, preferred_element_type=F32)


def _dot_nt(a, b):
    return lax.dot_general(a, b, (((1,), (1,)), ((), ())), preferred_element_type=F32)


def _rms(x, g):
    return x * lax.rsqrt(jnp.mean(x * x, axis=-1, keepdims=True) + EPS) * g


def _sigmoid(x):
    return 1.0 / (1.0 + jnp.exp(-x))


def _gelu(x):
    c = np.float32(np.sqrt(2.0 / np.pi))
    return x * (0.5 * (1.0 + jnp.tanh(c * (x + 0.044715 * (x * x * x)))))


def _const_spec(shape):
    nd = len(shape)
    return pl.BlockSpec(shape, lambda *_: (0,) * nd, pipeline_mode=pl.Buffered(1))


def _params(n_axes):
    return pltpu.CompilerParams(dimension_semantics=("arbitrary",) * n_axes, vmem_limit_bytes=VMEM_LIMIT)


def _ffn_apply(x, g_ref, wup_ref, wdn_ref, h_ref):
    xb = _rms(x, g_ref[...]).astype(BF16)
    for j in range(D_FF // F_CHUNK):
        gate = _dot(xb, wup_ref[:, j * F_CHUNK:(j + 1) * F_CHUNK])
        up = _dot(xb, wup_ref[:, D_FF + j * F_CHUNK:D_FF + (j + 1) * F_CHUNK])
        h_ref[:, j * F_CHUNK:(j + 1) * F_CHUNK] = (gate * _sigmoid(gate) * up).astype(BF16)
    return x + 0.5 * _dot(h_ref[...], wdn_ref[...])


def _ffn_kernel(x_ref, g_ref, wup_ref, wdn_ref, o_ref, h_ref):
    o_ref[...] = _ffn_apply(x_ref[...], g_ref, wup_ref, wdn_ref, h_ref)


def _ffn(x, g, wup, wdn, tm):
    n = x.shape[0]
    return pl.pallas_call(
        _ffn_kernel,
        grid=(n // tm,),
        in_specs=[pl.BlockSpec((tm, D_MODEL), lambda i: (i, 0)),
                  _const_spec((1, D_MODEL)),
                  _const_spec((D_MODEL, 2 * D_FF)),
                  _const_spec((D_FF, D_MODEL))],
        out_specs=pl.BlockSpec((tm, D_MODEL), lambda i: (i, 0)),
        out_shape=jax.ShapeDtypeStruct((n, D_MODEL), F32),
        scratch_shapes=[pltpu.VMEM((tm, D_FF), BF16)],
        compiler_params=_params(1),
        name="ffn",
    )(x, g, wup, wdn)


def _inproj_kernel(x_ref, g_ref, wa_ref, ws_ref, wg_ref, gv_ref, gq_ref, gk_ref, *out_refs, prompt):
    if prompt:
        (u_ref, v_ref, k_ref, vv_ref, kit_ref, sga_ref, sgb_ref,
         qt_ref, qit_ref, wit_ref, vt_ref, kb_ref, smallb_ref) = out_refs
    else:
        (u_ref, v_ref, k_ref, vv_ref, small_ref, sga_ref, sgb_ref, q_ref, qi_ref) = out_refs
    tm = x_ref.shape[0]
    hb = _rms(x_ref[...], g_ref[...]).astype(BF16)

    u_ref[...] = _gelu(_dot(hb, wa_ref[:, 0:D_MODEL])).astype(u_ref.dtype)
    v = _gelu(_dot(hb, wa_ref[:, D_MODEL:2 * D_MODEL]))
    v_ref[...] = _rms(v, gv_ref[...]).astype(v_ref.dtype)

    q = _dot(hb, wa_ref[:, 2 * D_MODEL:3 * D_MODEL])
    gq = gq_ref[...]
    for h in range(N_HEADS):
        qh = _rms(q[:, h * HEAD_DIM:(h + 1) * HEAD_DIM], gq)
        if prompt:
            qt_ref[h * HEAD_DIM:(h + 1) * HEAD_DIM, :] = qh.T.astype(BF16)
        else:
            q_ref[:, h * HEAD_DIM:(h + 1) * HEAD_DIM] = qh.astype(BF16)

    o = 3 * D_MODEL
    kk = _dot(hb, wa_ref[:, o:o + KV_W])
    gk = gk_ref[...]
    vv = _dot(hb, wa_ref[:, o + KV_W:o + 2 * KV_W])
    for j in range(N_KV_HEADS):
        kj = _rms(kk[:, j * HEAD_DIM:(j + 1) * HEAD_DIM], gk)
        if prompt:
            k_ref[pl.ds(j, tm, stride=N_KV_HEADS), :] = kj
            vv_ref[pl.ds(j, tm, stride=N_KV_HEADS), :] = vv[:, j * HEAD_DIM:(j + 1) * HEAD_DIM]
            kb_ref[:, j * HEAD_DIM:(j + 1) * HEAD_DIM] = kj.astype(BF16)
        else:
            k_ref[:, j * HEAD_DIM:(j + 1) * HEAD_DIM] = kj
    if prompt:
        for r in range(tm // KEY_CHUNK):
            vt_ref[r] = vv[r * KEY_CHUNK:(r + 1) * KEY_CHUNK, :].T.astype(BF16)
    else:
        vv_ref[...] = vv

    qi = _dot(hb, wa_ref[:, o + 2 * KV_W:o + 2 * KV_W + QI_W])
    if prompt:
        qit_ref[...] = qi.T.astype(BF16)
    else:
        qi_ref[...] = qi.astype(BF16)

    small = _dot(hb, ws_ref[...])
    lane = lax.broadcasted_iota(jnp.int32, small.shape, 1)
    is_w = (lane >= IDX_DIM) & (lane < IDX_DIM + IDX_HEADS)
    small = small * jnp.where(is_w, np.float32(IDX_HEADS ** -0.5), np.float32(1.0))
    if prompt:
        smallb_ref[...] = small.astype(BF16)
        small_t = small.T
        kit_ref[0] = small_t[0:IDX_DIM, :]
        wit_ref[...] = small_t[IDX_DIM:IDX_DIM + IDX_HEADS, :]
    else:
        small_ref[...] = small

    sga_ref[...] = _sigmoid(_dot(hb, wg_ref[:, 0:D_MODEL])).astype(BF16)
    sgb_ref[...] = _sigmoid(_dot(hb, wg_ref[:, D_MODEL:2 * D_MODEL])).astype(BF16)


def _inproj(x, g, wa, ws, wg, gv, gq, gk, tm, prompt, seq=None):
    n = x.shape[0]
    steps = n // tm
    row = lambda w: pl.BlockSpec((tm, w), lambda i: (i, 0))
    col = lambda h: pl.BlockSpec((h, tm), lambda i: (0, i))
    if prompt:
        per_seq = seq // tm
        kv_shape = jax.ShapeDtypeStruct((n * N_KV_HEADS, HEAD_DIM), F32)
        kv_spec = pl.BlockSpec((tm * N_KV_HEADS, HEAD_DIM), lambda i: (i, 0))
        kit_shape = jax.ShapeDtypeStruct((n // seq, IDX_DIM, seq), F32)
        kit_spec = pl.BlockSpec((1, IDX_DIM, tm), lambda i: (i // per_seq, 0, i % per_seq))
        third = [(kv_shape, kv_spec), (kv_shape, kv_spec), (kit_shape, kit_spec)]
    else:
        third = [(jax.ShapeDtypeStruct((n, KV_W), F32), row(KV_W)),
                 (jax.ShapeDtypeStruct((n, KV_W), F32), row(KV_W)),
                 (jax.ShapeDtypeStruct((n, SMALL_W), F32), row(SMALL_W))]
    out_shape = [jax.ShapeDtypeStruct((n, D_MODEL), BF16),
                 jax.ShapeDtypeStruct((n, A_WIDTH), BF16 if prompt else F32),
                 *[s for s, _ in third],
                 jax.ShapeDtypeStruct((n, D_MODEL), BF16),
                 jax.ShapeDtypeStruct((n, D_MODEL), BF16)]
    out_specs = [row(D_MODEL), row(A_WIDTH), *[s for _, s in third], row(D_MODEL), row(D_MODEL)]
    if prompt:
        out_shape += [jax.ShapeDtypeStruct((D_MODEL, n), BF16),
                      jax.ShapeDtypeStruct((QI_W, n), BF16),
                      jax.ShapeDtypeStruct((IDX_HEADS, n), F32),
                      jax.ShapeDtypeStruct((n // KEY_CHUNK, KV_W, KEY_CHUNK), BF16),
                      jax.ShapeDtypeStruct((n, KV_W), BF16),
                      jax.ShapeDtypeStruct((n, SMALL_W), BF16)]
        out_specs += [col(D_MODEL), col(QI_W), col(IDX_HEADS),
                      pl.BlockSpec((tm // KEY_CHUNK, KV_W, KEY_CHUNK), lambda i: (i, 0, 0)),
                      row(KV_W), row(SMALL_W)]
    else:
        out_shape += [jax.ShapeDtypeStruct((n, D_MODEL), BF16),
                      jax.ShapeDtypeStruct((n, QI_W), BF16)]
        out_specs += [row(D_MODEL), row(QI_W)]
    return pl.pallas_call(
        functools.partial(_inproj_kernel, prompt=prompt),
        grid=(steps,),
        in_specs=[row(D_MODEL), _const_spec((1, D_MODEL)),
                  _const_spec((D_MODEL, WA_W)), _const_spec((D_MODEL, SMALL_W)), _const_spec((D_MODEL, 2 * D_MODEL)),
                  _const_spec((1, A_WIDTH)), _const_spec((1, HEAD_DIM)), _const_spec((1, HEAD_DIM))],
        out_specs=out_specs,
        out_shape=out_shape,
        compiler_params=_params(1),
        name="inproj_prompt" if prompt else "inproj_sample",
    )(x, g, wa, ws, wg, gv, gq, gk)


def _sort_key(x):
    bits = pltpu.bitcast(x, jnp.int32)
    bits = jnp.where(bits == INT_MIN, 0, bits)
    return jnp.where(bits < 0, bits ^ 0x7FFFFFFF, bits)


def _kth_threshold(count_ge, k, shape, total):
    def body(i, carry):
        t_u, cnt_t = carry
        cand_u = t_u | jnp.left_shift(jnp.int32(1), 31 - i)
        cnt = count_ge(cand_u ^ INT_MIN)
        ok = cnt >= k
        return jnp.where(ok, cand_u, t_u), jnp.where(ok, cnt, cnt_t)
    t_u, cnt_t = lax.fori_loop(0, 32, body, (jnp.zeros(shape, jnp.int32), jnp.full(shape, total, F32)))
    return t_u ^ INT_MIN, cnt_t


def _count_rows(keys_ref, rows, pred):
    lanes = keys_ref.shape[1]
    n_acc = 8
    accs = [jnp.zeros((8, lanes), F32) for _ in range(n_acc)]
    for i in range(rows // 8):
        hit = jnp.where(pred(keys_ref[i * 8:(i + 1) * 8, :]), 1.0, 0.0)
        accs[i % n_acc] = accs[i % n_acc] + hit
    while len(accs) > 1:
        accs = [a + b for a, b in zip(accs[0::2], accs[1::2])]
    return jnp.sum(accs[0], axis=0, keepdims=True)


def _mix_kernel(qt_ref, qit_ref, wit_ref, kb_ref, vt_ref, smallb_ref, o_ref,
                keys_ref, bias_ref, thr_ref, need_ref, flag_ref, m_ref, acc_ref, *, k_sel):
    seq = keys_ref.shape[0]
    n = pl.program_id(1)
    t0 = n * Q_BLOCK
    n_att = (n * Q_BLOCK) // ATT_CHUNK + 1
    n_search = (n * Q_BLOCK) // SEARCH_CHUNK + 1
    n_bias = n_att * (ATT_CHUNK // KEY_CHUNK)
    row = lax.broadcasted_iota(jnp.int32, (KEY_CHUNK, Q_BLOCK), 0)
    col = lax.broadcasted_iota(jnp.int32, (KEY_CHUNK, Q_BLOCK), 1)
    row_a = lax.broadcasted_iota(jnp.int32, (ATT_CHUNK, Q_BLOCK), 0)
    col_a = lax.broadcasted_iota(jnp.int32, (ATT_CHUNK, Q_BLOCK), 1)

    qcat = jnp.concatenate([qit_ref[h * IDX_DIM:(h + 1) * IDX_DIM, :] for h in range(IDX_HEADS)], axis=1)
    w = wit_ref[...] * np.float32(IDX_DIM ** -0.5)

    def score_chunk(c, carry):
        r0 = pl.multiple_of(c * ATT_CHUNK, ATT_CHUNK)
        ki = smallb_ref[pl.ds(r0, ATT_CHUNK), :][:, 0:IDX_DIM]
        dots = _dot(ki, qcat)
        sc = w[0:1, :] * jnp.maximum(dots[:, 0:Q_BLOCK], 0.0)
        for h in range(1, IDX_HEADS):
            sc = sc + w[h:h + 1, :] * jnp.maximum(dots[:, h * Q_BLOCK:(h + 1) * Q_BLOCK], 0.0)
        causal = (row_a + r0) <= (col_a + t0)
        keys_ref[pl.ds(r0, ATT_CHUNK), :] = _sort_key(jnp.where(causal, sc, -jnp.inf))
        return carry

    lax.fori_loop(0, n_att, score_chunk, 0)

    few = (n + 1) * Q_BLOCK <= k_sel

    @pl.when(few)
    def _():
        thr_ref[...] = jnp.full((1, Q_BLOCK), KEY_NEG_INF, jnp.int32)
        need_ref[...] = jnp.zeros((1, Q_BLOCK), F32)
        flag_ref[0] = 0.0

    for ns in range(1, seq // SEARCH_CHUNK + 1):
        @pl.when((n_search == ns) & jnp.logical_not(few))
        def _(ns=ns):
            rows = ns * SEARCH_CHUNK
            thr, n_ge = _kth_threshold(lambda t: _count_rows(keys_ref, rows, lambda kk: kk >= t),
                                       k_sel, (1, Q_BLOCK), rows)
            n_gt = _count_rows(keys_ref, rows, lambda kk: kk > thr)
            need = k_sel - n_gt
            tie = jnp.where((thr > KEY_NEG_INF) & (n_ge - n_gt > need), 1.0, 0.0)
            thr_ref[...] = thr
            need_ref[...] = need
            flag_ref[0] = jnp.max(tie)

    thr = thr_ref[...]
    need = need_ref[...]

    @pl.when(flag_ref[0] == 0.0)
    def _():
        thr_eff = jnp.maximum(thr, KEY_NEG_INF + 1)

        def body(c, carry):
            r0 = pl.multiple_of(c * KEY_CHUNK, KEY_CHUNK)
            bias_ref[pl.ds(r0, KEY_CHUNK), :] = jnp.where(keys_ref[pl.ds(r0, KEY_CHUNK), :] >= thr_eff, 0.0, NEG)
            return carry
        lax.fori_loop(0, n_bias, body, 0)

    @pl.when(flag_ref[0] != 0.0)
    def _():
        lower = (lax.broadcasted_iota(jnp.int32, (KEY_CHUNK, KEY_CHUNK), 1)
                 < lax.broadcasted_iota(jnp.int32, (KEY_CHUNK, KEY_CHUNK), 0))
        lower = jnp.where(lower, 1.0, 0.0).astype(BF16)

        def body(c, tie_carry):
            r0 = pl.multiple_of(c * KEY_CHUNK, KEY_CHUNK)
            kk = keys_ref[pl.ds(r0, KEY_CHUNK), :]
            eq = jnp.where(kk == thr, 1.0, 0.0)
            before = _dot(lower, eq.astype(BF16)) + tie_carry
            keep_tie = jnp.where(before < need, eq, 0.0)
            sel = jnp.where(kk > thr, 1.0, keep_tie)
            sel = jnp.where((row + r0) <= (col + t0), sel, 0.0)
            bias_ref[pl.ds(r0, KEY_CHUNK), :] = jnp.where(sel > 0.5, 0.0, NEG)
            return tie_carry + jnp.sum(eq, axis=0, keepdims=True)
        lax.fori_loop(0, n_bias, body, jnp.zeros((1, Q_BLOCK), F32))

    c2 = np.float32(HEAD_DIM ** -0.5 * np.log2(np.e))
    m_ref[...] = jnp.full(m_ref.shape, NEG, F32)
    acc_ref[...] = jnp.zeros(acc_ref.shape, F32)
    ones_rows = jnp.ones((ONES_ROWS, KEY_CHUNK), BF16)
    sub = ATT_CHUNK // KEY_CHUNK

    def attend(c, carry):
        r0 = pl.multiple_of(c * ATT_CHUNK, ATT_CHUNK)
        bias = bias_ref[pl.ds(r0, ATT_CHUNK), :]
        bias4 = jnp.concatenate([bias] * KV_GROUP, axis=1)
        for j in range(N_KV_HEADS):
            qj = jnp.concatenate(
                [qt_ref[(j * KV_GROUP + g) * HEAD_DIM:(j * KV_GROUP + g + 1) * HEAD_DIM, :]
                 for g in range(KV_GROUP)], axis=1)
            kj = kb_ref[pl.ds(r0, ATT_CHUNK), j * HEAD_DIM:(j + 1) * HEAD_DIM]
            s = _dot(kj, qj) + bias4
            m_old = m_ref[j]
            m_new = jnp.maximum(m_old, jnp.max(s, axis=0, keepdims=True))
            alpha = jnp.exp2((m_old - m_new) * c2)
            p = jnp.exp2((s - m_new) * c2).astype(BF16)
            acc = alpha * acc_ref[j]
            for r in range(sub):
                vj = jnp.concatenate([vt_ref[c * sub + r, j * HEAD_DIM:(j + 1) * HEAD_DIM, :], ones_rows], axis=0)
                acc = acc + _dot(vj, p[r * KEY_CHUNK:(r + 1) * KEY_CHUNK, :])
            acc_ref[j] = acc
            m_ref[j] = m_new
        return carry

    lax.fori_loop(0, n_att, attend, 0)

    for j in range(N_KV_HEADS):
        ot = acc_ref[j, 0:HEAD_DIM, :] / acc_ref[j, HEAD_DIM:HEAD_DIM + 1, :]
        for g in range(KV_GROUP):
            h = j * KV_GROUP + g
            o_ref[:, h * HEAD_DIM:(h + 1) * HEAD_DIM] = ot[:, g * Q_BLOCK:(g + 1) * Q_BLOCK].T.astype(BF16)


def _mix(qt, qit, wit, kb, vt, smallb, batch, seq):
    n = batch * seq
    nq = seq // Q_BLOCK
    k_sel = min(TOPK_MAX, seq // 4)
    gq = KV_GROUP * Q_BLOCK
    return pl.pallas_call(
        functools.partial(_mix_kernel, k_sel=k_sel),
        grid=(batch, nq),
        in_specs=[pl.BlockSpec((D_MODEL, Q_BLOCK), lambda b, i: (0, b * nq + i)),
                  pl.BlockSpec((QI_W, Q_BLOCK), lambda b, i: (0, b * nq + i)),
                  pl.BlockSpec((IDX_HEADS, Q_BLOCK), lambda b, i: (0, b * nq + i)),
                  pl.BlockSpec((seq, KV_W), lambda b, i: (b, 0)),
                  pl.BlockSpec((seq // KEY_CHUNK, KV_W, KEY_CHUNK), lambda b, i: (b, 0, 0)),
                  pl.BlockSpec((seq, SMALL_W), lambda b, i: (b, 0))],
        out_specs=pl.BlockSpec((Q_BLOCK, D_MODEL), lambda b, i: (b * nq + i, 0)),
        out_shape=jax.ShapeDtypeStruct((n, D_MODEL), BF16),
        scratch_shapes=[pltpu.VMEM((seq, Q_BLOCK), jnp.int32),
                        pltpu.VMEM((seq, Q_BLOCK), F32),
                        pltpu.VMEM((1, Q_BLOCK), jnp.int32),
                        pltpu.VMEM((1, Q_BLOCK), F32),
                        pltpu.SMEM((1,), F32),
                        pltpu.VMEM((N_KV_HEADS, 1, gq), F32),
                        pltpu.VMEM((N_KV_HEADS, HEAD_DIM + ONES_ROWS, gq), F32)],
        compiler_params=_params(2),
        name="mix_prompt",
    )(qt, qit, wit, kb, vt, smallb)


S_IDX_PAGES = 32
S_ATT_PAGES = 32
T_PAD = 8


def _page_copies(pt_ref, b, first, count, layer, cache_ref, buf_ref, slot, sem):
    return [pltpu.make_async_copy(cache_ref.at[layer, pt_ref[b, first + p]], buf_ref.at[slot, p], sem)
            for p in range(count)]


def _paged_loop(n_groups, copies, compute):
    def start(group):
        for i, c in enumerate(group):
            c.start(priority=i % 2)

    start(copies(0, 0))

    def body(g, carry):
        slot = g % 2

        @pl.when(g + 1 < n_groups)
        def _():
            start(copies(g + 1, 1 - slot))
        for c in copies(g, slot):
            c.wait()
        compute(g, slot)
        return carry

    lax.fori_loop(0, n_groups, body, 0)


def _s_idx_kernel(pt_ref, qi_ref, wb_ref, kin_ref, cache_ref, out_ref, outn_ref, buf_ref, sem_ref, *, layer):
    b = pl.program_id(0)
    n_pages = out_ref.shape[0]
    qi = qi_ref[0]
    wb = wb_ref[0]

    def score(kit, n_pg):
        r = jnp.maximum(_dot(qi, kit) * np.float32(IDX_DIM ** -0.5), 0.0) * jnp.concatenate([wb] * n_pg, axis=1)
        sc = r[0:T_PAD, :]
        for h in range(1, IDX_HEADS):
            sc = sc + r[h * T_PAD:(h + 1) * T_PAD, :]
        return sc

    outn_ref[...] = score(kin_ref[0], 1)

    def copies(g, slot):
        return _page_copies(pt_ref, b, g * S_IDX_PAGES, S_IDX_PAGES, layer, cache_ref, buf_ref, slot, sem_ref.at[slot])

    def compute(g, slot):
        kit = jnp.concatenate([buf_ref[slot, p] for p in range(S_IDX_PAGES)], axis=1).astype(BF16)
        sc = score(kit, S_IDX_PAGES)
        for p in range(S_IDX_PAGES):
            out_ref[g * S_IDX_PAGES + p] = sc[:, p * PAGE_SIZE:(p + 1) * PAGE_SIZE]

    _paged_loop(n_pages // S_IDX_PAGES, copies, compute)


def _s_idx(page_table, qi_rows, wb_rows, kit_new, cache_kidx_t, layer):
    db, n_pages = page_table.shape
    grid_spec = pltpu.PrefetchScalarGridSpec(
        num_scalar_prefetch=1,
        grid=(db,),
        in_specs=[pl.BlockSpec((1, IDX_HEADS * T_PAD, IDX_DIM), lambda b, pt: (b, 0, 0)),
                  pl.BlockSpec((1, IDX_HEADS * T_PAD, PAGE_SIZE), lambda b, pt: (b, 0, 0)),
                  pl.BlockSpec((1, IDX_DIM, PAGE_SIZE), lambda b, pt: (b, 0, 0)),
                  pl.BlockSpec(memory_space=pl.ANY)],
        out_specs=[pl.BlockSpec((n_pages, T_PAD, PAGE_SIZE), lambda b, pt: (0, b, 0)),
                   pl.BlockSpec((T_PAD, PAGE_SIZE), lambda b, pt: (b, 0))],
        scratch_shapes=[pltpu.VMEM((2, S_IDX_PAGES, IDX_DIM, PAGE_SIZE), F32),
                        pltpu.SemaphoreType.DMA((2,))])
    return pl.pallas_call(
        functools.partial(_s_idx_kernel, layer=layer),
        grid_spec=grid_spec,
        out_shape=[jax.ShapeDtypeStruct((n_pages, db * T_PAD, PAGE_SIZE), F32),
                   jax.ShapeDtypeStruct((db * T_PAD, PAGE_SIZE), F32)],
        compiler_params=_params(1),
        name="s_idx",
    )(page_table, qi_rows, wb_rows, kit_new, cache_kidx_t)


def _s_sel_kernel(sc_ref, scn_ref, mask_ref, maskn_ref, keys_ref, *, k_sel, n_new):
    n_pages, rows, _ = sc_ref.shape
    keys_ref[0:n_pages] = _sort_key(sc_ref[...])
    t = lax.broadcasted_iota(jnp.int32, (rows, PAGE_SIZE), 0) % T_PAD
    j = lax.broadcasted_iota(jnp.int32, (rows, PAGE_SIZE), 1)
    new_ok = (j <= t) & (j < n_new)
    keys_ref[n_pages] = _sort_key(jnp.where(new_ok, scn_ref[...], -jnp.inf))

    def count(pred):
        acc = jnp.zeros((rows, PAGE_SIZE), F32)
        for c in range(n_pages + 1):
            acc = acc + jnp.where(pred(keys_ref[c]), 1.0, 0.0)
        return jnp.sum(acc, axis=1, keepdims=True)

    thr, n_ge = _kth_threshold(lambda c: count(lambda kk: kk >= c), k_sel, (rows, 1), (n_pages + 1) * PAGE_SIZE)
    n_gt = count(lambda kk: kk > thr)
    need = k_sel - n_gt
    surplus = jnp.max(jnp.where((n_ge - n_gt > need) & (t[:, 0:1] < n_new), 1.0, 0.0))

    @pl.when(surplus == 0.0)
    def _():
        mask_ref[...] = jnp.where(keys_ref[0:n_pages] >= thr, 1.0, 0.0)
        maskn_ref[...] = jnp.where(new_ok & (keys_ref[n_pages] >= thr), 1.0, 0.0)

    @pl.when(surplus != 0.0)
    def _():
        upper = (lax.broadcasted_iota(jnp.int32, (PAGE_SIZE, PAGE_SIZE), 0)
                 < lax.broadcasted_iota(jnp.int32, (PAGE_SIZE, PAGE_SIZE), 1))
        upper = jnp.where(upper, 1.0, 0.0).astype(BF16)
        ones = jnp.ones((PAGE_SIZE, PAGE_SIZE), BF16)

        def tile(c, carry):
            kk = keys_ref[c]
            eq = jnp.where(kk == thr, 1.0, 0.0)
            eqb = eq.astype(BF16)
            before = _dot(eqb, upper) + carry
            keep_tie = jnp.where(before < need, eq, 0.0)
            return jnp.where(kk > thr, 1.0, keep_tie), carry + _dot(eqb, ones)

        def past_tile(c, carry):
            sel, carry = tile(c, carry)
            mask_ref[c] = sel
            return carry

        carry = lax.fori_loop(0, n_pages, past_tile, jnp.zeros((rows, PAGE_SIZE), F32))
        sel, _ = tile(n_pages, carry)
        maskn_ref[...] = jnp.where(new_ok, sel, 0.0)


def _s_sel(scores, scores_new, k_sel, n_new, rows_per_step):
    n_pages, rows, _ = scores.shape
    return pl.pallas_call(
        functools.partial(_s_sel_kernel, k_sel=k_sel, n_new=n_new),
        grid=(rows // rows_per_step,),
        in_specs=[pl.BlockSpec((n_pages, rows_per_step, PAGE_SIZE), lambda i: (0, i, 0)),
                  pl.BlockSpec((rows_per_step, PAGE_SIZE), lambda i: (i, 0))],
        out_specs=[pl.BlockSpec((n_pages, rows_per_step, PAGE_SIZE), lambda i: (0, i, 0)),
                   pl.BlockSpec((rows_per_step, PAGE_SIZE), lambda i: (i, 0))],
        out_shape=[jax.ShapeDtypeStruct((n_pages, rows, PAGE_SIZE), F32),
                   jax.ShapeDtypeStruct((rows, PAGE_SIZE), F32)],
        scratch_shapes=[pltpu.VMEM((n_pages + 1, rows_per_step, PAGE_SIZE), jnp.int32)],
        compiler_params=_params(1),
        name="s_sel",
    )(scores, scores_new)


def _s_att_kernel(pt_ref, q_ref, mask_ref, maskn_ref, kn_ref, vn_ref, ck_ref, cv_ref, o_ref,
                  kbuf_ref, vbuf_ref, sem_ref, m_ref, l_ref, acc_ref, *, layer):
    b = pl.program_id(0)
    n_pages = mask_ref.shape[0]
    flat = N_KV_HEADS * PAGE_SIZE
    c2 = np.float32(HEAD_DIM ** -0.5 * np.log2(np.e))
    q = q_ref[0]
    spread = (lax.broadcasted_iota(jnp.int32, (PAGE_SIZE, flat), 1) // N_KV_HEADS
              == lax.broadcasted_iota(jnp.int32, (PAGE_SIZE, flat), 0))
    spread = jnp.where(spread, 1.0, 0.0).astype(BF16)

    def update(kf, vf, sel, n_pg):
        sel2 = _dot(sel.astype(BF16), spread)
        sel2 = jnp.concatenate([sel2[p * T_PAD:(p + 1) * T_PAD, :] for p in range(n_pg)], axis=1)
        bias_t = jnp.where(sel2 > 0.5, 0.0, NEG)
        bias_g = jnp.concatenate([bias_t] * KV_GROUP, axis=0)
        parity = lax.broadcasted_iota(jnp.int32, (1, n_pg * flat), 1) % N_KV_HEADS
        bias = jnp.concatenate([bias_g + jnp.where(parity == j, 0.0, NEG) for j in range(N_KV_HEADS)], axis=0)
        s = _dot_nt(q, kf) + bias
        m_old = m_ref[...]
        m_new = jnp.maximum(m_old, jnp.max(s, axis=1, keepdims=True))
        alpha = jnp.exp2((m_old - m_new) * c2)
        p = jnp.exp2((s - m_new) * c2)
        l_ref[...] = alpha * l_ref[...] + jnp.sum(p, axis=1, keepdims=True)
        acc_ref[...] = alpha * acc_ref[...] + _dot(p.astype(BF16), vf)
        m_ref[...] = m_new

    m_ref[...] = jnp.full(m_ref.shape, NEG, F32)
    l_ref[...] = jnp.zeros(l_ref.shape, F32)
    acc_ref[...] = jnp.zeros(acc_ref.shape, F32)
    update(kn_ref[0], vn_ref[0], maskn_ref[...], 1)

    def copies(g, slot):
        first = g * S_ATT_PAGES
        ks = _page_copies(pt_ref, b, first, S_ATT_PAGES, layer, ck_ref, kbuf_ref, slot, sem_ref.at[0, slot])
        vs = _page_copies(pt_ref, b, first, S_ATT_PAGES, layer, cv_ref, vbuf_ref, slot, sem_ref.at[1, slot])
        return [c for kv in zip(ks, vs) for c in kv]

    def compute(g, slot):
        kf = kbuf_ref[slot].reshape(S_ATT_PAGES * flat, HEAD_DIM).astype(BF16)
        vf = vbuf_ref[slot].reshape(S_ATT_PAGES * flat, HEAD_DIM).astype(BF16)
        sel = mask_ref[pl.ds(g * S_ATT_PAGES, S_ATT_PAGES)].reshape(S_ATT_PAGES * T_PAD, PAGE_SIZE)
        update(kf, vf, sel, S_ATT_PAGES)

    _paged_loop(n_pages // S_ATT_PAGES, copies, compute)
    o_ref[0] = acc_ref[...] / l_ref[...]


def _s_att(page_table, q_rows, mask, mask_new, k_new, v_new, cache_k, cache_v, layer):
    db, n_pages = page_table.shape
    rows = N_KV_HEADS * KV_GROUP * T_PAD
    flat = N_KV_HEADS * PAGE_SIZE
    grid_spec = pltpu.PrefetchScalarGridSpec(
        num_scalar_prefetch=1,
        grid=(db,),
        in_specs=[pl.BlockSpec((1, rows, HEAD_DIM), lambda b, pt: (b, 0, 0)),
                  pl.BlockSpec((n_pages, T_PAD, PAGE_SIZE), lambda b, pt: (0, b, 0)),
                  pl.BlockSpec((T_PAD, PAGE_SIZE), lambda b, pt: (b, 0)),
                  pl.BlockSpec((1, flat, HEAD_DIM), lambda b, pt: (b, 0, 0)),
                  pl.BlockSpec((1, flat, HEAD_DIM), lambda b, pt: (b, 0, 0)),
                  pl.BlockSpec(memory_space=pl.ANY),
                  pl.BlockSpec(memory_space=pl.ANY)],
        out_specs=pl.BlockSpec((1, rows, HEAD_DIM), lambda b, pt: (b, 0, 0)),
        scratch_shapes=[pltpu.VMEM((2, S_ATT_PAGES, flat, HEAD_DIM), F32),
                        pltpu.VMEM((2, S_ATT_PAGES, flat, HEAD_DIM), F32),
                        pltpu.SemaphoreType.DMA((2, 2)),
                        pltpu.VMEM((rows, 1), F32),
                        pltpu.VMEM((rows, 1), F32),
                        pltpu.VMEM((rows, HEAD_DIM), F32)])
    return pl.pallas_call(
        functools.partial(_s_att_kernel, layer=layer),
        grid_spec=grid_spec,
        out_shape=jax.ShapeDtypeStruct((db, rows, HEAD_DIM), F32),
        compiler_params=_params(1),
        name="s_att",
    )(page_table, q_rows, mask, mask_new, k_new, v_new, cache_k, cache_v)


def _sample_mixer(page_table, q_b, qi_b, small_f, k_f, vv_f, cache_k_flat, cache_v_flat, cache_kidx_t, layer, dec_seq):
    db, n_pages = page_table.shape
    t = dec_seq
    past = n_pages * PAGE_SIZE
    k_sel = min(TOPK_MAX, (past + t) // 4)
    pad_t = lambda a: jnp.pad(a, [(0, 0)] * (a.ndim - 2) + [(0, T_PAD - t), (0, 0)])

    qi_rows = pad_t(qi_b.reshape(db, t, IDX_HEADS, IDX_DIM).transpose(0, 2, 1, 3))
    qi_rows = qi_rows.reshape(db, IDX_HEADS * T_PAD, IDX_DIM)
    wi = small_f[:, IDX_DIM:IDX_DIM + IDX_HEADS].reshape(db, t, IDX_HEADS).transpose(0, 2, 1)
    wb_rows = jnp.pad(wi, ((0, 0), (0, 0), (0, T_PAD - t))).reshape(db, IDX_HEADS * T_PAD, 1)
    wb_rows = jnp.broadcast_to(wb_rows, (db, IDX_HEADS * T_PAD, PAGE_SIZE))
    kit_new = small_f[:, 0:IDX_DIM].reshape(db, t, IDX_DIM).transpose(0, 2, 1)
    kit_new = jnp.pad(kit_new, ((0, 0), (0, 0), (0, PAGE_SIZE - t))).astype(BF16)

    scores, scores_new = _s_idx(page_table, qi_rows, wb_rows, kit_new, cache_kidx_t, layer)
    rows_per_step = min(db * T_PAD, 64)
    mask, mask_new = _s_sel(scores, scores_new, k_sel, t, rows_per_step)

    q_rows = pad_t(q_b.reshape(db, t, N_KV_HEADS, KV_GROUP, HEAD_DIM).transpose(0, 2, 3, 1, 4))
    q_rows = q_rows.reshape(db, N_KV_HEADS * KV_GROUP * T_PAD, HEAD_DIM)
    flat_new = lambda a: jnp.pad(a.reshape(db, t * N_KV_HEADS, HEAD_DIM),
                                 ((0, 0), (0, (PAGE_SIZE - t) * N_KV_HEADS), (0, 0))).astype(BF16)
    o = _s_att(page_table, q_rows, mask, mask_new, flat_new(k_f), flat_new(vv_f), cache_k_flat, cache_v_flat, layer)
    o = o.reshape(db, N_KV_HEADS, KV_GROUP, T_PAD, HEAD_DIM)[:, :, :, :t]
    return o.transpose(0, 3, 1, 2, 4).reshape(db * t, N_HEADS * HEAD_DIM).astype(BF16)


def _post_kernel(x_ref, u_ref, v_ref, attn_ref, sga_ref, sgb_ref, ws_ref, bias_ref, wpa_ref, wpb_ref, wout_ref,
                 o_ref, a_ref, *, block):
    tm = x_ref.shape[0]
    r = lax.broadcasted_iota(jnp.int32, (CHUNK, CHUNK), 0)
    c = lax.broadcasted_iota(jnp.int32, (CHUNK, CHUNK), 1)
    allowed = (r >= c) & ((r // block) == (c // block))
    bias = bias_ref[...]
    for g in range(A_GROUPS):
        wg = jnp.where(allowed, ws_ref[g], 0.0).astype(BF16)
        for i in range(tm // CHUNK):
            rows = slice(i * CHUNK, (i + 1) * CHUNK)
            cols = slice(g * CHUNK, (g + 1) * CHUNK)
            sv = _dot(wg, v_ref[rows, cols]) + bias[:, cols]
            a_ref[rows, cols] = (u_ref[rows, cols].astype(F32) * sv).astype(BF16)
    merged = (sga_ref[...].astype(F32) * _dot(a_ref[...], wpa_ref[...])
              + sgb_ref[...].astype(F32) * _dot(attn_ref[...], wpb_ref[...]))
    o_ref[...] = x_ref[...] + _dot(merged.astype(BF16), wout_ref[...])


def _post(x, u, v, attn, sga, sgb, ws, bias, wpa, wpb, wout, tm, block):
    n = x.shape[0]
    row = lambda dt: pl.BlockSpec((tm, D_MODEL), lambda i: (i, 0))
    return pl.pallas_call(
        functools.partial(_post_kernel, block=block),
        grid=(n // tm,),
        in_specs=[row(F32), row(BF16), row(BF16), row(BF16), row(BF16), row(BF16),
                  _const_spec((A_GROUPS, CHUNK, CHUNK)), _const_spec((CHUNK, A_WIDTH)),
                  _const_spec((A_WIDTH, D_MODEL)), _const_spec((N_HEADS * HEAD_DIM, D_MODEL)),
                  _const_spec((D_MODEL, D_MODEL))],
        out_specs=row(F32),
        out_shape=jax.ShapeDtypeStruct((n, D_MODEL), F32),
        scratch_shapes=[pltpu.VMEM((tm, A_WIDTH), BF16)],
        compiler_params=_params(1),
        name="post",
    )(x, u, v, attn, sga, sgb, ws, bias, wpa, wpb, wout)


def kernel(x_prompt, x_sample, cache_k, cache_v, cache_kidx, page_table, g_ffn1, w_up1, w_down1, g_mix, w_in, g_v, g_q, g_k, w_s, b_s, w_pa, w_pb, w_out, g_ffn2, w_up2, w_down2):
    batch, seq, _ = x_prompt.shape
    db, dec_seq, _ = x_sample.shape
    depth = w_in.shape[0]
    n_p, n_s = batch * seq, db * dec_seq
    assert seq % ATT_CHUNK == 0 and n_s % CHUNK == 0 and CHUNK % dec_seq == 0 and dec_seq <= T_PAD
    tm_p = 512 if n_p % 512 == 0 else KEY_CHUNK
    tm_s = CHUNK

    xp = x_prompt.reshape(n_p, D_MODEL)
    xs = x_sample.reshape(n_s, D_MODEL)
    n_pool = cache_k.shape[1]
    cache_k_flat = cache_k.reshape(depth, n_pool, PAGE_SIZE * N_KV_HEADS, HEAD_DIM)
    cache_v_flat = cache_v.reshape(depth, n_pool, PAGE_SIZE * N_KV_HEADS, HEAD_DIM)
    cache_kidx_t = jnp.swapaxes(cache_kidx, 2, 3)
    outs_p = [[] for _ in range(3)]
    outs = [[] for _ in range(4)]
    for l in range(depth):
        row = lambda a: a[l].reshape(1, -1)
        wup1, wdn1 = w_up1[l].astype(BF16), w_down1[l].astype(BF16)
        wup2, wdn2 = w_up2[l].astype(BF16), w_down2[l].astype(BF16)
        wa = w_in[l, :, 0:WA_W].astype(BF16)
        n_small = IDX_DIM + IDX_HEADS
        ws = jnp.pad(w_in[l, :, WA_W:WA_W + n_small], ((0, 0), (0, SMALL_W - n_small))).astype(BF16)
        wg = w_in[l, :, WA_W + n_small:].astype(BF16)
        wpa, wpb, wout = w_pa[l].astype(BF16), w_pb[l].astype(BF16), w_out[l].astype(BF16)
        ws_p = w_s[l]
        bias_p = jnp.repeat(b_s[l].T, CHUNK, axis=1)
        ws_s = jnp.tile(w_s[l, :, 0:dec_seq, 0:dec_seq], (1, CHUNK // dec_seq, CHUNK // dec_seq))
        bias_s = jnp.repeat(jnp.tile(b_s[l, :, 0:dec_seq].T, (CHUNK // dec_seq, 1)), CHUNK, axis=1)

        x1 = _ffn(xp, row(g_ffn1), wup1, wdn1, tm_p)
        (u, v, k_p, v_p, kit_p, sga, sgb, qt, qit, wit, vt, kb, smallb) = _inproj(
            x1, row(g_mix), wa, ws, wg, row(g_v), row(g_q), row(g_k), tm_p, True, seq=seq)
        outs_p[0].append(k_p.reshape(batch, seq, N_KV_HEADS, HEAD_DIM))
        outs_p[1].append(v_p.reshape(batch, seq, N_KV_HEADS, HEAD_DIM))
        outs_p[2].append(jnp.swapaxes(kit_p, 1, 2))
        attn = _mix(qt, qit, wit, kb, vt, smallb, batch, seq)
        x2 = _post(x1, u, v, attn, sga, sgb, ws_p, bias_p, wpa, wpb, wout, tm_p, CHUNK)
        xp = _ffn(x2, row(g_ffn2), wup2, wdn2, tm_p)

        x1 = _ffn(xs, row(g_ffn1), wup1, wdn1, tm_s)
        (u, v, k_f, vv_f, small_f, sga, sgb, q_b, qi_b) = _inproj(
            x1, row(g_mix), wa, ws, wg, row(g_v), row(g_q), row(g_k), tm_s, False)
        attn = _sample_mixer(page_table, q_b, qi_b, small_f, k_f, vv_f,
                             cache_k_flat, cache_v_flat, cache_kidx_t, l, dec_seq)
        x2 = _post(x1, u.astype(BF16), v.astype(BF16), attn, sga, sgb, ws_s, bias_s, wpa, wpb, wout, tm_s, dec_seq)
        xs = _ffn(x2, row(g_ffn2), wup2, wdn2, tm_s)
        outs[0].append(k_f.reshape(db, dec_seq, N_KV_HEADS, HEAD_DIM))
        outs[1].append(vv_f.reshape(db, dec_seq, N_KV_HEADS, HEAD_DIM))
        outs[2].append(small_f[:, 0:IDX_DIM].reshape(db, dec_seq, IDX_DIM))
        outs[3].append(v.reshape(db, dec_seq, A_WIDTH))

    return (xp.reshape(batch, seq, D_MODEL), xs.reshape(db, dec_seq, D_MODEL),
            *[jnp.stack(o) for o in outs_p], *[jnp.stack(o) for o in outs])
```

```python
import functools

import jax
import jax.numpy as jnp
import numpy as np
from jax import lax
from jax.experimental import pallas as pl
from jax.experimental.pallas import tpu as pltpu

D_MODEL = 1024
D_FF = 2816
CHUNK = 128
A_GROUPS = 8
A_WIDTH = D_MODEL
N_HEADS = 8
HEAD_DIM = 128
N_KV_HEADS = 2
KV_GROUP = N_HEADS // N_KV_HEADS
IDX_HEADS = 8
IDX_DIM = 64
TOPK_MAX = 256
Q_BLOCK = 512
PAGE_SIZE = 128
EPS = 1e-6

KV_W = N_KV_HEADS * HEAD_DIM
QI_W = IDX_HEADS * IDX_DIM
WA_W = 3 * D_MODEL + 2 * KV_W + QI_W
SMALL_W = 128
F_CHUNK = 256
KEY_CHUNK = 256
ATT_CHUNK = 512
SEARCH_CHUNK = 256
ONES_ROWS = 16
NEG = -1e30
INT_MIN = -2 ** 31
KEY_NEG_INF = INT_MIN + 0x7FFFFF
VMEM_LIMIT = 56 * 1024 * 1024

BF16 = jnp.bfloat16
F32 = jnp.float32


def _dot(a, b):
    return jnp.dot(a, b, preferred_element_type=F32)


def _dot_nt(a, b):
    return lax.dot_general(a, b, (((1,), (1,)), ((), ())), preferred_element_type=F32)


def _rms(x, g):
    return x * lax.rsqrt(jnp.mean(x * x, axis=-1, keepdims=True) + EPS) * g


def _sigmoid(x):
    return 1.0 / (1.0 + jnp.exp(-x))


def _gelu(x):
    c = np.float32(np.sqrt(2.0 / np.pi))
    return x * (0.5 * (1.0 + jnp.tanh(c * (x + 0.044715 * (x * x * x)))))


def _const_spec(shape):
    nd = len(shape)
    return pl.BlockSpec(shape, lambda *_: (0,) * nd, pipeline_mode=pl.Buffered(1))


def _params(n_axes):
    return pltpu.CompilerParams(dimension_semantics=("arbitrary",) * n_axes, vmem_limit_bytes=VMEM_LIMIT)


def _ffn_apply(x, g_ref, wup_ref, wdn_ref, h_ref):
    xb = _rms(x, g_ref[...]).astype(BF16)
    for j in range(D_FF // F_CHUNK):
        gate = _dot(xb, wup_ref[:, j * F_CHUNK:(j + 1) * F_CHUNK])
        up = _dot(xb, wup_ref[:, D_FF + j * F_CHUNK:D_FF + (j + 1) * F_CHUNK])
        h_ref[:, j * F_CHUNK:(j + 1) * F_CHUNK] = (gate * _sigmoid(gate) * up).astype(BF16)
    return x + 0.5 * _dot(h_ref[...], wdn_ref[...])


def _ffn_kernel(x_ref, g_ref, wup_ref, wdn_ref, o_ref, h_ref):
    o_ref[...] = _ffn_apply(x_ref[...], g_ref, wup_ref, wdn_ref, h_ref)


def _ffn(x, g, wup, wdn, tm):
    n = x.shape[0]
    return pl.pallas_call(
        _ffn_kernel,
        grid=(n // tm,),
        in_specs=[pl.BlockSpec((tm, D_MODEL), lambda i: (i, 0)),
                  _const_spec((1, D_MODEL)),
                  _const_spec((D_MODEL, 2 * D_FF)),
                  _const_spec((D_FF, D_MODEL))],
        out_specs=pl.BlockSpec((tm, D_MODEL), lambda i: (i, 0)),
        out_shape=jax.ShapeDtypeStruct((n, D_MODEL), F32),
        scratch_shapes=[pltpu.VMEM((tm, D_FF), BF16)],
        compiler_params=_params(1),
        name="ffn",
    )(x, g, wup, wdn)


def _inproj_kernel(x_ref, g_ref, wa_ref, ws_ref, wg_ref, gv_ref, gq_ref, gk_ref, *out_refs, prompt):
    if prompt:
        (u_ref, v_ref, k_ref, vv_ref, kit_ref, sga_ref, sgb_ref,
         qt_ref, qit_ref, wit_ref, vt_ref, kb_ref, smallb_ref) = out_refs
    else:
        (u_ref, v_ref, k_ref, vv_ref, small_ref, sga_ref, sgb_ref, q_ref, qi_ref) = out_refs
    tm = x_ref.shape[0]
    hb = _rms(x_ref[...], g_ref[...]).astype(BF16)

    u_ref[...] = _gelu(_dot(hb, wa_ref[:, 0:D_MODEL])).astype(u_ref.dtype)
    v = _gelu(_dot(hb, wa_ref[:, D_MODEL:2 * D_MODEL]))
    v_ref[...] = _rms(v, gv_ref[...]).astype(v_ref.dtype)

    q = _dot(hb, wa_ref[:, 2 * D_MODEL:3 * D_MODEL])
    gq = gq_ref[...]
    for h in range(N_HEADS):
        qh = _rms(q[:, h * HEAD_DIM:(h + 1) * HEAD_DIM], gq)
        if prompt:
            qt_ref[h * HEAD_DIM:(h + 1) * HEAD_DIM, :] = qh.T.astype(BF16)
        else:
            q_ref[:, h * HEAD_DIM:(h + 1) * HEAD_DIM] = qh.astype(BF16)

    o = 3 * D_MODEL
    kk = _dot(hb, wa_ref[:, o:o + KV_W])
    gk = gk_ref[...]
    vv = _dot(hb, wa_ref[:, o + KV_W:o + 2 * KV_W])
    for j in range(N_KV_HEADS):
        kj = _rms(kk[:, j * HEAD_DIM:(j + 1) * HEAD_DIM], gk)
        if prompt:
            k_ref[pl.ds(j, tm, stride=N_KV_HEADS), :] = kj
            vv_ref[pl.ds(j, tm, stride=N_KV_HEADS), :] = vv[:, j * HEAD_DIM:(j + 1) * HEAD_DIM]
            kb_ref[:, j * HEAD_DIM:(j + 1) * HEAD_DIM] = kj.astype(BF16)
        else:
            k_ref[:, j * HEAD_DIM:(j + 1) * HEAD_DIM] = kj
    if prompt:
        for r in range(tm // KEY_CHUNK):
            vt_ref[r] = vv[r * KEY_CHUNK:(r + 1) * KEY_CHUNK, :].T.astype(BF16)
    else:
        vv_ref[...] = vv

    qi = _dot(hb, wa_ref[:, o + 2 * KV_W:o + 2 * KV_W + QI_W])
    if prompt:
        qit_ref[...] = qi.T.astype(BF16)
    else:
        qi_ref[...] = qi.astype(BF16)

    small = _dot(hb, ws_ref[...])
    lane = lax.broadcasted_iota(jnp.int32, small.shape, 1)
    is_w = (lane >= IDX_DIM) & (lane < IDX_DIM + IDX_HEADS)
    small = small * jnp.where(is_w, np.float32(IDX_HEADS ** -0.5), np.float32(1.0))
    if prompt:
        smallb_ref[...] = small.astype(BF16)
        small_t = small.T
        kit_ref[0] = small_t[0:IDX_DIM, :]
        wit_ref[...] = small_t[IDX_DIM:IDX_DIM + IDX_HEADS, :]
    else:
        small_ref[...] = small

    sga_ref[...] = _sigmoid(_dot(hb, wg_ref[:, 0:D_MODEL])).astype(BF16)
    sgb_ref[...] = _sigmoid(_dot(hb, wg_ref[:, D_MODEL:2 * D_MODEL])).astype(BF16)


def _inproj(x, g, wa, ws, wg, gv, gq, gk, tm, prompt, seq=None):
    n = x.shape[0]
    steps = n // tm
    row = lambda w: pl.BlockSpec((tm, w), lambda i: (i, 0))
    col = lambda h: pl.BlockSpec((h, tm), lambda i: (0, i))
    if prompt:
        per_seq = seq // tm
        kv_shape = jax.ShapeDtypeStruct((n * N_KV_HEADS, HEAD_DIM), F32)
        kv_spec = pl.BlockSpec((tm * N_KV_HEADS, HEAD_DIM), lambda i: (i, 0))
        kit_shape = jax.ShapeDtypeStruct((n // seq, IDX_DIM, seq), F32)
        kit_spec = pl.BlockSpec((1, IDX_DIM, tm), lambda i: (i // per_seq, 0, i % per_seq))
        third = [(kv_shape, kv_spec), (kv_shape, kv_spec), (kit_shape, kit_spec)]
    else:
        third = [(jax.ShapeDtypeStruct((n, KV_W), F32), row(KV_W)),
                 (jax.ShapeDtypeStruct((n, KV_W), F32), row(KV_W)),
                 (jax.ShapeDtypeStruct((n, SMALL_W), F32), row(SMALL_W))]
    out_shape = [jax.ShapeDtypeStruct((n, D_MODEL), BF16),
                 jax.ShapeDtypeStruct((n, A_WIDTH), BF16 if prompt else F32),
                 *[s for s, _ in third],
                 jax.ShapeDtypeStruct((n, D_MODEL), BF16),
                 jax.ShapeDtypeStruct((n, D_MODEL), BF16)]
    out_specs = [row(D_MODEL), row(A_WIDTH), *[s for _, s in third], row(D_MODEL), row(D_MODEL)]
    if prompt:
        out_shape += [jax.ShapeDtypeStruct((D_MODEL, n), BF16),
                      jax.ShapeDtypeStruct((QI_W, n), BF16),
                      jax.ShapeDtypeStruct((IDX_HEADS, n), F32),
                      jax.ShapeDtypeStruct((n // KEY_CHUNK, KV_W, KEY_CHUNK), BF16),
                      jax.ShapeDtypeStruct((n, KV_W), BF16),
                      jax.ShapeDtypeStruct((n, SMALL_W), BF16)]
        out_specs += [col(D_MODEL), col(QI_W), col(IDX_HEADS),
                      pl.BlockSpec((tm // KEY_CHUNK, KV_W, KEY_CHUNK), lambda i: (i, 0, 0)),
                      row(KV_W), row(SMALL_W)]
    else:
        out_shape += [jax.ShapeDtypeStruct((n, D_MODEL), BF16),
                      jax.ShapeDtypeStruct((n, QI_W), BF16)]
        out_specs += [row(D_MODEL), row(QI_W)]
    return pl.pallas_call(
        functools.partial(_inproj_kernel, prompt=prompt),
        grid=(steps,),
        in_specs=[row(D_MODEL), _const_spec((1, D_MODEL)),
                  _const_spec((D_MODEL, WA_W)), _const_spec((D_MODEL, SMALL_W)), _const_spec((D_MODEL, 2 * D_MODEL)),
                  _const_spec((1, A_WIDTH)), _const_spec((1, HEAD_DIM)), _const_spec((1, HEAD_DIM))],
        out_specs=out_specs,
        out_shape=out_shape,
        compiler_params=_params(1),
        name="inproj_prompt" if prompt else "inproj_sample",
    )(x, g, wa, ws, wg, gv, gq, gk)


def _sort_key(x):
    bits = pltpu.bitcast(x, jnp.int32)
    bits = jnp.where(bits == INT_MIN, 0, bits)
    return jnp.where(bits < 0, bits ^ 0x7FFFFFFF, bits)


def _kth_threshold(count_ge, k, shape, total):
    def body(i, carry):
        t_u, cnt_t = carry
        cand_u = t_u | jnp.left_shift(jnp.int32(1), 31 - i)
        cnt = count_ge(cand_u ^ INT_MIN)
        ok = cnt >= k
        return jnp.where(ok, cand_u, t_u), jnp.where(ok, cnt, cnt_t)
    t_u, cnt_t = lax.fori_loop(0, 32, body, (jnp.zeros(shape, jnp.int32), jnp.full(shape, total, F32)))
    return t_u ^ INT_MIN, cnt_t


def _count_rows(keys_ref, rows, pred):
    lanes = keys_ref.shape[1]
    n_acc = 8
    accs = [jnp.zeros((8, lanes), F32) for _ in range(n_acc)]
    for i in range(rows // 8):
        hit = jnp.where(pred(keys_ref[i * 8:(i + 1) * 8, :]), 1.0, 0.0)
        accs[i % n_acc] = accs[i % n_acc] + hit
    while len(accs) > 1:
        accs = [a + b for a, b in zip(accs[0::2], accs[1::2])]
    return jnp.sum(accs[0], axis=0, keepdims=True)


def _mix_kernel(qt_ref, qit_ref, wit_ref, kb_ref, vt_ref, smallb_ref, o_ref,
                keys_ref, bias_ref, thr_ref, need_ref, flag_ref, m_ref, acc_ref, *, k_sel):
    seq = keys_ref.shape[0]
    n = pl.program_id(1)
    t0 = n * Q_BLOCK
    last = (n + 1) * Q_BLOCK - 1
    n_att = last // ATT_CHUNK + 1
    n_search = last // SEARCH_CHUNK + 1
    n_bias = n_att * (ATT_CHUNK // KEY_CHUNK)
    row = lax.broadcasted_iota(jnp.int32, (KEY_CHUNK, Q_BLOCK), 0)
    col = lax.broadcasted_iota(jnp.int32, (KEY_CHUNK, Q_BLOCK), 1)
    row_a = lax.broadcasted_iota(jnp.int32, (ATT_CHUNK, Q_BLOCK), 0)
    col_a = lax.broadcasted_iota(jnp.int32, (ATT_CHUNK, Q_BLOCK), 1)

    qcat = jnp.concatenate([qit_ref[h * IDX_DIM:(h + 1) * IDX_DIM, :] for h in range(IDX_HEADS)], axis=1)
    w = wit_ref[...] * np.float32(IDX_DIM ** -0.5)

    def score_chunk(c, carry):
        r0 = pl.multiple_of(c * ATT_CHUNK, ATT_CHUNK)
        ki = smallb_ref[pl.ds(r0, ATT_CHUNK), :][:, 0:IDX_DIM]
        dots = _dot(ki, qcat)
        sc = w[0:1, :] * jnp.maximum(dots[:, 0:Q_BLOCK], 0.0)
        for h in range(1, IDX_HEADS):
            sc = sc + w[h:h + 1, :] * jnp.maximum(dots[:, h * Q_BLOCK:(h + 1) * Q_BLOCK], 0.0)
        causal = (row_a + r0) <= (col_a + t0)
        keys_ref[pl.ds(r0, ATT_CHUNK), :] = _sort_key(jnp.where(causal, sc, -jnp.inf))
        return carry

    lax.fori_loop(0, n_att, score_chunk, 0)

    few = (n + 1) * Q_BLOCK <= k_sel

    @pl.when(few)
    def _():
        thr_ref[...] = jnp.full((1, Q_BLOCK), KEY_NEG_INF, jnp.int32)
        need_ref[...] = jnp.zeros((1, Q_BLOCK), F32)
        flag_ref[0] = 0.0

    for ns in range(1, seq // SEARCH_CHUNK + 1):
        @pl.when((n_search == ns) & jnp.logical_not(few))
        def _(ns=ns):
            rows = ns * SEARCH_CHUNK
            thr, n_ge = _kth_threshold(lambda t: _count_rows(keys_ref, rows, lambda kk: kk >= t),
                                       k_sel, (1, Q_BLOCK), rows)
            n_gt = _count_rows(keys_ref, rows, lambda kk: kk > thr)
            need = k_sel - n_gt
            tie = jnp.where((thr > KEY_NEG_INF) & (n_ge - n_gt > need), 1.0, 0.0)
            thr_ref[...] = thr
            need_ref[...] = need
            flag_ref[0] = jnp.max(tie)

    thr = thr_ref[...]
    need = need_ref[...]

    @pl.when(flag_ref[0] == 0.0)
    def _():
        thr_eff = jnp.maximum(thr, KEY_NEG_INF + 1)

        def body(c, carry):
            r0 = pl.multiple_of(c * KEY_CHUNK, KEY_CHUNK)
            bias_ref[pl.ds(r0, KEY_CHUNK), :] = jnp.where(keys_ref[pl.ds(r0, KEY_CHUNK), :] >= thr_eff, 0.0, NEG)
            return carry
        lax.fori_loop(0, n_bias, body, 0)

    @pl.when(flag_ref[0] != 0.0)
    def _():
        lower = (lax.broadcasted_iota(jnp.int32, (KEY_CHUNK, KEY_CHUNK), 1)
                 < lax.broadcasted_iota(jnp.int32, (KEY_CHUNK, KEY_CHUNK), 0))
        lower = jnp.where(lower, 1.0, 0.0).astype(BF16)

        def body(c, tie_carry):
            r0 = pl.multiple_of(c * KEY_CHUNK, KEY_CHUNK)
            kk = keys_ref[pl.ds(r0, KEY_CHUNK), :]
            eq = jnp.where(kk == thr, 1.0, 0.0)
            before = _dot(lower, eq.astype(BF16)) + tie_carry
            keep_tie = jnp.where(before < need, eq, 0.0)
            sel = jnp.where(kk > thr, 1.0, keep_tie)
            sel = jnp.where((row + r0) <= (col + t0), sel, 0.0)
            bias_ref[pl.ds(r0, KEY_CHUNK), :] = jnp.where(sel > 0.5, 0.0, NEG)
            return tie_carry + jnp.sum(eq, axis=0, keepdims=True)
        lax.fori_loop(0, n_bias, body, jnp.zeros((1, Q_BLOCK), F32))

    c2 = np.float32(HEAD_DIM ** -0.5 * np.log2(np.e))
    m_ref[...] = jnp.full(m_ref.shape, NEG, F32)
    acc_ref[...] = jnp.zeros(acc_ref.shape, F32)
    ones_rows = jnp.ones((ONES_ROWS, KEY_CHUNK), BF16)
    sub = ATT_CHUNK // KEY_CHUNK

    def attend(c, carry):
        r0 = pl.multiple_of(c * ATT_CHUNK, ATT_CHUNK)
        bias = bias_ref[pl.ds(r0, ATT_CHUNK), :]
        bias4 = jnp.concatenate([bias] * KV_GROUP, axis=1)
        for j in range(N_KV_HEADS):
            qj = jnp.concatenate(
                [qt_ref[(j * KV_GROUP + g) * HEAD_DIM:(j * KV_GROUP + g + 1) * HEAD_DIM, :]
                 for g in range(KV_GROUP)], axis=1)
            kj = kb_ref[pl.ds(r0, ATT_CHUNK), j * HEAD_DIM:(j + 1) * HEAD_DIM]
            s = _dot(kj, qj) + bias4
            m_old = m_ref[j]
            m_new = jnp.maximum(m_old, jnp.max(s, axis=0, keepdims=True))
            alpha = jnp.exp2((m_old - m_new) * c2)
            p = jnp.exp2((s - m_new) * c2).astype(BF16)
            acc = alpha * acc_ref[j]
            for r in range(sub):
                vj = jnp.concatenate([vt_ref[c * sub + r, j * HEAD_DIM:(j + 1) * HEAD_DIM, :], ones_rows], axis=0)
                acc = acc + _dot(vj, p[r * KEY_CHUNK:(r + 1) * KEY_CHUNK, :])
            acc_ref[j] = acc
            m_ref[j] = m_new
        return carry

    lax.fori_loop(0, n_att, attend, 0)

    for j in range(N_KV_HEADS):
        ot = acc_ref[j, 0:HEAD_DIM, :] / acc_ref[j, HEAD_DIM:HEAD_DIM + 1, :]
        for g in range(KV_GROUP):
            h = j * KV_GROUP + g
            o_ref[:, h * HEAD_DIM:(h + 1) * HEAD_DIM] = ot[:, g * Q_BLOCK:(g + 1) * Q_BLOCK].T.astype(BF16)


def _mix(qt, qit, wit, kb, vt, smallb, batch, seq):
    n = batch * seq
    nq = seq // Q_BLOCK
    k_sel = min(TOPK_MAX, seq // 4)
    gq = KV_GROUP * Q_BLOCK
    return pl.pallas_call(
        functools.partial(_mix_kernel, k_sel=k_sel),
        grid=(batch, nq),
        in_specs=[pl.BlockSpec((D_MODEL, Q_BLOCK), lambda b, i: (0, b * nq + i)),
                  pl.BlockSpec((QI_W, Q_BLOCK), lambda b, i: (0, b * nq + i)),
                  pl.BlockSpec((IDX_HEADS, Q_BLOCK), lambda b, i: (0, b * nq + i)),
                  pl.BlockSpec((seq, KV_W), lambda b, i: (b, 0)),
                  pl.BlockSpec((seq // KEY_CHUNK, KV_W, KEY_CHUNK), lambda b, i: (b, 0, 0)),
                  pl.BlockSpec((seq, SMALL_W), lambda b, i: (b, 0))],
        out_specs=pl.BlockSpec((Q_BLOCK, D_MODEL), lambda b, i: (b * nq + i, 0)),
        out_shape=jax.ShapeDtypeStruct((n, D_MODEL), BF16),
        scratch_shapes=[pltpu.VMEM((seq, Q_BLOCK), jnp.int32),
                        pltpu.VMEM((seq, Q_BLOCK), F32),
                        pltpu.VMEM((1, Q_BLOCK), jnp.int32),
                        pltpu.VMEM((1, Q_BLOCK), F32),
                        pltpu.SMEM((1,), F32),
                        pltpu.VMEM((N_KV_HEADS, 1, gq), F32),
                        pltpu.VMEM((N_KV_HEADS, HEAD_DIM + ONES_ROWS, gq), F32)],
        compiler_params=_params(2),
        name="mix_prompt",
    )(qt, qit, wit, kb, vt, smallb)


S_IDX_PAGES = 32
S_ATT_PAGES = 32
T_PAD = 8


def _page_copies(pt_ref, b, first, count, layer, cache_ref, buf_ref, slot, sem):
    return [pltpu.make_async_copy(cache_ref.at[layer, pt_ref[b, first + p]], buf_ref.at[slot, p], sem)
            for p in range(count)]


def _paged_loop(n_groups, copies, compute):
    def start(group):
        for i, c in enumerate(group):
            c.start(priority=i % 2)

    start(copies(0, 0))

    def body(g, carry):
        slot = g % 2

        @pl.when(g + 1 < n_groups)
        def _():
            start(copies(g + 1, 1 - slot))
        for c in copies(g, slot):
            c.wait()
        compute(g, slot)
        return carry

    lax.fori_loop(0, n_groups, body, 0)


def _s_idx_kernel(pt_ref, qi_ref, wb_ref, kin_ref, cache_ref, out_ref, outn_ref, buf_ref, sem_ref, *, layer):
    b = pl.program_id(0)
    n_pages = out_ref.shape[0]
    qi = qi_ref[0]
    wb = wb_ref[0]

    def score(kit, n_pg):
        r = jnp.maximum(_dot(qi, kit) * np.float32(IDX_DIM ** -0.5), 0.0) * jnp.concatenate([wb] * n_pg, axis=1)
        sc = r[0:T_PAD, :]
        for h in range(1, IDX_HEADS):
            sc = sc + r[h * T_PAD:(h + 1) * T_PAD, :]
        return sc

    outn_ref[...] = score(kin_ref[0], 1)

    def copies(g, slot):
        return _page_copies(pt_ref, b, g * S_IDX_PAGES, S_IDX_PAGES, layer, cache_ref, buf_ref, slot, sem_ref.at[slot])

    def compute(g, slot):
        kit = jnp.concatenate([buf_ref[slot, p] for p in range(S_IDX_PAGES)], axis=1).astype(BF16)
        sc = score(kit, S_IDX_PAGES)
        for p in range(S_IDX_PAGES):
            out_ref[g * S_IDX_PAGES + p] = sc[:, p * PAGE_SIZE:(p + 1) * PAGE_SIZE]

    _paged_loop(n_pages // S_IDX_PAGES, copies, compute)


def _s_idx(page_table, qi_rows, wb_rows, kit_new, cache_kidx_t, layer):
    db, n_pages = page_table.shape
    grid_spec = pltpu.PrefetchScalarGridSpec(
        num_scalar_prefetch=1,
        grid=(db,),
        in_specs=[pl.BlockSpec((1, IDX_HEADS * T_PAD, IDX_DIM), lambda b, pt: (b, 0, 0)),
                  pl.BlockSpec((1, IDX_HEADS * T_PAD, PAGE_SIZE), lambda b, pt: (b, 0, 0)),
                  pl.BlockSpec((1, IDX_DIM, PAGE_SIZE), lambda b, pt: (b, 0, 0)),
                  pl.BlockSpec(memory_space=pl.ANY)],
        out_specs=[pl.BlockSpec((n_pages, T_PAD, PAGE_SIZE), lambda b, pt: (0, b, 0)),
                   pl.BlockSpec((T_PAD, PAGE_SIZE), lambda b, pt: (b, 0))],
        scratch_shapes=[pltpu.VMEM((2, S_IDX_PAGES, IDX_DIM, PAGE_SIZE), F32),
                        pltpu.SemaphoreType.DMA((2,))])
    return pl.pallas_call(
        functools.partial(_s_idx_kernel, layer=layer),
        grid_spec=grid_spec,
        out_shape=[jax.ShapeDtypeStruct((n_pages, db * T_PAD, PAGE_SIZE), F32),
                   jax.ShapeDtypeStruct((db * T_PAD, PAGE_SIZE), F32)],
        compiler_params=_params(1),
        name="s_idx",
    )(page_table, qi_rows, wb_rows, kit_new, cache_kidx_t)


def _s_sel_kernel(sc_ref, scn_ref, mask_ref, maskn_ref, keys_ref, *, k_sel, n_new):
    n_pages, rows, _ = sc_ref.shape
    keys_ref[0:n_pages] = _sort_key(sc_ref[...])
    t = lax.broadcasted_iota(jnp.int32, (rows, PAGE_SIZE), 0) % T_PAD
    j = lax.broadcasted_iota(jnp.int32, (rows, PAGE_SIZE), 1)
    new_ok = (j <= t) & (j < n_new)
    keys_ref[n_pages] = _sort_key(jnp.where(new_ok, scn_ref[...], -jnp.inf))

    def count(pred):
        acc = jnp.zeros((rows, PAGE_SIZE), F32)
        for c in range(n_pages + 1):
            acc = acc + jnp.where(pred(keys_ref[c]), 1.0, 0.0)
        return jnp.sum(acc, axis=1, keepdims=True)

    thr, n_ge = _kth_threshold(lambda c: count(lambda kk: kk >= c), k_sel, (rows, 1), (n_pages + 1) * PAGE_SIZE)
    n_gt = count(lambda kk: kk > thr)
    need = k_sel - n_gt
    surplus = jnp.max(jnp.where((n_ge - n_gt > need) & (t[:, 0:1] < n_new), 1.0, 0.0))

    @pl.when(surplus == 0.0)
    def _():
        mask_ref[...] = jnp.where(keys_ref[0:n_pages] >= thr, 1.0, 0.0)
        maskn_ref[...] = jnp.where(new_ok & (keys_ref[n_pages] >= thr), 1.0, 0.0)

    @pl.when(surplus != 0.0)
    def _():
        upper = (lax.broadcasted_iota(jnp.int32, (PAGE_SIZE, PAGE_SIZE), 0)
                 < lax.broadcasted_iota(jnp.int32, (PAGE_SIZE, PAGE_SIZE), 1))
        upper = jnp.where(upper, 1.0, 0.0).astype(BF16)
        ones = jnp.ones((PAGE_SIZE, PAGE_SIZE), BF16)

        def tile(c, carry):
            kk = keys_ref[c]
            eq = jnp.where(kk == thr, 1.0, 0.0)
            eqb = eq.astype(BF16)
            before = _dot(eqb, upper) + carry
            keep_tie = jnp.where(before < need, eq, 0.0)
            return jnp.where(kk > thr, 1.0, keep_tie), carry + _dot(eqb, ones)

        def past_tile(c, carry):
            sel, carry = tile(c, carry)
            mask_ref[c] = sel
            return carry

        carry = lax.fori_loop(0, n_pages, past_tile, jnp.zeros((rows, PAGE_SIZE), F32))
        sel, _ = tile(n_pages, carry)
        maskn_ref[...] = jnp.where(new_ok, sel, 0.0)


def _s_sel(scores, scores_new, k_sel, n_new, rows_per_step):
    n_pages, rows, _ = scores.shape
    return pl.pallas_call(
        functools.partial(_s_sel_kernel, k_sel=k_sel, n_new=n_new),
        grid=(rows // rows_per_step,),
        in_specs=[pl.BlockSpec((n_pages, rows_per_step, PAGE_SIZE), lambda i: (0, i, 0)),
                  pl.BlockSpec((rows_per_step, PAGE_SIZE), lambda i: (i, 0))],
        out_specs=[pl.BlockSpec((n_pages, rows_per_step, PAGE_SIZE), lambda i: (0, i, 0)),
                   pl.BlockSpec((rows_per_step, PAGE_SIZE), lambda i: (i, 0))],
        out_shape=[jax.ShapeDtypeStruct((n_pages, rows, PAGE_SIZE), F32),
                   jax.ShapeDtypeStruct((rows, PAGE_SIZE), F32)],
        scratch_shapes=[pltpu.VMEM((n_pages + 1, rows_per_step, PAGE_SIZE), jnp.int32)],
        compiler_params=_params(1),
        name="s_sel",
    )(scores, scores_new)


def _s_att_kernel(pt_ref, q_ref, mask_ref, maskn_ref, kn_ref, vn_ref, *refs):
    k_refs = refs[:S_ATT_PAGES]
    v_refs = refs[S_ATT_PAGES:2 * S_ATT_PAGES]
    o_ref, m_ref, l_ref, acc_ref = refs[2 * S_ATT_PAGES:]
    i = pl.program_id(1)
    flat = N_KV_HEADS * PAGE_SIZE
    c2 = np.float32(HEAD_DIM ** -0.5 * np.log2(np.e))
    q = q_ref[0]
    spread = (lax.broadcasted_iota(jnp.int32, (PAGE_SIZE, flat), 1) // N_KV_HEADS
              == lax.broadcasted_iota(jnp.int32, (PAGE_SIZE, flat), 0))
    spread = jnp.where(spread, 1.0, 0.0).astype(BF16)

    def update(kf, vf, sel, n_pg):
        sel2 = _dot(sel.astype(BF16), spread)
        sel2 = jnp.concatenate([sel2[p * T_PAD:(p + 1) * T_PAD, :] for p in range(n_pg)], axis=1)
        bias_t = jnp.where(sel2 > 0.5, 0.0, NEG)
        bias_g = jnp.concatenate([bias_t] * KV_GROUP, axis=0)
        parity = lax.broadcasted_iota(jnp.int32, (1, n_pg * flat), 1) % N_KV_HEADS
        bias = jnp.concatenate([bias_g + jnp.where(parity == j, 0.0, NEG) for j in range(N_KV_HEADS)], axis=0)
        s = _dot_nt(q, kf) + bias
        m_old = m_ref[...]
        m_new = jnp.maximum(m_old, jnp.max(s, axis=1, keepdims=True))
        alpha = jnp.exp2((m_old - m_new) * c2)
        p = jnp.exp2((s - m_new) * c2)
        l_ref[...] = alpha * l_ref[...] + jnp.sum(p, axis=1, keepdims=True)
        acc_ref[...] = alpha * acc_ref[...] + _dot(p.astype(BF16), vf)
        m_ref[...] = m_new

    @pl.when(i == 0)
    def _():
        m_ref[...] = jnp.full(m_ref.shape, NEG, F32)
        l_ref[...] = jnp.zeros(l_ref.shape, F32)
        acc_ref[...] = jnp.zeros(acc_ref.shape, F32)
        update(kn_ref[0], vn_ref[0], maskn_ref[...], 1)

    kf = jnp.concatenate([k_refs[p][...] for p in range(S_ATT_PAGES)], axis=0).astype(BF16)
    vf = jnp.concatenate([v_refs[p][...] for p in range(S_ATT_PAGES)], axis=0).astype(BF16)
    update(kf, vf, mask_ref[...].reshape(S_ATT_PAGES * T_PAD, PAGE_SIZE), S_ATT_PAGES)

    @pl.when(i == pl.num_programs(1) - 1)
    def _():
        o_ref[0] = acc_ref[...] / l_ref[...]


def _s_att(page_table, q_rows, mask, mask_new, k_new, v_new, cache_k, cache_v, layer):
    db, n_pages = page_table.shape
    steps = n_pages // S_ATT_PAGES
    rows = N_KV_HEADS * KV_GROUP * T_PAD
    flat = N_KV_HEADS * PAGE_SIZE
    page_spec = lambda p: pl.BlockSpec(
        (None, None, flat, HEAD_DIM), lambda b, i, pt: (layer, pt[b, i * S_ATT_PAGES + p], 0, 0))
    grid_spec = pltpu.PrefetchScalarGridSpec(
        num_scalar_prefetch=1,
        grid=(db, steps),
        in_specs=[pl.BlockSpec((1, rows, HEAD_DIM), lambda b, i, pt: (b, 0, 0)),
                  pl.BlockSpec((S_ATT_PAGES, T_PAD, PAGE_SIZE), lambda b, i, pt: (i, b, 0)),
                  pl.BlockSpec((T_PAD, PAGE_SIZE), lambda b, i, pt: (b, 0)),
                  pl.BlockSpec((1, flat, HEAD_DIM), lambda b, i, pt: (b, 0, 0)),
                  pl.BlockSpec((1, flat, HEAD_DIM), lambda b, i, pt: (b, 0, 0))]
                 + [page_spec(p) for p in range(S_ATT_PAGES)] * 2,
        out_specs=pl.BlockSpec((1, rows, HEAD_DIM), lambda b, i, pt: (b, 0, 0)),
        scratch_shapes=[pltpu.VMEM((rows, 1), F32),
                        pltpu.VMEM((rows, 1), F32),
                        pltpu.VMEM((rows, HEAD_DIM), F32)])
    return pl.pallas_call(
        _s_att_kernel,
        grid_spec=grid_spec,
        out_shape=jax.ShapeDtypeStruct((db, rows, HEAD_DIM), F32),
        compiler_params=_params(2),
        name="s_att",
    )(page_table, q_rows, mask, mask_new, k_new, v_new,
      *([cache_k] * S_ATT_PAGES), *([cache_v] * S_ATT_PAGES))


def _sample_mixer(page_table, q_b, qi_b, small_f, k_f, vv_f, cache_k_flat, cache_v_flat, cache_kidx_t, layer, dec_seq):
    db, n_pages = page_table.shape
    t = dec_seq
    past = n_pages * PAGE_SIZE
    k_sel = min(TOPK_MAX, (past + t) // 4)
    pad_t = lambda a: jnp.pad(a, [(0, 0)] * (a.ndim - 2) + [(0, T_PAD - t), (0, 0)])

    qi_rows = pad_t(qi_b.reshape(db, t, IDX_HEADS, IDX_DIM).transpose(0, 2, 1, 3))
    qi_rows = qi_rows.reshape(db, IDX_HEADS * T_PAD, IDX_DIM)
    wi = small_f[:, IDX_DIM:IDX_DIM + IDX_HEADS].reshape(db, t, IDX_HEADS).transpose(0, 2, 1)
    wb_rows = jnp.pad(wi, ((0, 0), (0, 0), (0, T_PAD - t))).reshape(db, IDX_HEADS * T_PAD, 1)
    wb_rows = jnp.broadcast_to(wb_rows, (db, IDX_HEADS * T_PAD, PAGE_SIZE))
    kit_new = small_f[:, 0:IDX_DIM].reshape(db, t, IDX_DIM).transpose(0, 2, 1)
    kit_new = jnp.pad(kit_new, ((0, 0), (0, 0), (0, PAGE_SIZE - t))).astype(BF16)

    scores, scores_new = _s_idx(page_table, qi_rows, wb_rows, kit_new, cache_kidx_t, layer)
    rows_per_step = min(db * T_PAD, 64)
    mask, mask_new = _s_sel(scores, scores_new, k_sel, t, rows_per_step)

    q_rows = pad_t(q_b.reshape(db, t, N_KV_HEADS, KV_GROUP, HEAD_DIM).transpose(0, 2, 3, 1, 4))
    q_rows = q_rows.reshape(db, N_KV_HEADS * KV_GROUP * T_PAD, HEAD_DIM)
    flat_new = lambda a: jnp.pad(a.reshape(db, t * N_KV_HEADS, HEAD_DIM),
                                 ((0, 0), (0, (PAGE_SIZE - t) * N_KV_HEADS), (0, 0))).astype(BF16)
    o = _s_att(page_table, q_rows, mask, mask_new, flat_new(k_f), flat_new(vv_f), cache_k_flat, cache_v_flat, layer)
    o = o.reshape(db, N_KV_HEADS, KV_GROUP, T_PAD, HEAD_DIM)[:, :, :, :t]
    return o.transpose(0, 3, 1, 2, 4).reshape(db * t, N_HEADS * HEAD_DIM).astype(BF16)


def _post_kernel(x_ref, u_ref, v_ref, attn_ref, sga_ref, sgb_ref, ws_ref, bias_ref, wpa_ref, wpb_ref, wout_ref,
                 o_ref, a_ref, *, block):
    tm = x_ref.shape[0]
    r = lax.broadcasted_iota(jnp.int32, (CHUNK, CHUNK), 0)
    c = lax.broadcasted_iota(jnp.int32, (CHUNK, CHUNK), 1)
    allowed = (r >= c) & ((r // block) == (c // block))
    bias = bias_ref[...]
    for g in range(A_GROUPS):
        wg = jnp.where(allowed, ws_ref[g], 0.0).astype(BF16)
        for i in range(tm // CHUNK):
            rows = slice(i * CHUNK, (i + 1) * CHUNK)
            cols = slice(g * CHUNK, (g + 1) * CHUNK)
            sv = _dot(wg, v_ref[rows, cols]) + bias[:, cols]
            a_ref[rows, cols] = (u_ref[rows, cols].astype(F32) * sv).astype(BF16)
    merged = (sga_ref[...].astype(F32) * _dot(a_ref[...], wpa_ref[...])
              + sgb_ref[...].astype(F32) * _dot(attn_ref[...], wpb_ref[...]))
    o_ref[...] = x_ref[...] + _dot(merged.astype(BF16), wout_ref[...])


def _post(x, u, v, attn, sga, sgb, ws, bias, wpa, wpb, wout, tm, block):
    n = x.shape[0]
    row = lambda dt: pl.BlockSpec((tm, D_MODEL), lambda i: (i, 0))
    return pl.pallas_call(
        functools.partial(_post_kernel, block=block),
        grid=(n // tm,),
        in_specs=[row(F32), row(BF16), row(BF16), row(BF16), row(BF16), row(BF16),
                  _const_spec((A_GROUPS, CHUNK, CHUNK)), _const_spec((CHUNK, A_WIDTH)),
                  _const_spec((A_WIDTH, D_MODEL)), _const_spec((N_HEADS * HEAD_DIM, D_MODEL)),
                  _const_spec((D_MODEL, D_MODEL))],
        out_specs=row(F32),
        out_shape=jax.ShapeDtypeStruct((n, D_MODEL), F32),
        scratch_shapes=[pltpu.VMEM((tm, A_WIDTH), BF16)],
        compiler_params=_params(1),
        name="post",
    )(x, u, v, attn, sga, sgb, ws, bias, wpa, wpb, wout)


def kernel(x_prompt, x_sample, cache_k, cache_v, cache_kidx, page_table, g_ffn1, w_up1, w_down1, g_mix, w_in, g_v, g_q, g_k, w_s, b_s, w_pa, w_pb, w_out, g_ffn2, w_up2, w_down2):
    batch, seq, _ = x_prompt.shape
    db, dec_seq, _ = x_sample.shape
    depth = w_in.shape[0]
    n_p, n_s = batch * seq, db * dec_seq
    assert seq % ATT_CHUNK == 0 and n_s % CHUNK == 0 and CHUNK % dec_seq == 0 and dec_seq <= T_PAD
    tm_p = 512 if n_p % 512 == 0 else KEY_CHUNK
    tm_s = CHUNK

    xp = x_prompt.reshape(n_p, D_MODEL)
    xs = x_sample.reshape(n_s, D_MODEL)
    n_pool = cache_k.shape[1]
    cache_k_flat = cache_k.reshape(depth, n_pool, PAGE_SIZE * N_KV_HEADS, HEAD_DIM)
    cache_v_flat = cache_v.reshape(depth, n_pool, PAGE_SIZE * N_KV_HEADS, HEAD_DIM)
    cache_kidx_t = jnp.swapaxes(cache_kidx, 2, 3)
    outs_p = [[] for _ in range(3)]
    outs = [[] for _ in range(4)]
    for l in range(depth):
        row = lambda a: a[l].reshape(1, -1)
        wup1, wdn1 = w_up1[l].astype(BF16), w_down1[l].astype(BF16)
        wup2, wdn2 = w_up2[l].astype(BF16), w_down2[l].astype(BF16)
        wa = w_in[l, :, 0:WA_W].astype(BF16)
        n_small = IDX_DIM + IDX_HEADS
        ws = jnp.pad(w_in[l, :, WA_W:WA_W + n_small], ((0, 0), (0, SMALL_W - n_small))).astype(BF16)
        wg = w_in[l, :, WA_W + n_small:].astype(BF16)
        wpa, wpb, wout = w_pa[l].astype(BF16), w_pb[l].astype(BF16), w_out[l].astype(BF16)
        ws_p = w_s[l]
        bias_p = jnp.repeat(b_s[l].T, CHUNK, axis=1)
        ws_s = jnp.tile(w_s[l, :, 0:dec_seq, 0:dec_seq], (1, CHUNK // dec_seq, CHUNK // dec_seq))
        bias_s = jnp.repeat(jnp.tile(b_s[l, :, 0:dec_seq].T, (CHUNK // dec_seq, 1)), CHUNK, axis=1)

        x1 = _ffn(xp, row(g_ffn1), wup1, wdn1, tm_p)
        (u, v, k_p, v_p, kit_p, sga, sgb, qt, qit, wit, vt, kb, smallb) = _inproj(
            x1, row(g_mix), wa, ws, wg, row(g_v), row(g_q), row(g_k), tm_p, True, seq=seq)
        outs_p[0].append(k_p.reshape(batch, seq, N_KV_HEADS, HEAD_DIM))
        outs_p[1].append(v_p.reshape(batch, seq, N_KV_HEADS, HEAD_DIM))
        outs_p[2].append(jnp.swapaxes(kit_p, 1, 2))
        attn = _mix(qt, qit, wit, kb, vt, smallb, batch, seq)
        x2 = _post(x1, u, v, attn, sga, sgb, ws_p, bias_p, wpa, wpb, wout, tm_p, CHUNK)
        xp = _ffn(x2, row(g_ffn2), wup2, wdn2, tm_p)

        x1 = _ffn(xs, row(g_ffn1), wup1, wdn1, tm_s)
        (u, v, k_f, vv_f, small_f, sga, sgb, q_b, qi_b) = _inproj(
            x1, row(g_mix), wa, ws, wg, row(g_v), row(g_q), row(g_k), tm_s, False)
        attn = _sample_mixer(page_table, q_b, qi_b, small_f, k_f, vv_f,
                             cache_k_flat, cache_v_flat, cache_kidx_t, l, dec_seq)
        x2 = _post(x1, u.astype(BF16), v.astype(BF16), attn, sga, sgb, ws_s, bias_s, wpa, wpb, wout, tm_s, dec_seq)
        xs = _ffn(x2, row(g_ffn2), wup2, wdn2, tm_s)
        outs[0].append(k_f.reshape(db, dec_seq, N_KV_HEADS, HEAD_DIM))
        outs[1].append(vv_f.reshape(db, dec_seq, N_KV_HEADS, HEAD_DIM))
        outs[2].append(small_f[:, 0:IDX_DIM].reshape(db, dec_seq, IDX_DIM))
        outs[3].append(v.reshape(db, dec_seq, A_WIDTH))

    return (xp.reshape(batch, seq, D_MODEL), xs.reshape(db, dec_seq, D_MODEL),
            *[jnp.stack(o) for o in outs_p], *[jnp.stack(o) for o in outs])
```

```python
import functools

import jax
import jax.numpy as jnp
import numpy as np
from jax import lax
from jax.experimental import pallas as pl
from jax.experimental.pallas import tpu as pltpu

D_MODEL = 1024
D_FF = 2816
CHUNK = 128
A_GROUPS = 8
A_WIDTH = D_MODEL
N_HEADS = 8
HEAD_DIM = 128
N_KV_HEADS = 2
KV_GROUP = N_HEADS // N_KV_HEADS
IDX_HEADS = 8
IDX_DIM = 64
TOPK_MAX = 256
Q_BLOCK = 512
PAGE_SIZE = 128
EPS = 1e-6

KV_W = N_KV_HEADS * HEAD_DIM
QI_W = IDX_HEADS * IDX_DIM
WA_W = 3 * D_MODEL + 2 * KV_W + QI_W
SMALL_W = 128
F_CHUNK = 256
KEY_CHUNK = 256
ATT_CHUNK = 512
ONES_ROWS = 16
NEG = -1e30
INT_MIN = -2 ** 31
KEY_NEG_INF = INT_MIN + 0x7FFFFF
VMEM_LIMIT = 56 * 1024 * 1024

BF16 = jnp.bfloat16
F32 = jnp.float32


def _dot(a, b):
    return jnp.dot(a, b, preferred_element_type=F32)


def _dot_nt(a, b):
    return lax.dot_general(a, b, (((1,), (1,)), ((), ())), preferred_element_type=F32)


def _rms(x, g):
    return x * lax.rsqrt(jnp.mean(x * x, axis=-1, keepdims=True) + EPS) * g


def _sigmoid(x):
    return 1.0 / (1.0 + jnp.exp(-x))


def _gelu(x):
    c = np.float32(np.sqrt(2.0 / np.pi))
    return x * (0.5 * (1.0 + jnp.tanh(c * (x + 0.044715 * (x * x * x)))))


def _const_spec(shape):
    nd = len(shape)
    return pl.BlockSpec(shape, lambda *_: (0,) * nd, pipeline_mode=pl.Buffered(1))


def _params(n_axes):
    return pltpu.CompilerParams(dimension_semantics=("arbitrary",) * n_axes, vmem_limit_bytes=VMEM_LIMIT)


def _ffn_apply(x, g_ref, wup_ref, wdn_ref, h_ref):
    xb = _rms(x, g_ref[...]).astype(BF16)
    for j in range(D_FF // F_CHUNK):
        gate = _dot(xb, wup_ref[:, j * F_CHUNK:(j + 1) * F_CHUNK])
        up = _dot(xb, wup_ref[:, D_FF + j * F_CHUNK:D_FF + (j + 1) * F_CHUNK])
        h_ref[:, j * F_CHUNK:(j + 1) * F_CHUNK] = (gate * _sigmoid(gate) * up).astype(BF16)
    return x + 0.5 * _dot(h_ref[...], wdn_ref[...])


def _ffn_kernel(x_ref, g_ref, wup_ref, wdn_ref, o_ref, h_ref):
    o_ref[...] = _ffn_apply(x_ref[...], g_ref, wup_ref, wdn_ref, h_ref)


def _ffn(x, g, wup, wdn, tm):
    n = x.shape[0]
    return pl.pallas_call(
        _ffn_kernel,
        grid=(n // tm,),
        in_specs=[pl.BlockSpec((tm, D_MODEL), lambda i: (i, 0)),
                  _const_spec((1, D_MODEL)),
                  _const_spec((D_MODEL, 2 * D_FF)),
                  _const_spec((D_FF, D_MODEL))],
        out_specs=pl.BlockSpec((tm, D_MODEL), lambda i: (i, 0)),
        out_shape=jax.ShapeDtypeStruct((n, D_MODEL), F32),
        scratch_shapes=[pltpu.VMEM((tm, D_FF), BF16)],
        compiler_params=_params(1),
        name="ffn",
    )(x, g, wup, wdn)


def _inproj_kernel(x_ref, g_ref, wa_ref, ws_ref, wg_ref, gv_ref, gq_ref, gk_ref, *out_refs, prompt):
    if prompt:
        (u_ref, v_ref, k_ref, vv_ref, kit_ref, sga_ref, sgb_ref,
         qt_ref, qit_ref, wit_ref, vt_ref, kb_ref, smallb_ref) = out_refs
    else:
        (u_ref, v_ref, k_ref, vv_ref, small_ref, sga_ref, sgb_ref, q_ref, qi_ref) = out_refs
    tm = x_ref.shape[0]
    hb = _rms(x_ref[...], g_ref[...]).astype(BF16)

    u_ref[...] = _gelu(_dot(hb, wa_ref[:, 0:D_MODEL])).astype(u_ref.dtype)
    v = _gelu(_dot(hb, wa_ref[:, D_MODEL:2 * D_MODEL]))
    v_ref[...] = _rms(v, gv_ref[...]).astype(v_ref.dtype)

    q = _dot(hb, wa_ref[:, 2 * D_MODEL:3 * D_MODEL])
    gq = gq_ref[...]
    for h in range(N_HEADS):
        qh = _rms(q[:, h * HEAD_DIM:(h + 1) * HEAD_DIM], gq)
        if prompt:
            qt_ref[h * HEAD_DIM:(h + 1) * HEAD_DIM, :] = qh.T.astype(BF16)
        else:
            q_ref[:, h * HEAD_DIM:(h + 1) * HEAD_DIM] = qh.astype(BF16)

    o = 3 * D_MODEL
    kk = _dot(hb, wa_ref[:, o:o + KV_W])
    gk = gk_ref[...]
    vv = _dot(hb, wa_ref[:, o + KV_W:o + 2 * KV_W])
    for j in range(N_KV_HEADS):
        kj = _rms(kk[:, j * HEAD_DIM:(j + 1) * HEAD_DIM], gk)
        if prompt:
            k_ref[pl.ds(j, tm, stride=N_KV_HEADS), :] = kj
            vv_ref[pl.ds(j, tm, stride=N_KV_HEADS), :] = vv[:, j * HEAD_DIM:(j + 1) * HEAD_DIM]
            kb_ref[:, j * HEAD_DIM:(j + 1) * HEAD_DIM] = kj.astype(BF16)
        else:
            k_ref[:, j * HEAD_DIM:(j + 1) * HEAD_DIM] = kj
    if prompt:
        for r in range(tm // KEY_CHUNK):
            vt_ref[r] = vv[r * KEY_CHUNK:(r + 1) * KEY_CHUNK, :].T.astype(BF16)
    else:
        vv_ref[...] = vv

    qi = _dot(hb, wa_ref[:, o + 2 * KV_W:o + 2 * KV_W + QI_W])
    if prompt:
        qit_ref[...] = qi.T.astype(BF16)
    else:
        qi_ref[...] = qi.astype(BF16)

    small = _dot(hb, ws_ref[...])
    lane = lax.broadcasted_iota(jnp.int32, small.shape, 1)
    is_w = (lane >= IDX_DIM) & (lane < IDX_DIM + IDX_HEADS)
    small = small * jnp.where(is_w, np.float32(IDX_HEADS ** -0.5), np.float32(1.0))
    if prompt:
        smallb_ref[...] = small.astype(BF16)
        small_t = small.T
        kit_ref[0] = small_t[0:IDX_DIM, :]
        wit_ref[...] = small_t[IDX_DIM:IDX_DIM + IDX_HEADS, :]
    else:
        small_ref[...] = small

    sga_ref[...] = _sigmoid(_dot(hb, wg_ref[:, 0:D_MODEL])).astype(BF16)
    sgb_ref[...] = _sigmoid(_dot(hb, wg_ref[:, D_MODEL:2 * D_MODEL])).astype(BF16)


def _inproj(x, g, wa, ws, wg, gv, gq, gk, tm, prompt, seq=None):
    n = x.shape[0]
    steps = n // tm
    row = lambda w: pl.BlockSpec((tm, w), lambda i: (i, 0))
    col = lambda h: pl.BlockSpec((h, tm), lambda i: (0, i))
    if prompt:
        per_seq = seq // tm
        kv_shape = jax.ShapeDtypeStruct((n * N_KV_HEADS, HEAD_DIM), F32)
        kv_spec = pl.BlockSpec((tm * N_KV_HEADS, HEAD_DIM), lambda i: (i, 0))
        kit_shape = jax.ShapeDtypeStruct((n // seq, IDX_DIM, seq), F32)
        kit_spec = pl.BlockSpec((1, IDX_DIM, tm), lambda i: (i // per_seq, 0, i % per_seq))
        third = [(kv_shape, kv_spec), (kv_shape, kv_spec), (kit_shape, kit_spec)]
    else:
        third = [(jax.ShapeDtypeStruct((n, KV_W), F32), row(KV_W)),
                 (jax.ShapeDtypeStruct((n, KV_W), F32), row(KV_W)),
                 (jax.ShapeDtypeStruct((n, SMALL_W), F32), row(SMALL_W))]
    out_shape = [jax.ShapeDtypeStruct((n, D_MODEL), BF16),
                 jax.ShapeDtypeStruct((n, A_WIDTH), BF16 if prompt else F32),
                 *[s for s, _ in third],
                 jax.ShapeDtypeStruct((n, D_MODEL), BF16),
                 jax.ShapeDtypeStruct((n, D_MODEL), BF16)]
    out_specs = [row(D_MODEL), row(A_WIDTH), *[s for _, s in third], row(D_MODEL), row(D_MODEL)]
    if prompt:
        out_shape += [jax.ShapeDtypeStruct((D_MODEL, n), BF16),
                      jax.ShapeDtypeStruct((QI_W, n), BF16),
                      jax.ShapeDtypeStruct((IDX_HEADS, n), F32),
                      jax.ShapeDtypeStruct((n // KEY_CHUNK, KV_W, KEY_CHUNK), BF16),
                      jax.ShapeDtypeStruct((n, KV_W), BF16),
                      jax.ShapeDtypeStruct((n, SMALL_W), BF16)]
        out_specs += [col(D_MODEL), col(QI_W), col(IDX_HEADS),
                      pl.BlockSpec((tm // KEY_CHUNK, KV_W, KEY_CHUNK), lambda i: (i, 0, 0)),
                      row(KV_W), row(SMALL_W)]
    else:
        out_shape += [jax.ShapeDtypeStruct((n, D_MODEL), BF16),
                      jax.ShapeDtypeStruct((n, QI_W), BF16)]
        out_specs += [row(D_MODEL), row(QI_W)]
    return pl.pallas_call(
        functools.partial(_inproj_kernel, prompt=prompt),
        grid=(steps,),
        in_specs=[row(D_MODEL), _const_spec((1, D_MODEL)),
                  _const_spec((D_MODEL, WA_W)), _const_spec((D_MODEL, SMALL_W)), _const_spec((D_MODEL, 2 * D_MODEL)),
                  _const_spec((1, A_WIDTH)), _const_spec((1, HEAD_DIM)), _const_spec((1, HEAD_DIM))],
        out_specs=out_specs,
        out_shape=out_shape,
        compiler_params=_params(1),
        name="inproj_prompt" if prompt else "inproj_sample",
    )(x, g, wa, ws, wg, gv, gq, gk)


def _sort_key(x):
    bits = pltpu.bitcast(x, jnp.int32)
    bits = jnp.where(bits == INT_MIN, 0, bits)
    return jnp.where(bits < 0, bits ^ 0x7FFFFFFF, bits)


def _kth_threshold(count_ge, k, shape, total):
    def body(i, carry):
        t_u, cnt_t = carry
        cand_u = t_u | jnp.left_shift(jnp.int32(1), 31 - i)
        cnt = count_ge(cand_u ^ INT_MIN)
        ok = cnt >= k
        return jnp.where(ok, cand_u, t_u), jnp.where(ok, cnt, cnt_t)
    t_u, cnt_t = lax.fori_loop(0, 32, body, (jnp.zeros(shape, jnp.int32), jnp.full(shape, total, F32)))
    return t_u ^ INT_MIN, cnt_t


def _count_causal(keys_ref, rows, pred, t):
    lanes = keys_ref.shape[1]
    tiles = lanes // 128
    n_acc = 4
    cols = [slice(j * 128, (j + 1) * 128) for j in range(tiles)]
    ts = [t[:, c] for c in cols]
    accs = [[jnp.zeros((8, 128), F32) for _ in range(n_acc)] for _ in range(tiles)]
    for i in range(rows // 8):
        for j in range(tiles):
            if i * 8 < rows - lanes + (j + 1) * 128:
                hit = jnp.where(pred(keys_ref[i * 8:(i + 1) * 8, cols[j]], ts[j]), 1.0, 0.0)
                accs[j][i % n_acc] = accs[j][i % n_acc] + hit
    return jnp.concatenate([jnp.sum((a[0] + a[1]) + (a[2] + a[3]), axis=0, keepdims=True) for a in accs], axis=1)


def _mix_kernel(qt_ref, qit_ref, wit_ref, kb_ref, vt_ref, smallb_ref, o_ref,
                keys_ref, bias_ref, thr_ref, need_ref, flag_ref, m_ref, acc_ref, *, k_sel):
    seq = keys_ref.shape[0]
    n = pl.program_id(1)
    t0 = n * Q_BLOCK
    last = (n + 1) * Q_BLOCK - 1
    n_att = last // ATT_CHUNK + 1
    n_bias = n_att * (ATT_CHUNK // KEY_CHUNK)
    row = lax.broadcasted_iota(jnp.int32, (KEY_CHUNK, Q_BLOCK), 0)
    col = lax.broadcasted_iota(jnp.int32, (KEY_CHUNK, Q_BLOCK), 1)
    row_a = lax.broadcasted_iota(jnp.int32, (ATT_CHUNK, Q_BLOCK), 0)
    col_a = lax.broadcasted_iota(jnp.int32, (ATT_CHUNK, Q_BLOCK), 1)

    qcat = jnp.concatenate([qit_ref[h * IDX_DIM:(h + 1) * IDX_DIM, :] for h in range(IDX_HEADS)], axis=1)
    w = wit_ref[...] * np.float32(IDX_DIM ** -0.5)

    def score_chunk(c, carry):
        r0 = pl.multiple_of(c * ATT_CHUNK, ATT_CHUNK)
        ki = smallb_ref[pl.ds(r0, ATT_CHUNK), :][:, 0:IDX_DIM]
        dots = _dot(ki, qcat)
        sc = w[0:1, :] * jnp.maximum(dots[:, 0:Q_BLOCK], 0.0)
        for h in range(1, IDX_HEADS):
            sc = sc + w[h:h + 1, :] * jnp.maximum(dots[:, h * Q_BLOCK:(h + 1) * Q_BLOCK], 0.0)
        causal = (row_a + r0) <= (col_a + t0)
        keys_ref[pl.ds(r0, ATT_CHUNK), :] = _sort_key(jnp.where(causal, sc, -jnp.inf))
        return carry

    lax.fori_loop(0, n_att, score_chunk, 0)

    few = (n + 1) * Q_BLOCK <= k_sel

    @pl.when(few)
    def _():
        thr_ref[...] = jnp.full((1, Q_BLOCK), KEY_NEG_INF, jnp.int32)
        need_ref[...] = jnp.zeros((1, Q_BLOCK), F32)
        flag_ref[0] = 0.0

    ge = lambda kk, t: kk >= t
    gt = lambda kk, t: kk > t
    for nb in range(1, seq // Q_BLOCK + 1):
        @pl.when((n + 1 == nb) & jnp.logical_not(few))
        def _(nb=nb):
            rows = nb * Q_BLOCK
            thr, n_ge = _kth_threshold(lambda t: _count_causal(keys_ref, rows, ge, t), k_sel, (1, Q_BLOCK), rows)
            n_gt = _count_causal(keys_ref, rows, gt, thr)
            need = k_sel - n_gt
            tie = jnp.where((thr > KEY_NEG_INF) & (n_ge - n_gt > need), 1.0, 0.0)
            thr_ref[...] = thr
            need_ref[...] = need
            flag_ref[0] = jnp.max(tie)

    thr = thr_ref[...]
    need = need_ref[...]

    @pl.when(flag_ref[0] == 0.0)
    def _():
        thr_eff = jnp.maximum(thr, KEY_NEG_INF + 1)

        def body(c, carry):
            r0 = pl.multiple_of(c * KEY_CHUNK, KEY_CHUNK)
            bias_ref[pl.ds(r0, KEY_CHUNK), :] = jnp.where(keys_ref[pl.ds(r0, KEY_CHUNK), :] >= thr_eff, 0.0, NEG)
            return carry
        lax.fori_loop(0, n_bias, body, 0)

    @pl.when(flag_ref[0] != 0.0)
    def _():
        lower = (lax.broadcasted_iota(jnp.int32, (KEY_CHUNK, KEY_CHUNK), 1)
                 < lax.broadcasted_iota(jnp.int32, (KEY_CHUNK, KEY_CHUNK), 0))
        lower = jnp.where(lower, 1.0, 0.0).astype(BF16)

        def body(c, tie_carry):
            r0 = pl.multiple_of(c * KEY_CHUNK, KEY_CHUNK)
            kk = keys_ref[pl.ds(r0, KEY_CHUNK), :]
            eq = jnp.where(kk == thr, 1.0, 0.0)
            before = _dot(lower, eq.astype(BF16)) + tie_carry
            keep_tie = jnp.where(before < need, eq, 0.0)
            sel = jnp.where(kk > thr, 1.0, keep_tie)
            sel = jnp.where((row + r0) <= (col + t0), sel, 0.0)
            bias_ref[pl.ds(r0, KEY_CHUNK), :] = jnp.where(sel > 0.5, 0.0, NEG)
            return tie_carry + jnp.sum(eq, axis=0, keepdims=True)
        lax.fori_loop(0, n_bias, body, jnp.zeros((1, Q_BLOCK), F32))

    c2 = np.float32(HEAD_DIM ** -0.5 * np.log2(np.e))
    m_ref[...] = jnp.full(m_ref.shape, NEG, F32)
    acc_ref[...] = jnp.zeros(acc_ref.shape, F32)
    ones_rows = jnp.ones((ONES_ROWS, KEY_CHUNK), BF16)
    sub = ATT_CHUNK // KEY_CHUNK

    def attend(c, carry):
        r0 = pl.multiple_of(c * ATT_CHUNK, ATT_CHUNK)
        bias = bias_ref[pl.ds(r0, ATT_CHUNK), :]
        bias4 = jnp.concatenate([bias] * KV_GROUP, axis=1)
        for j in range(N_KV_HEADS):
            qj = jnp.concatenate(
                [qt_ref[(j * KV_GROUP + g) * HEAD_DIM:(j * KV_GROUP + g + 1) * HEAD_DIM, :]
                 for g in range(KV_GROUP)], axis=1)
            kj = kb_ref[pl.ds(r0, ATT_CHUNK), j * HEAD_DIM:(j + 1) * HEAD_DIM]
            s = _dot(kj, qj) + bias4
            m_old = m_ref[j]
            m_new = jnp.maximum(m_old, jnp.max(s, axis=0, keepdims=True))
            alpha = jnp.exp2((m_old - m_new) * c2)
            p = jnp.exp2((s - m_new) * c2).astype(BF16)
            acc = alpha * acc_ref[j]
            for r in range(sub):
                vj = jnp.concatenate([vt_ref[c * sub + r, j * HEAD_DIM:(j + 1) * HEAD_DIM, :], ones_rows], axis=0)
                acc = acc + _dot(vj, p[r * KEY_CHUNK:(r + 1) * KEY_CHUNK, :])
            acc_ref[j] = acc
            m_ref[j] = m_new
        return carry

    lax.fori_loop(0, n_att, attend, 0)

    for j in range(N_KV_HEADS):
        ot = acc_ref[j, 0:HEAD_DIM, :] / acc_ref[j, HEAD_DIM:HEAD_DIM + 1, :]
        for g in range(KV_GROUP):
            h = j * KV_GROUP + g
            o_ref[:, h * HEAD_DIM:(h + 1) * HEAD_DIM] = ot[:, g * Q_BLOCK:(g + 1) * Q_BLOCK].T.astype(BF16)


def _mix(qt, qit, wit, kb, vt, smallb, batch, seq):
    n = batch * seq
    nq = seq // Q_BLOCK
    k_sel = min(TOPK_MAX, seq // 4)
    gq = KV_GROUP * Q_BLOCK
    return pl.pallas_call(
        functools.partial(_mix_kernel, k_sel=k_sel),
        grid=(batch, nq),
        in_specs=[pl.BlockSpec((D_MODEL, Q_BLOCK), lambda b, i: (0, b * nq + i)),
                  pl.BlockSpec((QI_W, Q_BLOCK), lambda b, i: (0, b * nq + i)),
                  pl.BlockSpec((IDX_HEADS, Q_BLOCK), lambda b, i: (0, b * nq + i)),
                  pl.BlockSpec((seq, KV_W), lambda b, i: (b, 0)),
                  pl.BlockSpec((seq // KEY_CHUNK, KV_W, KEY_CHUNK), lambda b, i: (b, 0, 0)),
                  pl.BlockSpec((seq, SMALL_W), lambda b, i: (b, 0))],
        out_specs=pl.BlockSpec((Q_BLOCK, D_MODEL), lambda b, i: (b * nq + i, 0)),
        out_shape=jax.ShapeDtypeStruct((n, D_MODEL), BF16),
        scratch_shapes=[pltpu.VMEM((seq, Q_BLOCK), jnp.int32),
                        pltpu.VMEM((seq, Q_BLOCK), F32),
                        pltpu.VMEM((1, Q_BLOCK), jnp.int32),
                        pltpu.VMEM((1, Q_BLOCK), F32),
                        pltpu.SMEM((1,), F32),
                        pltpu.VMEM((N_KV_HEADS, 1, gq), F32),
                        pltpu.VMEM((N_KV_HEADS, HEAD_DIM + ONES_ROWS, gq), F32)],
        compiler_params=_params(2),
        name="mix_prompt",
    )(qt, qit, wit, kb, vt, smallb)


S_IDX_PAGES = 32
S_ATT_PAGES = 32
T_PAD = 8


def _page_copies(pt_ref, b, first, count, layer, cache_ref, buf_ref, slot, sem):
    return [pltpu.make_async_copy(cache_ref.at[layer, pt_ref[b, first + p]], buf_ref.at[slot, p], sem)
            for p in range(count)]


def _paged_loop(n_groups, copies, compute):
    def start(group):
        for i, c in enumerate(group):
            c.start(priority=i % 2)

    start(copies(0, 0))

    def body(g, carry):
        slot = g % 2

        @pl.when(g + 1 < n_groups)
        def _():
            start(copies(g + 1, 1 - slot))
        for c in copies(g, slot):
            c.wait()
        compute(g, slot)
        return carry

    lax.fori_loop(0, n_groups, body, 0)


def _s_idx_kernel(pt_ref, qi_ref, wb_ref, kin_ref, cache_ref, out_ref, outn_ref, buf_ref, sem_ref, *, layer):
    b = pl.program_id(0)
    n_pages = out_ref.shape[0]
    qi = qi_ref[0]
    wb = wb_ref[0]

    def score(kit, n_pg):
        r = jnp.maximum(_dot(qi, kit) * np.float32(IDX_DIM ** -0.5), 0.0) * jnp.concatenate([wb] * n_pg, axis=1)
        sc = r[0:T_PAD, :]
        for h in range(1, IDX_HEADS):
            sc = sc + r[h * T_PAD:(h + 1) * T_PAD, :]
        return sc

    outn_ref[...] = score(kin_ref[0], 1)

    def copies(g, slot):
        return _page_copies(pt_ref, b, g * S_IDX_PAGES, S_IDX_PAGES, layer, cache_ref, buf_ref, slot, sem_ref.at[slot])

    def compute(g, slot):
        kit = jnp.concatenate([buf_ref[slot, p] for p in range(S_IDX_PAGES)], axis=1).astype(BF16)
        sc = score(kit, S_IDX_PAGES)
        for p in range(S_IDX_PAGES):
            out_ref[g * S_IDX_PAGES + p] = sc[:, p * PAGE_SIZE:(p + 1) * PAGE_SIZE]

    _paged_loop(n_pages // S_IDX_PAGES, copies, compute)


def _s_idx(page_table, qi_rows, wb_rows, kit_new, cache_kidx_t, layer):
    db, n_pages = page_table.shape
    grid_spec = pltpu.PrefetchScalarGridSpec(
        num_scalar_prefetch=1,
        grid=(db,),
        in_specs=[pl.BlockSpec((1, IDX_HEADS * T_PAD, IDX_DIM), lambda b, pt: (b, 0, 0)),
                  pl.BlockSpec((1, IDX_HEADS * T_PAD, PAGE_SIZE), lambda b, pt: (b, 0, 0)),
                  pl.BlockSpec((1, IDX_DIM, PAGE_SIZE), lambda b, pt: (b, 0, 0)),
                  pl.BlockSpec(memory_space=pl.ANY)],
        out_specs=[pl.BlockSpec((n_pages, T_PAD, PAGE_SIZE), lambda b, pt: (0, b, 0)),
                   pl.BlockSpec((T_PAD, PAGE_SIZE), lambda b, pt: (b, 0))],
        scratch_shapes=[pltpu.VMEM((2, S_IDX_PAGES, IDX_DIM, PAGE_SIZE), F32),
                        pltpu.SemaphoreType.DMA((2,))])
    return pl.pallas_call(
        functools.partial(_s_idx_kernel, layer=layer),
        grid_spec=grid_spec,
        out_shape=[jax.ShapeDtypeStruct((n_pages, db * T_PAD, PAGE_SIZE), F32),
                   jax.ShapeDtypeStruct((db * T_PAD, PAGE_SIZE), F32)],
        compiler_params=_params(1),
        name="s_idx",
    )(page_table, qi_rows, wb_rows, kit_new, cache_kidx_t)


def _s_sel_kernel(sc_ref, scn_ref, mask_ref, maskn_ref, keys_ref, *, k_sel, n_new):
    n_pages, rows, _ = sc_ref.shape
    keys_ref[0:n_pages] = _sort_key(sc_ref[...])
    t = lax.broadcasted_iota(jnp.int32, (rows, PAGE_SIZE), 0) % T_PAD
    j = lax.broadcasted_iota(jnp.int32, (rows, PAGE_SIZE), 1)
    new_ok = (j <= t) & (j < n_new)
    keys_ref[n_pages] = _sort_key(jnp.where(new_ok, scn_ref[...], -jnp.inf))

    def count(pred):
        acc = jnp.zeros((rows, PAGE_SIZE), F32)
        for c in range(n_pages + 1):
            acc = acc + jnp.where(pred(keys_ref[c]), 1.0, 0.0)
        return jnp.sum(acc, axis=1, keepdims=True)

    thr, n_ge = _kth_threshold(lambda c: count(lambda kk: kk >= c), k_sel, (rows, 1), (n_pages + 1) * PAGE_SIZE)
    n_gt = count(lambda kk: kk > thr)
    need = k_sel - n_gt
    surplus = jnp.max(jnp.where((n_ge - n_gt > need) & (t[:, 0:1] < n_new), 1.0, 0.0))

    @pl.when(surplus == 0.0)
    def _():
        mask_ref[...] = jnp.where(keys_ref[0:n_pages] >= thr, 1.0, 0.0)
        maskn_ref[...] = jnp.where(new_ok & (keys_ref[n_pages] >= thr), 1.0, 0.0)

    @pl.when(surplus != 0.0)
    def _():
        upper = (lax.broadcasted_iota(jnp.int32, (PAGE_SIZE, PAGE_SIZE), 0)
                 < lax.broadcasted_iota(jnp.int32, (PAGE_SIZE, PAGE_SIZE), 1))
        upper = jnp.where(upper, 1.0, 0.0).astype(BF16)
        ones = jnp.ones((PAGE_SIZE, PAGE_SIZE), BF16)

        def tile(c, carry):
            kk = keys_ref[c]
            eq = jnp.where(kk == thr, 1.0, 0.0)
            eqb = eq.astype(BF16)
            before = _dot(eqb, upper) + carry
            keep_tie = jnp.where(before < need, eq, 0.0)
            return jnp.where(kk > thr, 1.0, keep_tie), carry + _dot(eqb, ones)

        def past_tile(c, carry):
            sel, carry = tile(c, carry)
            mask_ref[c] = sel
            return carry

        carry = lax.fori_loop(0, n_pages, past_tile, jnp.zeros((rows, PAGE_SIZE), F32))
        sel, _ = tile(n_pages, carry)
        maskn_ref[...] = jnp.where(new_ok, sel, 0.0)


def _s_sel(scores, scores_new, k_sel, n_new, rows_per_step):
    n_pages, rows, _ = scores.shape
    return pl.pallas_call(
        functools.partial(_s_sel_kernel, k_sel=k_sel, n_new=n_new),
        grid=(rows // rows_per_step,),
        in_specs=[pl.BlockSpec((n_pages, rows_per_step, PAGE_SIZE), lambda i: (0, i, 0)),
                  pl.BlockSpec((rows_per_step, PAGE_SIZE), lambda i: (i, 0))],
        out_specs=[pl.BlockSpec((n_pages, rows_per_step, PAGE_SIZE), lambda i: (0, i, 0)),
                   pl.BlockSpec((rows_per_step, PAGE_SIZE), lambda i: (i, 0))],
        out_shape=[jax.ShapeDtypeStruct((n_pages, rows, PAGE_SIZE), F32),
                   jax.ShapeDtypeStruct((rows, PAGE_SIZE), F32)],
        scratch_shapes=[pltpu.VMEM((n_pages + 1, rows_per_step, PAGE_SIZE), jnp.int32)],
        compiler_params=_params(1),
        name="s_sel",
    )(scores, scores_new)


def _s_att_kernel(pt_ref, q_ref, mask_ref, maskn_ref, kn_ref, vn_ref, *refs):
    k_refs = refs[:S_ATT_PAGES]
    v_refs = refs[S_ATT_PAGES:2 * S_ATT_PAGES]
    o_ref, m_ref, l_ref, acc_ref = refs[2 * S_ATT_PAGES:]
    i = pl.program_id(1)
    flat = N_KV_HEADS * PAGE_SIZE
    c2 = np.float32(HEAD_DIM ** -0.5 * np.log2(np.e))
    q = q_ref[0]
    spread = (lax.broadcasted_iota(jnp.int32, (PAGE_SIZE, flat), 1) // N_KV_HEADS
              == lax.broadcasted_iota(jnp.int32, (PAGE_SIZE, flat), 0))
    spread = jnp.where(spread, 1.0, 0.0).astype(BF16)

    def update(kf, vf, sel, n_pg):
        sel2 = _dot(sel.astype(BF16), spread)
        sel2 = jnp.concatenate([sel2[p * T_PAD:(p + 1) * T_PAD, :] for p in range(n_pg)], axis=1)
        bias_t = jnp.where(sel2 > 0.5, 0.0, NEG)
        bias_g = jnp.concatenate([bias_t] * KV_GROUP, axis=0)
        parity = lax.broadcasted_iota(jnp.int32, (1, n_pg * flat), 1) % N_KV_HEADS
        bias = jnp.concatenate([bias_g + jnp.where(parity == j, 0.0, NEG) for j in range(N_KV_HEADS)], axis=0)
        s = _dot_nt(q, kf) + bias
        m_old = m_ref[...]
        m_new = jnp.maximum(m_old, jnp.max(s, axis=1, keepdims=True))
        alpha = jnp.exp2((m_old - m_new) * c2)
        p = jnp.exp2((s - m_new) * c2)
        l_ref[...] = alpha * l_ref[...] + jnp.sum(p, axis=1, keepdims=True)
        acc_ref[...] = alpha * acc_ref[...] + _dot(p.astype(BF16), vf)
        m_ref[...] = m_new

    @pl.when(i == 0)
    def _():
        m_ref[...] = jnp.full(m_ref.shape, NEG, F32)
        l_ref[...] = jnp.zeros(l_ref.shape, F32)
        acc_ref[...] = jnp.zeros(acc_ref.shape, F32)
        update(kn_ref[0], vn_ref[0], maskn_ref[...], 1)

    kf = jnp.concatenate([k_refs[p][...] for p in range(S_ATT_PAGES)], axis=0).astype(BF16)
    vf = jnp.concatenate([v_refs[p][...] for p in range(S_ATT_PAGES)], axis=0).astype(BF16)
    update(kf, vf, mask_ref[...].reshape(S_ATT_PAGES * T_PAD, PAGE_SIZE), S_ATT_PAGES)

    @pl.when(i == pl.num_programs(1) - 1)
    def _():
        o_ref[0] = acc_ref[...] / l_ref[...]


def _s_att(page_table, q_rows, mask, mask_new, k_new, v_new, cache_k, cache_v, layer):
    db, n_pages = page_table.shape
    steps = n_pages // S_ATT_PAGES
    rows = N_KV_HEADS * KV_GROUP * T_PAD
    flat = N_KV_HEADS * PAGE_SIZE
    page_spec = lambda p: pl.BlockSpec(
        (None, None, flat, HEAD_DIM), lambda b, i, pt: (layer, pt[b, i * S_ATT_PAGES + p], 0, 0))
    grid_spec = pltpu.PrefetchScalarGridSpec(
        num_scalar_prefetch=1,
        grid=(db, steps),
        in_specs=[pl.BlockSpec((1, rows, HEAD_DIM), lambda b, i, pt: (b, 0, 0)),
                  pl.BlockSpec((S_ATT_PAGES, T_PAD, PAGE_SIZE), lambda b, i, pt: (i, b, 0)),
                  pl.BlockSpec((T_PAD, PAGE_SIZE), lambda b, i, pt: (b, 0)),
                  pl.BlockSpec((1, flat, HEAD_DIM), lambda b, i, pt: (b, 0, 0)),
                  pl.BlockSpec((1, flat, HEAD_DIM), lambda b, i, pt: (b, 0, 0))]
                 + [page_spec(p) for p in range(S_ATT_PAGES)] * 2,
        out_specs=pl.BlockSpec((1, rows, HEAD_DIM), lambda b, i, pt: (b, 0, 0)),
        scratch_shapes=[pltpu.VMEM((rows, 1), F32),
                        pltpu.VMEM((rows, 1), F32),
                        pltpu.VMEM((rows, HEAD_DIM), F32)])
    return pl.pallas_call(
        _s_att_kernel,
        grid_spec=grid_spec,
        out_shape=jax.ShapeDtypeStruct((db, rows, HEAD_DIM), F32),
        compiler_params=_params(2),
        name="s_att",
    )(page_table, q_rows, mask, mask_new, k_new, v_new,
      *([cache_k] * S_ATT_PAGES), *([cache_v] * S_ATT_PAGES))


def _sample_mixer(page_table, q_b, qi_b, small_f, k_f, vv_f, cache_k_flat, cache_v_flat, cache_kidx_t, layer, dec_seq):
    db, n_pages = page_table.shape
    t = dec_seq
    past = n_pages * PAGE_SIZE
    k_sel = min(TOPK_MAX, (past + t) // 4)
    pad_t = lambda a: jnp.pad(a, [(0, 0)] * (a.ndim - 2) + [(0, T_PAD - t), (0, 0)])

    qi_rows = pad_t(qi_b.reshape(db, t, IDX_HEADS, IDX_DIM).transpose(0, 2, 1, 3))
    qi_rows = qi_rows.reshape(db, IDX_HEADS * T_PAD, IDX_DIM)
    wi = small_f[:, IDX_DIM:IDX_DIM + IDX_HEADS].reshape(db, t, IDX_HEADS).transpose(0, 2, 1)
    wb_rows = jnp.pad(wi, ((0, 0), (0, 0), (0, T_PAD - t))).reshape(db, IDX_HEADS * T_PAD, 1)
    wb_rows = jnp.broadcast_to(wb_rows, (db, IDX_HEADS * T_PAD, PAGE_SIZE))
    kit_new = small_f[:, 0:IDX_DIM].reshape(db, t, IDX_DIM).transpose(0, 2, 1)
    kit_new = jnp.pad(kit_new, ((0, 0), (0, 0), (0, PAGE_SIZE - t))).astype(BF16)

    scores, scores_new = _s_idx(page_table, qi_rows, wb_rows, kit_new, cache_kidx_t, layer)
    rows_per_step = min(db * T_PAD, 64)
    mask, mask_new = _s_sel(scores, scores_new, k_sel, t, rows_per_step)

    q_rows = pad_t(q_b.reshape(db, t, N_KV_HEADS, KV_GROUP, HEAD_DIM).transpose(0, 2, 3, 1, 4))
    q_rows = q_rows.reshape(db, N_KV_HEADS * KV_GROUP * T_PAD, HEAD_DIM)
    flat_new = lambda a: jnp.pad(a.reshape(db, t * N_KV_HEADS, HEAD_DIM),
                                 ((0, 0), (0, (PAGE_SIZE - t) * N_KV_HEADS), (0, 0))).astype(BF16)
    o = _s_att(page_table, q_rows, mask, mask_new, flat_new(k_f), flat_new(vv_f), cache_k_flat, cache_v_flat, layer)
    o = o.reshape(db, N_KV_HEADS, KV_GROUP, T_PAD, HEAD_DIM)[:, :, :, :t]
    return o.transpose(0, 3, 1, 2, 4).reshape(db * t, N_HEADS * HEAD_DIM).astype(BF16)


def _post_kernel(x_ref, u_ref, v_ref, attn_ref, sga_ref, sgb_ref, ws_ref, bias_ref, wpa_ref, wpb_ref, wout_ref,
                 o_ref, a_ref, *, block):
    tm = x_ref.shape[0]
    r = lax.broadcasted_iota(jnp.int32, (CHUNK, CHUNK), 0)
    c = lax.broadcasted_iota(jnp.int32, (CHUNK, CHUNK), 1)
    allowed = (r >= c) & ((r // block) == (c // block))
    bias = bias_ref[...]
    for g in range(A_GROUPS):
        wg = jnp.where(allowed, ws_ref[g], 0.0).astype(BF16)
        for i in range(tm // CHUNK):
            rows = slice(i * CHUNK, (i + 1) * CHUNK)
            cols = slice(g * CHUNK, (g + 1) * CHUNK)
            sv = _dot(wg, v_ref[rows, cols]) + bias[:, cols]
            a_ref[rows, cols] = (u_ref[rows, cols].astype(F32) * sv).astype(BF16)
    merged = (sga_ref[...].astype(F32) * _dot(a_ref[...], wpa_ref[...])
              + sgb_ref[...].astype(F32) * _dot(attn_ref[...], wpb_ref[...]))
    o_ref[...] = x_ref[...] + _dot(merged.astype(BF16), wout_ref[...])


def _post(x, u, v, attn, sga, sgb, ws, bias, wpa, wpb, wout, tm, block):
    n = x.shape[0]
    row = lambda dt: pl.BlockSpec((tm, D_MODEL), lambda i: (i, 0))
    return pl.pallas_call(
        functools.partial(_post_kernel, block=block),
        grid=(n // tm,),
        in_specs=[row(F32), row(BF16), row(BF16), row(BF16), row(BF16), row(BF16),
                  _const_spec((A_GROUPS, CHUNK, CHUNK)), _const_spec((CHUNK, A_WIDTH)),
                  _const_spec((A_WIDTH, D_MODEL)), _const_spec((N_HEADS * HEAD_DIM, D_MODEL)),
                  _const_spec((D_MODEL, D_MODEL))],
        out_specs=row(F32),
        out_shape=jax.ShapeDtypeStruct((n, D_MODEL), F32),
        scratch_shapes=[pltpu.VMEM((tm, A_WIDTH), BF16)],
        compiler_params=_params(1),
        name="post",
    )(x, u, v, attn, sga, sgb, ws, bias, wpa, wpb, wout)


def kernel(x_prompt, x_sample, cache_k, cache_v, cache_kidx, page_table, g_ffn1, w_up1, w_down1, g_mix, w_in, g_v, g_q, g_k, w_s, b_s, w_pa, w_pb, w_out, g_ffn2, w_up2, w_down2):
    batch, seq, _ = x_prompt.shape
    db, dec_seq, _ = x_sample.shape
    depth = w_in.shape[0]
    n_p, n_s = batch * seq, db * dec_seq
    assert seq % ATT_CHUNK == 0 and seq % Q_BLOCK == 0 and n_s % CHUNK == 0 and CHUNK % dec_seq == 0 and dec_seq <= T_PAD
    tm_p = 512 if n_p % 512 == 0 else KEY_CHUNK
    tm_s = CHUNK

    xp = x_prompt.reshape(n_p, D_MODEL)
    xs = x_sample.reshape(n_s, D_MODEL)
    n_pool = cache_k.shape[1]
    cache_k_flat = cache_k.reshape(depth, n_pool, PAGE_SIZE * N_KV_HEADS, HEAD_DIM)
    cache_v_flat = cache_v.reshape(depth, n_pool, PAGE_SIZE * N_KV_HEADS, HEAD_DIM)
    cache_kidx_t = jnp.swapaxes(cache_kidx, 2, 3)
    outs_p = [[] for _ in range(3)]
    outs = [[] for _ in range(4)]
    for l in range(depth):
        row = lambda a: a[l].reshape(1, -1)
        wup1, wdn1 = w_up1[l].astype(BF16), w_down1[l].astype(BF16)
        wup2, wdn2 = w_up2[l].astype(BF16), w_down2[l].astype(BF16)
        wa = w_in[l, :, 0:WA_W].astype(BF16)
        n_small = IDX_DIM + IDX_HEADS
        ws = jnp.pad(w_in[l, :, WA_W:WA_W + n_small], ((0, 0), (0, SMALL_W - n_small))).astype(BF16)
        wg = w_in[l, :, WA_W + n_small:].astype(BF16)
        wpa, wpb, wout = w_pa[l].astype(BF16), w_pb[l].astype(BF16), w_out[l].astype(BF16)
        ws_p = w_s[l]
        bias_p = jnp.repeat(b_s[l].T, CHUNK, axis=1)
        ws_s = jnp.tile(w_s[l, :, 0:dec_seq, 0:dec_seq], (1, CHUNK // dec_seq, CHUNK // dec_seq))
        bias_s = jnp.repeat(jnp.tile(b_s[l, :, 0:dec_seq].T, (CHUNK // dec_seq, 1)), CHUNK, axis=1)

        x1 = _ffn(xp, row(g_ffn1), wup1, wdn1, tm_p)
        (u, v, k_p, v_p, kit_p, sga, sgb, qt, qit, wit, vt, kb, smallb) = _inproj(
            x1, row(g_mix), wa, ws, wg, row(g_v), row(g_q), row(g_k), tm_p, True, seq=seq)
        outs_p[0].append(k_p.reshape(batch, seq, N_KV_HEADS, HEAD_DIM))
        outs_p[1].append(v_p.reshape(batch, seq, N_KV_HEADS, HEAD_DIM))
        outs_p[2].append(jnp.swapaxes(kit_p, 1, 2))
        attn = _mix(qt, qit, wit, kb, vt, smallb, batch, seq)
        x2 = _post(x1, u, v, attn, sga, sgb, ws_p, bias_p, wpa, wpb, wout, tm_p, CHUNK)
        xp = _ffn(x2, row(g_ffn2), wup2, wdn2, tm_p)

        x1 = _ffn(xs, row(g_ffn1), wup1, wdn1, tm_s)
        (u, v, k_f, vv_f, small_f, sga, sgb, q_b, qi_b) = _inproj(
            x1, row(g_mix), wa, ws, wg, row(g_v), row(g_q), row(g_k), tm_s, False)
        attn = _sample_mixer(page_table, q_b, qi_b, small_f, k_f, vv_f,
                             cache_k_flat, cache_v_flat, cache_kidx_t, l, dec_seq)
        x2 = _post(x1, u.astype(BF16), v.astype(BF16), attn, sga, sgb, ws_s, bias_s, wpa, wpb, wout, tm_s, dec_seq)
        xs = _ffn(x2, row(g_ffn2), wup2, wdn2, tm_s)
        outs[0].append(k_f.reshape(db, dec_seq, N_KV_HEADS, HEAD_DIM))
        outs[1].append(vv_f.reshape(db, dec_seq, N_KV_HEADS, HEAD_DIM))
        outs[2].append(small_f[:, 0:IDX_DIM].reshape(db, dec_seq, IDX_DIM))
        outs[3].append(v.reshape(db, dec_seq, A_WIDTH))

    return (xp.reshape(batch, seq, D_MODEL), xs.reshape(db, dec_seq, D_MODEL),
            *[jnp.stack(o) for o in outs_p], *[jnp.stack(o) for o in outs])
```

```python
import functools

import jax
import jax.numpy as jnp
import numpy as np
from jax import lax
from jax.experimental import pallas as pl
from jax.experimental.pallas import tpu as pltpu

D_MODEL = 1024
D_FF = 2816
CHUNK = 128
A_GROUPS = 8
A_WIDTH = D_MODEL
N_HEADS = 8
HEAD_DIM = 128
N_KV_HEADS = 2
KV_GROUP = N_HEADS // N_KV_HEADS
IDX_HEADS = 8
IDX_DIM = 64
TOPK_MAX = 256
Q_BLOCK = 512
PAGE_SIZE = 128
EPS = 1e-6

KV_W = N_KV_HEADS * HEAD_DIM
QI_W = IDX_HEADS * IDX_DIM
WA_W = 3 * D_MODEL + 2 * KV_W + QI_W
SMALL_W = 128
F_CHUNK = 256
KEY_CHUNK = 256
ATT_CHUNK = 512
ONES_ROWS = 16
NEG = -1e30
INT_MIN = -2 ** 31
KEY_NEG_INF = INT_MIN + 0x7FFFFF
VMEM_LIMIT = 56 * 1024 * 1024

BF16 = jnp.bfloat16
F32 = jnp.float32


def _dot(a, b):
    return jnp.dot(a, b, preferred_element_type=F32)


def _dot_nt(a, b):
    return lax.dot_general(a, b, (((1,), (1,)), ((), ())), preferred_element_type=F32)


def _rms(x, g):
    return x * lax.rsqrt(jnp.mean(x * x, axis=-1, keepdims=True) + EPS) * g


def _sigmoid(x):
    return 1.0 / (1.0 + jnp.exp(-x))


def _gelu(x):
    c = np.float32(np.sqrt(2.0 / np.pi))
    return x * (0.5 * (1.0 + jnp.tanh(c * (x + 0.044715 * (x * x * x)))))


def _const_spec(shape):
    nd = len(shape)
    return pl.BlockSpec(shape, lambda *_: (0,) * nd, pipeline_mode=pl.Buffered(1))


def _layer_spec(shape, layer, index=None):
    index = (0,) * len(shape) if index is None else index
    return pl.BlockSpec((None,) + tuple(shape), lambda *_: (layer,) + tuple(index), pipeline_mode=pl.Buffered(1))


def _params(n_axes):
    return pltpu.CompilerParams(dimension_semantics=("arbitrary",) * n_axes, vmem_limit_bytes=VMEM_LIMIT)


def _ffn_apply(x, g_ref, wup_ref, wdn_ref, h_ref):
    xb = _rms(x, g_ref[...]).astype(BF16)
    for j in range(D_FF // F_CHUNK):
        gate = _dot(xb, wup_ref[:, j * F_CHUNK:(j + 1) * F_CHUNK])
        up = _dot(xb, wup_ref[:, D_FF + j * F_CHUNK:D_FF + (j + 1) * F_CHUNK])
        h_ref[:, j * F_CHUNK:(j + 1) * F_CHUNK] = (gate * _sigmoid(gate) * up).astype(BF16)
    return x + 0.5 * _dot(h_ref[...], wdn_ref[...])


def _ffn_kernel(x_ref, g_ref, wup_ref, wdn_ref, o_ref, h_ref):
    o_ref[...] = _ffn_apply(x_ref[...], g_ref, wup_ref, wdn_ref, h_ref)


def _ffn(x, g, wup, wdn, tm, layer):
    n = x.shape[0]
    return pl.pallas_call(
        _ffn_kernel,
        grid=(n // tm,),
        in_specs=[pl.BlockSpec((tm, D_MODEL), lambda i: (i, 0)),
                  _const_spec((1, D_MODEL)),
                  _layer_spec((D_MODEL, 2 * D_FF), layer),
                  _layer_spec((D_FF, D_MODEL), layer)],
        out_specs=pl.BlockSpec((tm, D_MODEL), lambda i: (i, 0)),
        out_shape=jax.ShapeDtypeStruct((n, D_MODEL), F32),
        scratch_shapes=[pltpu.VMEM((tm, D_FF), BF16)],
        compiler_params=_params(1),
        name="ffn",
    )(x, g, wup, wdn)


def _inproj_kernel(x_ref, g_ref, wa_ref, ws_ref, wg_ref, gv_ref, gq_ref, gk_ref, *out_refs, prompt):
    if prompt:
        (u_ref, v_ref, k_ref, vv_ref, kit_ref, sga_ref, sgb_ref,
         qt_ref, qit_ref, wit_ref, vt_ref, kb_ref, smallb_ref) = out_refs
    else:
        (u_ref, v_ref, k_ref, vv_ref, small_ref, sga_ref, sgb_ref, q_ref, qi_ref) = out_refs
    tm = x_ref.shape[0]
    hb = _rms(x_ref[...], g_ref[...]).astype(BF16)

    u_ref[...] = _gelu(_dot(hb, wa_ref[:, 0:D_MODEL])).astype(u_ref.dtype)
    v = _gelu(_dot(hb, wa_ref[:, D_MODEL:2 * D_MODEL]))
    v_ref[...] = _rms(v, gv_ref[...]).astype(v_ref.dtype)

    q = _dot(hb, wa_ref[:, 2 * D_MODEL:3 * D_MODEL])
    gq = gq_ref[...]
    for h in range(N_HEADS):
        qh = _rms(q[:, h * HEAD_DIM:(h + 1) * HEAD_DIM], gq)
        if prompt:
            qt_ref[h * HEAD_DIM:(h + 1) * HEAD_DIM, :] = qh.T.astype(BF16)
        else:
            q_ref[:, h * HEAD_DIM:(h + 1) * HEAD_DIM] = qh.astype(BF16)

    o = 3 * D_MODEL
    kk = _dot(hb, wa_ref[:, o:o + KV_W])
    gk = gk_ref[...]
    vv = _dot(hb, wa_ref[:, o + KV_W:o + 2 * KV_W])
    for j in range(N_KV_HEADS):
        kj = _rms(kk[:, j * HEAD_DIM:(j + 1) * HEAD_DIM], gk)
        if prompt:
            k_ref[pl.ds(j, tm, stride=N_KV_HEADS), :] = kj
            vv_ref[pl.ds(j, tm, stride=N_KV_HEADS), :] = vv[:, j * HEAD_DIM:(j + 1) * HEAD_DIM]
            kb_ref[:, j * HEAD_DIM:(j + 1) * HEAD_DIM] = kj.astype(BF16)
        else:
            k_ref[:, j * HEAD_DIM:(j + 1) * HEAD_DIM] = kj
    if prompt:
        for r in range(tm // KEY_CHUNK):
            vt_ref[r] = vv[r * KEY_CHUNK:(r + 1) * KEY_CHUNK, :].T.astype(BF16)
    else:
        vv_ref[...] = vv

    qi = _dot(hb, wa_ref[:, o + 2 * KV_W:o + 2 * KV_W + QI_W])
    if prompt:
        qit_ref[...] = qi.T.astype(BF16)
    else:
        qi_ref[...] = qi.astype(BF16)

    small = _dot(hb, ws_ref[...])
    lane = lax.broadcasted_iota(jnp.int32, small.shape, 1)
    is_w = (lane >= IDX_DIM) & (lane < IDX_DIM + IDX_HEADS)
    small = small * jnp.where(is_w, np.float32(IDX_HEADS ** -0.5), np.float32(1.0))
    if prompt:
        smallb_ref[...] = small.astype(BF16)
        small_t = small.T
        kit_ref[0] = small_t[0:IDX_DIM, :]
        wit_ref[...] = small_t[IDX_DIM:IDX_DIM + IDX_HEADS, :]
    else:
        small_ref[...] = small

    sga_ref[...] = _sigmoid(_dot(hb, wg_ref[:, 0:D_MODEL])).astype(BF16)
    sgb_ref[...] = _sigmoid(_dot(hb, wg_ref[:, D_MODEL:2 * D_MODEL])).astype(BF16)


def _inproj(x, g, w_packed, gv, gq, gk, tm, prompt, layer, seq=None):
    n = x.shape[0]
    steps = n // tm
    row = lambda w: pl.BlockSpec((tm, w), lambda i: (i, 0))
    col = lambda h: pl.BlockSpec((h, tm), lambda i: (0, i))
    if prompt:
        per_seq = seq // tm
        kv_shape = jax.ShapeDtypeStruct((n * N_KV_HEADS, HEAD_DIM), F32)
        kv_spec = pl.BlockSpec((tm * N_KV_HEADS, HEAD_DIM), lambda i: (i, 0))
        kit_shape = jax.ShapeDtypeStruct((n // seq, IDX_DIM, seq), F32)
        kit_spec = pl.BlockSpec((1, IDX_DIM, tm), lambda i: (i // per_seq, 0, i % per_seq))
        third = [(kv_shape, kv_spec), (kv_shape, kv_spec), (kit_shape, kit_spec)]
    else:
        third = [(jax.ShapeDtypeStruct((n, KV_W), F32), row(KV_W)),
                 (jax.ShapeDtypeStruct((n, KV_W), F32), row(KV_W)),
                 (jax.ShapeDtypeStruct((n, SMALL_W), F32), row(SMALL_W))]
    out_shape = [jax.ShapeDtypeStruct((n, D_MODEL), BF16),
                 jax.ShapeDtypeStruct((n, A_WIDTH), BF16 if prompt else F32),
                 *[s for s, _ in third],
                 jax.ShapeDtypeStruct((n, D_MODEL), BF16),
                 jax.ShapeDtypeStruct((n, D_MODEL), BF16)]
    out_specs = [row(D_MODEL), row(A_WIDTH), *[s for _, s in third], row(D_MODEL), row(D_MODEL)]
    if prompt:
        out_shape += [jax.ShapeDtypeStruct((D_MODEL, n), BF16),
                      jax.ShapeDtypeStruct((QI_W, n), BF16),
                      jax.ShapeDtypeStruct((IDX_HEADS, n), F32),
                      jax.ShapeDtypeStruct((n // KEY_CHUNK, KV_W, KEY_CHUNK), BF16),
                      jax.ShapeDtypeStruct((n, KV_W), BF16),
                      jax.ShapeDtypeStruct((n, SMALL_W), BF16)]
        out_specs += [col(D_MODEL), col(QI_W), col(IDX_HEADS),
                      pl.BlockSpec((tm // KEY_CHUNK, KV_W, KEY_CHUNK), lambda i: (i, 0, 0)),
                      row(KV_W), row(SMALL_W)]
    else:
        out_shape += [jax.ShapeDtypeStruct((n, D_MODEL), BF16),
                      jax.ShapeDtypeStruct((n, QI_W), BF16)]
        out_specs += [row(D_MODEL), row(QI_W)]
    return pl.pallas_call(
        functools.partial(_inproj_kernel, prompt=prompt),
        grid=(steps,),
        in_specs=[row(D_MODEL), _const_spec((1, D_MODEL)),
                  _layer_spec((D_MODEL, WA_W), layer, (0, 0)),
                  _layer_spec((D_MODEL, SMALL_W), layer, (0, (WA_W + 2 * D_MODEL) // SMALL_W)),
                  _layer_spec((D_MODEL, 2 * D_MODEL), layer, (0, WA_W // (2 * D_MODEL))),
                  _const_spec((1, A_WIDTH)), _const_spec((1, HEAD_DIM)), _const_spec((1, HEAD_DIM))],
        out_specs=out_specs,
        out_shape=out_shape,
        compiler_params=_params(1),
        name="inproj_prompt" if prompt else "inproj_sample",
    )(x, g, w_packed, w_packed, w_packed, gv, gq, gk)


def _sort_key(x):
    bits = pltpu.bitcast(x, jnp.int32)
    bits = jnp.where(bits == INT_MIN, 0, bits)
    return jnp.where(bits < 0, bits ^ 0x7FFFFFFF, bits)


def _kth_threshold(count_ge, k, shape, total):
    def body(i, carry):
        t_u, cnt_t = carry
        cand_u = t_u | jnp.left_shift(jnp.int32(1), 31 - i)
        cnt = count_ge(cand_u ^ INT_MIN)
        ok = cnt >= k
        return jnp.where(ok, cand_u, t_u), jnp.where(ok, cnt, cnt_t)
    t_u, cnt_t = lax.fori_loop(0, 32, body, (jnp.zeros(shape, jnp.int32), jnp.full(shape, total, F32)))
    return t_u ^ INT_MIN, cnt_t


def _count_causal(keys_ref, rows, pred, t):
    lanes = keys_ref.shape[1]
    tiles = lanes // 128
    n_acc = 4
    cols = [slice(j * 128, (j + 1) * 128) for j in range(tiles)]
    ts = [t[:, c] for c in cols]
    accs = [[jnp.zeros((8, 128), F32) for _ in range(n_acc)] for _ in range(tiles)]
    for i in range(rows // 8):
        for j in range(tiles):
            if i * 8 < rows - lanes + (j + 1) * 128:
                hit = jnp.where(pred(keys_ref[i * 8:(i + 1) * 8, cols[j]], ts[j]), 1.0, 0.0)
                accs[j][i % n_acc] = accs[j][i % n_acc] + hit
    return jnp.concatenate([jnp.sum((a[0] + a[1]) + (a[2] + a[3]), axis=0, keepdims=True) for a in accs], axis=1)


def _mix_kernel(qt_ref, qit_ref, wit_ref, kb_ref, vt_ref, smallb_ref, o_ref,
                keys_ref, bias_ref, thr_ref, need_ref, flag_ref, m_ref, acc_ref, *, k_sel):
    seq = keys_ref.shape[0]
    n = pl.program_id(1)
    t0 = n * Q_BLOCK
    last = (n + 1) * Q_BLOCK - 1
    n_att = last // ATT_CHUNK + 1
    n_bias = n_att * (ATT_CHUNK // KEY_CHUNK)
    row = lax.broadcasted_iota(jnp.int32, (KEY_CHUNK, Q_BLOCK), 0)
    col = lax.broadcasted_iota(jnp.int32, (KEY_CHUNK, Q_BLOCK), 1)
    row_a = lax.broadcasted_iota(jnp.int32, (ATT_CHUNK, Q_BLOCK), 0)
    col_a = lax.broadcasted_iota(jnp.int32, (ATT_CHUNK, Q_BLOCK), 1)

    qcat = jnp.concatenate([qit_ref[h * IDX_DIM:(h + 1) * IDX_DIM, :] for h in range(IDX_HEADS)], axis=1)
    w = wit_ref[...] * np.float32(IDX_DIM ** -0.5)

    def score_chunk(c, carry):
        r0 = pl.multiple_of(c * ATT_CHUNK, ATT_CHUNK)
        ki = smallb_ref[pl.ds(r0, ATT_CHUNK), :][:, 0:IDX_DIM]
        dots = _dot(ki, qcat)
        sc = w[0:1, :] * jnp.maximum(dots[:, 0:Q_BLOCK], 0.0)
        for h in range(1, IDX_HEADS):
            sc = sc + w[h:h + 1, :] * jnp.maximum(dots[:, h * Q_BLOCK:(h + 1) * Q_BLOCK], 0.0)
        causal = (row_a + r0) <= (col_a + t0)
        keys_ref[pl.ds(r0, ATT_CHUNK), :] = _sort_key(jnp.where(causal, sc, -jnp.inf))
        return carry

    lax.fori_loop(0, n_att, score_chunk, 0)

    few = (n + 1) * Q_BLOCK <= k_sel

    @pl.when(few)
    def _():
        thr_ref[...] = jnp.full((1, Q_BLOCK), KEY_NEG_INF, jnp.int32)
        need_ref[...] = jnp.zeros((1, Q_BLOCK), F32)
        flag_ref[0] = 0.0

    ge = lambda kk, t: kk >= t
    gt = lambda kk, t: kk > t
    for nb in range(1, seq // Q_BLOCK + 1):
        @pl.when((n + 1 == nb) & jnp.logical_not(few))
        def _(nb=nb):
            rows = nb * Q_BLOCK
            thr, n_ge = _kth_threshold(lambda t: _count_causal(keys_ref, rows, ge, t), k_sel, (1, Q_BLOCK), rows)
            n_gt = _count_causal(keys_ref, rows, gt, thr)
            need = k_sel - n_gt
            tie = jnp.where((thr > KEY_NEG_INF) & (n_ge - n_gt > need), 1.0, 0.0)
            thr_ref[...] = thr
            need_ref[...] = need
            flag_ref[0] = jnp.max(tie)

    thr = thr_ref[...]
    need = need_ref[...]

    @pl.when(flag_ref[0] == 0.0)
    def _():
        thr_eff = jnp.maximum(thr, KEY_NEG_INF + 1)

        def body(c, carry):
            r0 = pl.multiple_of(c * KEY_CHUNK, KEY_CHUNK)
            bias_ref[pl.ds(r0, KEY_CHUNK), :] = jnp.where(keys_ref[pl.ds(r0, KEY_CHUNK), :] >= thr_eff, 0.0, NEG)
            return carry
        lax.fori_loop(0, n_bias, body, 0)

    @pl.when(flag_ref[0] != 0.0)
    def _():
        lower = (lax.broadcasted_iota(jnp.int32, (KEY_CHUNK, KEY_CHUNK), 1)
                 < lax.broadcasted_iota(jnp.int32, (KEY_CHUNK, KEY_CHUNK), 0))
        lower = jnp.where(lower, 1.0, 0.0).astype(BF16)

        def body(c, tie_carry):
            r0 = pl.multiple_of(c * KEY_CHUNK, KEY_CHUNK)
            kk = keys_ref[pl.ds(r0, KEY_CHUNK), :]
            eq = jnp.where(kk == thr, 1.0, 0.0)
            before = _dot(lower, eq.astype(BF16)) + tie_carry
            keep_tie = jnp.where(before < need, eq, 0.0)
            sel = jnp.where(kk > thr, 1.0, keep_tie)
            sel = jnp.where((row + r0) <= (col + t0), sel, 0.0)
            bias_ref[pl.ds(r0, KEY_CHUNK), :] = jnp.where(sel > 0.5, 0.0, NEG)
            return tie_carry + jnp.sum(eq, axis=0, keepdims=True)
        lax.fori_loop(0, n_bias, body, jnp.zeros((1, Q_BLOCK), F32))

    c2 = np.float32(HEAD_DIM ** -0.5 * np.log2(np.e))
    m_ref[...] = jnp.full(m_ref.shape, NEG, F32)
    acc_ref[...] = jnp.zeros(acc_ref.shape, F32)
    ones_rows = jnp.ones((ONES_ROWS, KEY_CHUNK), BF16)
    sub = ATT_CHUNK // KEY_CHUNK

    def attend(c, carry):
        r0 = pl.multiple_of(c * ATT_CHUNK, ATT_CHUNK)
        bias = bias_ref[pl.ds(r0, ATT_CHUNK), :]
        bias4 = jnp.concatenate([bias] * KV_GROUP, axis=1)
        for j in range(N_KV_HEADS):
            qj = jnp.concatenate(
                [qt_ref[(j * KV_GROUP + g) * HEAD_DIM:(j * KV_GROUP + g + 1) * HEAD_DIM, :]
                 for g in range(KV_GROUP)], axis=1)
            kj = kb_ref[pl.ds(r0, ATT_CHUNK), j * HEAD_DIM:(j + 1) * HEAD_DIM]
            s = _dot(kj, qj) + bias4
            m_old = m_ref[j]
            m_new = jnp.maximum(m_old, jnp.max(s, axis=0, keepdims=True))
            alpha = jnp.exp2((m_old - m_new) * c2)
            p = jnp.exp2((s - m_new) * c2).astype(BF16)
            acc = alpha * acc_ref[j]
            for r in range(sub):
                vj = jnp.concatenate([vt_ref[c * sub + r, j * HEAD_DIM:(j + 1) * HEAD_DIM, :], ones_rows], axis=0)
                acc = acc + _dot(vj, p[r * KEY_CHUNK:(r + 1) * KEY_CHUNK, :])
            acc_ref[j] = acc
            m_ref[j] = m_new
        return carry

    lax.fori_loop(0, n_att, attend, 0)

    for j in range(N_KV_HEADS):
        ot = acc_ref[j, 0:HEAD_DIM, :] / acc_ref[j, HEAD_DIM:HEAD_DIM + 1, :]
        for g in range(KV_GROUP):
            h = j * KV_GROUP + g
            o_ref[:, h * HEAD_DIM:(h + 1) * HEAD_DIM] = ot[:, g * Q_BLOCK:(g + 1) * Q_BLOCK].T.astype(BF16)


def _mix(qt, qit, wit, kb, vt, smallb, batch, seq):
    n = batch * seq
    nq = seq // Q_BLOCK
    k_sel = min(TOPK_MAX, seq // 4)
    gq = KV_GROUP * Q_BLOCK
    return pl.pallas_call(
        functools.partial(_mix_kernel, k_sel=k_sel),
        grid=(batch, nq),
        in_specs=[pl.BlockSpec((D_MODEL, Q_BLOCK), lambda b, i: (0, b * nq + i)),
                  pl.BlockSpec((QI_W, Q_BLOCK), lambda b, i: (0, b * nq + i)),
                  pl.BlockSpec((IDX_HEADS, Q_BLOCK), lambda b, i: (0, b * nq + i)),
                  pl.BlockSpec((seq, KV_W), lambda b, i: (b, 0)),
                  pl.BlockSpec((seq // KEY_CHUNK, KV_W, KEY_CHUNK), lambda b, i: (b, 0, 0)),
                  pl.BlockSpec((seq, SMALL_W), lambda b, i: (b, 0))],
        out_specs=pl.BlockSpec((Q_BLOCK, D_MODEL), lambda b, i: (b * nq + i, 0)),
        out_shape=jax.ShapeDtypeStruct((n, D_MODEL), BF16),
        scratch_shapes=[pltpu.VMEM((seq, Q_BLOCK), jnp.int32),
                        pltpu.VMEM((seq, Q_BLOCK), F32),
                        pltpu.VMEM((1, Q_BLOCK), jnp.int32),
                        pltpu.VMEM((1, Q_BLOCK), F32),
                        pltpu.SMEM((1,), F32),
                        pltpu.VMEM((N_KV_HEADS, 1, gq), F32),
                        pltpu.VMEM((N_KV_HEADS, HEAD_DIM + ONES_ROWS, gq), F32)],
        compiler_params=_params(2),
        name="mix_prompt",
    )(qt, qit, wit, kb, vt, smallb)


S_IDX_PAGES = 32
S_ATT_PAGES = 32
T_PAD = 8


def _page_copies(pt_ref, b, first, count, layer, cache_ref, buf_ref, slot, sem):
    return [pltpu.make_async_copy(cache_ref.at[layer, pt_ref[b, first + p]], buf_ref.at[slot, p], sem)
            for p in range(count)]


def _paged_loop(n_groups, copies, compute):
    def start(group):
        for i, c in enumerate(group):
            c.start(priority=i % 2)

    start(copies(0, 0))

    def body(g, carry):
        slot = g % 2

        @pl.when(g + 1 < n_groups)
        def _():
            start(copies(g + 1, 1 - slot))
        for c in copies(g, slot):
            c.wait()
        compute(g, slot)
        return carry

    lax.fori_loop(0, n_groups, body, 0)


def _s_idx_kernel(pt_ref, qi_ref, wb_ref, kin_ref, cache_ref, out_ref, outn_ref, buf_ref, sem_ref, *, layer):
    b = pl.program_id(0)
    n_pages = out_ref.shape[0]
    qi = qi_ref[0]
    wb = wb_ref[0]

    def score(kit, n_pg):
        r = jnp.maximum(_dot(qi, kit) * np.float32(IDX_DIM ** -0.5), 0.0) * jnp.concatenate([wb] * n_pg, axis=1)
        sc = r[0:T_PAD, :]
        for h in range(1, IDX_HEADS):
            sc = sc + r[h * T_PAD:(h + 1) * T_PAD, :]
        return sc

    outn_ref[...] = score(kin_ref[0], 1)

    def copies(g, slot):
        return _page_copies(pt_ref, b, g * S_IDX_PAGES, S_IDX_PAGES, layer, cache_ref, buf_ref, slot, sem_ref.at[slot])

    def compute(g, slot):
        kit = jnp.concatenate([buf_ref[slot, p] for p in range(S_IDX_PAGES)], axis=1).astype(BF16)
        sc = score(kit, S_IDX_PAGES)
        for p in range(S_IDX_PAGES):
            out_ref[g * S_IDX_PAGES + p] = sc[:, p * PAGE_SIZE:(p + 1) * PAGE_SIZE]

    _paged_loop(n_pages // S_IDX_PAGES, copies, compute)


def _s_idx(page_table, qi_rows, wb_rows, kit_new, cache_kidx_t, layer):
    db, n_pages = page_table.shape
    grid_spec = pltpu.PrefetchScalarGridSpec(
        num_scalar_prefetch=1,
        grid=(db,),
        in_specs=[pl.BlockSpec((1, IDX_HEADS * T_PAD, IDX_DIM), lambda b, pt: (b, 0, 0)),
                  pl.BlockSpec((1, IDX_HEADS * T_PAD, PAGE_SIZE), lambda b, pt: (b, 0, 0)),
                  pl.BlockSpec((1, IDX_DIM, PAGE_SIZE), lambda b, pt: (b, 0, 0)),
                  pl.BlockSpec(memory_space=pl.ANY)],
        out_specs=[pl.BlockSpec((n_pages, T_PAD, PAGE_SIZE), lambda b, pt: (0, b, 0)),
                   pl.BlockSpec((T_PAD, PAGE_SIZE), lambda b, pt: (b, 0))],
        scratch_shapes=[pltpu.VMEM((2, S_IDX_PAGES, IDX_DIM, PAGE_SIZE), F32),
                        pltpu.SemaphoreType.DMA((2,))])
    return pl.pallas_call(
        functools.partial(_s_idx_kernel, layer=layer),
        grid_spec=grid_spec,
        out_shape=[jax.ShapeDtypeStruct((n_pages, db * T_PAD, PAGE_SIZE), F32),
                   jax.ShapeDtypeStruct((db * T_PAD, PAGE_SIZE), F32)],
        compiler_params=_params(1),
        name="s_idx",
    )(page_table, qi_rows, wb_rows, kit_new, cache_kidx_t)


def _s_sel_kernel(sc_ref, scn_ref, mask_ref, maskn_ref, keys_ref, *, k_sel, n_new):
    n_pages, rows, _ = sc_ref.shape
    keys_ref[0:n_pages] = _sort_key(sc_ref[...])
    t = lax.broadcasted_iota(jnp.int32, (rows, PAGE_SIZE), 0) % T_PAD
    j = lax.broadcasted_iota(jnp.int32, (rows, PAGE_SIZE), 1)
    new_ok = (j <= t) & (j < n_new)
    keys_ref[n_pages] = _sort_key(jnp.where(new_ok, scn_ref[...], -jnp.inf))

    def count(pred):
        acc = jnp.zeros((rows, PAGE_SIZE), F32)
        for c in range(n_pages + 1):
            acc = acc + jnp.where(pred(keys_ref[c]), 1.0, 0.0)
        return jnp.sum(acc, axis=1, keepdims=True)

    thr, n_ge = _kth_threshold(lambda c: count(lambda kk: kk >= c), k_sel, (rows, 1), (n_pages + 1) * PAGE_SIZE)
    n_gt = count(lambda kk: kk > thr)
    need = k_sel - n_gt
    surplus = jnp.max(jnp.where((n_ge - n_gt > need) & (t[:, 0:1] < n_new), 1.0, 0.0))

    @pl.when(surplus == 0.0)
    def _():
        mask_ref[...] = jnp.where(keys_ref[0:n_pages] >= thr, 1.0, 0.0)
        maskn_ref[...] = jnp.where(new_ok & (keys_ref[n_pages] >= thr), 1.0, 0.0)

    @pl.when(surplus != 0.0)
    def _():
        upper = (lax.broadcasted_iota(jnp.int32, (PAGE_SIZE, PAGE_SIZE), 0)
                 < lax.broadcasted_iota(jnp.int32, (PAGE_SIZE, PAGE_SIZE), 1))
        upper = jnp.where(upper, 1.0, 0.0).astype(BF16)
        ones = jnp.ones((PAGE_SIZE, PAGE_SIZE), BF16)

        def tile(c, carry):
            kk = keys_ref[c]
            eq = jnp.where(kk == thr, 1.0, 0.0)
            eqb = eq.astype(BF16)
            before = _dot(eqb, upper) + carry
            keep_tie = jnp.where(before < need, eq, 0.0)
            return jnp.where(kk > thr, 1.0, keep_tie), carry + _dot(eqb, ones)

        def past_tile(c, carry):
            sel, carry = tile(c, carry)
            mask_ref[c] = sel
            return carry

        carry = lax.fori_loop(0, n_pages, past_tile, jnp.zeros((rows, PAGE_SIZE), F32))
        sel, _ = tile(n_pages, carry)
        maskn_ref[...] = jnp.where(new_ok, sel, 0.0)


def _s_sel(scores, scores_new, k_sel, n_new, rows_per_step):
    n_pages, rows, _ = scores.shape
    return pl.pallas_call(
        functools.partial(_s_sel_kernel, k_sel=k_sel, n_new=n_new),
        grid=(rows // rows_per_step,),
        in_specs=[pl.BlockSpec((n_pages, rows_per_step, PAGE_SIZE), lambda i: (0, i, 0)),
                  pl.BlockSpec((rows_per_step, PAGE_SIZE), lambda i: (i, 0))],
        out_specs=[pl.BlockSpec((n_pages, rows_per_step, PAGE_SIZE), lambda i: (0, i, 0)),
                   pl.BlockSpec((rows_per_step, PAGE_SIZE), lambda i: (i, 0))],
        out_shape=[jax.ShapeDtypeStruct((n_pages, rows, PAGE_SIZE), F32),
                   jax.ShapeDtypeStruct((rows, PAGE_SIZE), F32)],
        scratch_shapes=[pltpu.VMEM((n_pages + 1, rows_per_step, PAGE_SIZE), jnp.int32)],
        compiler_params=_params(1),
        name="s_sel",
    )(scores, scores_new)


def _s_att_kernel(pt_ref, q_ref, mask_ref, maskn_ref, kn_ref, vn_ref, *refs):
    k_refs = refs[:S_ATT_PAGES]
    v_refs = refs[S_ATT_PAGES:2 * S_ATT_PAGES]
    o_ref, m_ref, l_ref, acc_ref = refs[2 * S_ATT_PAGES:]
    i = pl.program_id(1)
    flat = N_KV_HEADS * PAGE_SIZE
    c2 = np.float32(HEAD_DIM ** -0.5 * np.log2(np.e))
    q = q_ref[0]
    spread = (lax.broadcasted_iota(jnp.int32, (PAGE_SIZE, flat), 1) // N_KV_HEADS
              == lax.broadcasted_iota(jnp.int32, (PAGE_SIZE, flat), 0))
    spread = jnp.where(spread, 1.0, 0.0).astype(BF16)

    def update(kf, vf, sel, n_pg):
        sel2 = _dot(sel.astype(BF16), spread)
        sel2 = jnp.concatenate([sel2[p * T_PAD:(p + 1) * T_PAD, :] for p in range(n_pg)], axis=1)
        bias_t = jnp.where(sel2 > 0.5, 0.0, NEG)
        bias_g = jnp.concatenate([bias_t] * KV_GROUP, axis=0)
        parity = lax.broadcasted_iota(jnp.int32, (1, n_pg * flat), 1) % N_KV_HEADS
        bias = jnp.concatenate([bias_g + jnp.where(parity == j, 0.0, NEG) for j in range(N_KV_HEADS)], axis=0)
        s = _dot_nt(q, kf) + bias
        m_old = m_ref[...]
        m_new = jnp.maximum(m_old, jnp.max(s, axis=1, keepdims=True))
        alpha = jnp.exp2((m_old - m_new) * c2)
        p = jnp.exp2((s - m_new) * c2)
        l_ref[...] = alpha * l_ref[...] + jnp.sum(p, axis=1, keepdims=True)
        acc_ref[...] = alpha * acc_ref[...] + _dot(p.astype(BF16), vf)
        m_ref[...] = m_new

    @pl.when(i == 0)
    def _():
        m_ref[...] = jnp.full(m_ref.shape, NEG, F32)
        l_ref[...] = jnp.zeros(l_ref.shape, F32)
        acc_ref[...] = jnp.zeros(acc_ref.shape, F32)
        update(kn_ref[0], vn_ref[0], maskn_ref[...], 1)

    kf = jnp.concatenate([k_refs[p][...] for p in range(S_ATT_PAGES)], axis=0).astype(BF16)
    vf = jnp.concatenate([v_refs[p][...] for p in range(S_ATT_PAGES)], axis=0).astype(BF16)
    update(kf, vf, mask_ref[...].reshape(S_ATT_PAGES * T_PAD, PAGE_SIZE), S_ATT_PAGES)

    @pl.when(i == pl.num_programs(1) - 1)
    def _():
        o_ref[0] = acc_ref[...] / l_ref[...]


def _s_att(page_table, q_rows, mask, mask_new, k_new, v_new, cache_k, cache_v, layer):
    db, n_pages = page_table.shape
    steps = n_pages // S_ATT_PAGES
    rows = N_KV_HEADS * KV_GROUP * T_PAD
    flat = N_KV_HEADS * PAGE_SIZE
    page_spec = lambda p: pl.BlockSpec(
        (None, None, flat, HEAD_DIM), lambda b, i, pt: (layer, pt[b, i * S_ATT_PAGES + p], 0, 0))
    grid_spec = pltpu.PrefetchScalarGridSpec(
        num_scalar_prefetch=1,
        grid=(db, steps),
        in_specs=[pl.BlockSpec((1, rows, HEAD_DIM), lambda b, i, pt: (b, 0, 0)),
                  pl.BlockSpec((S_ATT_PAGES, T_PAD, PAGE_SIZE), lambda b, i, pt: (i, b, 0)),
                  pl.BlockSpec((T_PAD, PAGE_SIZE), lambda b, i, pt: (b, 0)),
                  pl.BlockSpec((1, flat, HEAD_DIM), lambda b, i, pt: (b, 0, 0)),
                  pl.BlockSpec((1, flat, HEAD_DIM), lambda b, i, pt: (b, 0, 0))]
                 + [page_spec(p) for p in range(S_ATT_PAGES)] * 2,
        out_specs=pl.BlockSpec((1, rows, HEAD_DIM), lambda b, i, pt: (b, 0, 0)),
        scratch_shapes=[pltpu.VMEM((rows, 1), F32),
                        pltpu.VMEM((rows, 1), F32),
                        pltpu.VMEM((rows, HEAD_DIM), F32)])
    return pl.pallas_call(
        _s_att_kernel,
        grid_spec=grid_spec,
        out_shape=jax.ShapeDtypeStruct((db, rows, HEAD_DIM), F32),
        compiler_params=_params(2),
        name="s_att",
    )(page_table, q_rows, mask, mask_new, k_new, v_new,
      *([cache_k] * S_ATT_PAGES), *([cache_v] * S_ATT_PAGES))


def _sample_mixer(page_table, q_b, qi_b, small_f, k_f, vv_f, cache_k_flat, cache_v_flat, cache_kidx_t, layer, dec_seq):
    db, n_pages = page_table.shape
    t = dec_seq
    past = n_pages * PAGE_SIZE
    k_sel = min(TOPK_MAX, (past + t) // 4)
    pad_t = lambda a: jnp.pad(a, [(0, 0)] * (a.ndim - 2) + [(0, T_PAD - t), (0, 0)])

    qi_rows = pad_t(qi_b.reshape(db, t, IDX_HEADS, IDX_DIM).transpose(0, 2, 1, 3))
    qi_rows = qi_rows.reshape(db, IDX_HEADS * T_PAD, IDX_DIM)
    wi = small_f[:, IDX_DIM:IDX_DIM + IDX_HEADS].reshape(db, t, IDX_HEADS).transpose(0, 2, 1)
    wb_rows = jnp.pad(wi, ((0, 0), (0, 0), (0, T_PAD - t))).reshape(db, IDX_HEADS * T_PAD, 1)
    wb_rows = jnp.broadcast_to(wb_rows, (db, IDX_HEADS * T_PAD, PAGE_SIZE))
    kit_new = small_f[:, 0:IDX_DIM].reshape(db, t, IDX_DIM).transpose(0, 2, 1)
    kit_new = jnp.pad(kit_new, ((0, 0), (0, 0), (0, PAGE_SIZE - t))).astype(BF16)

    scores, scores_new = _s_idx(page_table, qi_rows, wb_rows, kit_new, cache_kidx_t, layer)
    rows_per_step = min(db * T_PAD, 64)
    mask, mask_new = _s_sel(scores, scores_new, k_sel, t, rows_per_step)

    q_rows = pad_t(q_b.reshape(db, t, N_KV_HEADS, KV_GROUP, HEAD_DIM).transpose(0, 2, 3, 1, 4))
    q_rows = q_rows.reshape(db, N_KV_HEADS * KV_GROUP * T_PAD, HEAD_DIM)
    flat_new = lambda a: jnp.pad(a.reshape(db, t * N_KV_HEADS, HEAD_DIM),
                                 ((0, 0), (0, (PAGE_SIZE - t) * N_KV_HEADS), (0, 0))).astype(BF16)
    o = _s_att(page_table, q_rows, mask, mask_new, flat_new(k_f), flat_new(vv_f), cache_k_flat, cache_v_flat, layer)
    o = o.reshape(db, N_KV_HEADS, KV_GROUP, T_PAD, HEAD_DIM)[:, :, :, :t]
    return o.transpose(0, 3, 1, 2, 4).reshape(db * t, N_HEADS * HEAD_DIM).astype(BF16)


def _post_kernel(x_ref, u_ref, v_ref, attn_ref, sga_ref, sgb_ref, ws_ref, bias_ref, wpa_ref, wpb_ref, wout_ref,
                 o_ref, a_ref, *, block):
    tm = x_ref.shape[0]
    r = lax.broadcasted_iota(jnp.int32, (CHUNK, CHUNK), 0)
    c = lax.broadcasted_iota(jnp.int32, (CHUNK, CHUNK), 1)
    allowed = (r >= c) & ((r // block) == (c // block))
    bias = bias_ref[...]
    for g in range(A_GROUPS):
        wg = jnp.where(allowed, ws_ref[g], 0.0).astype(BF16)
        for i in range(tm // CHUNK):
            rows = slice(i * CHUNK, (i + 1) * CHUNK)
            cols = slice(g * CHUNK, (g + 1) * CHUNK)
            sv = _dot(wg, v_ref[rows, cols]) + bias[:, cols]
            a_ref[rows, cols] = (u_ref[rows, cols].astype(F32) * sv).astype(BF16)
    merged = (sga_ref[...].astype(F32) * _dot(a_ref[...], wpa_ref[...])
              + sgb_ref[...].astype(F32) * _dot(attn_ref[...], wpb_ref[...]))
    o_ref[...] = x_ref[...] + _dot(merged.astype(BF16), wout_ref[...])


def _post(x, u, v, attn, sga, sgb, ws, bias, wpa, wpb, wout, tm, block, layer):
    n = x.shape[0]
    row = lambda dt: pl.BlockSpec((tm, D_MODEL), lambda i: (i, 0))
    return pl.pallas_call(
        functools.partial(_post_kernel, block=block),
        grid=(n // tm,),
        in_specs=[row(F32), row(BF16), row(BF16), row(BF16), row(BF16), row(BF16),
                  _const_spec((A_GROUPS, CHUNK, CHUNK)), _const_spec((CHUNK, A_WIDTH)),
                  _layer_spec((A_WIDTH, D_MODEL), layer), _layer_spec((N_HEADS * HEAD_DIM, D_MODEL), layer),
                  _layer_spec((D_MODEL, D_MODEL), layer)],
        out_specs=row(F32),
        out_shape=jax.ShapeDtypeStruct((n, D_MODEL), F32),
        scratch_shapes=[pltpu.VMEM((tm, A_WIDTH), BF16)],
        compiler_params=_params(1),
        name="post",
    )(x, u, v, attn, sga, sgb, ws, bias, wpa, wpb, wout)


def kernel(x_prompt, x_sample, cache_k, cache_v, cache_kidx, page_table, g_ffn1, w_up1, w_down1, g_mix, w_in, g_v, g_q, g_k, w_s, b_s, w_pa, w_pb, w_out, g_ffn2, w_up2, w_down2):
    batch, seq, _ = x_prompt.shape
    db, dec_seq, _ = x_sample.shape
    depth = w_in.shape[0]
    n_p, n_s = batch * seq, db * dec_seq
    assert seq % ATT_CHUNK == 0 and seq % Q_BLOCK == 0 and n_s % CHUNK == 0 and CHUNK % dec_seq == 0 and dec_seq <= T_PAD
    tm_p = 512 if n_p % 512 == 0 else KEY_CHUNK
    tm_s = CHUNK

    xp = x_prompt.reshape(n_p, D_MODEL)
    xs = x_sample.reshape(n_s, D_MODEL)
    n_pool = cache_k.shape[1]
    cache_k_flat = cache_k.reshape(depth, n_pool, PAGE_SIZE * N_KV_HEADS, HEAD_DIM)
    cache_v_flat = cache_v.reshape(depth, n_pool, PAGE_SIZE * N_KV_HEADS, HEAD_DIM)
    cache_kidx_t = jnp.swapaxes(cache_kidx, 2, 3)
    wup1, wdn1, wup2, wdn2 = (w.astype(BF16) for w in (w_up1, w_down1, w_up2, w_down2))
    wpa, wpb, wout = w_pa.astype(BF16), w_pb.astype(BF16), w_out.astype(BF16)
    n_small = IDX_DIM + IDX_HEADS
    w_packed = jnp.concatenate(
        [w_in[:, :, 0:WA_W], w_in[:, :, WA_W + n_small:],
         jnp.pad(w_in[:, :, WA_W:WA_W + n_small], ((0, 0), (0, 0), (0, SMALL_W - n_small)))], axis=2).astype(BF16)
    outs_p = [[] for _ in range(3)]
    outs = [[] for _ in range(4)]
    for l in range(depth):
        row = lambda a: a[l].reshape(1, -1)
        ws_p = w_s[l]
        bias_p = jnp.repeat(b_s[l].T, CHUNK, axis=1)
        ws_s = jnp.tile(w_s[l, :, 0:dec_seq, 0:dec_seq], (1, CHUNK // dec_seq, CHUNK // dec_seq))
        bias_s = jnp.repeat(jnp.tile(b_s[l, :, 0:dec_seq].T, (CHUNK // dec_seq, 1)), CHUNK, axis=1)

        x1 = _ffn(xp, row(g_ffn1), wup1, wdn1, tm_p, l)
        (u, v, k_p, v_p, kit_p, sga, sgb, qt, qit, wit, vt, kb, smallb) = _inproj(
            x1, row(g_mix), w_packed, row(g_v), row(g_q), row(g_k), tm_p, True, l, seq=seq)
        outs_p[0].append(k_p.reshape(batch, seq, N_KV_HEADS, HEAD_DIM))
        outs_p[1].append(v_p.reshape(batch, seq, N_KV_HEADS, HEAD_DIM))
        outs_p[2].append(jnp.swapaxes(kit_p, 1, 2))
        attn = _mix(qt, qit, wit, kb, vt, smallb, batch, seq)
        x2 = _post(x1, u, v, attn, sga, sgb, ws_p, bias_p, wpa, wpb, wout, tm_p, CHUNK, l)
        xp = _ffn(x2, row(g_ffn2), wup2, wdn2, tm_p, l)

        x1 = _ffn(xs, row(g_ffn1), wup1, wdn1, tm_s, l)
        (u, v, k_f, vv_f, small_f, sga, sgb, q_b, qi_b) = _inproj(
            x1, row(g_mix), w_packed, row(g_v), row(g_q), row(g_k), tm_s, False, l)
        attn = _sample_mixer(page_table, q_b, qi_b, small_f, k_f, vv_f,
                             cache_k_flat, cache_v_flat, cache_kidx_t, l, dec_seq)
        x2 = _post(x1, u.astype(BF16), v.astype(BF16), attn, sga, sgb, ws_s, bias_s, wpa, wpb, wout, tm_s, dec_seq, l)
        xs = _ffn(x2, row(g_ffn2), wup2, wdn2, tm_s, l)
        outs[0].append(k_f.reshape(db, dec_seq, N_KV_HEADS, HEAD_DIM))
        outs[1].append(vv_f.reshape(db, dec_seq, N_KV_HEADS, HEAD_DIM))
        outs[2].append(small_f[:, 0:IDX_DIM].reshape(db, dec_seq, IDX_DIM))
        outs[3].append(v.reshape(db, dec_seq, A_WIDTH))

    return (xp.reshape(batch, seq, D_MODEL), xs.reshape(db, dec_seq, D_MODEL),
            *[jnp.stack(o) for o in outs_p], *[jnp.stack(o) for o in outs])
```

```python
import functools

import jax
import jax.numpy as jnp
import numpy as np
from jax import lax
from jax.experimental import pallas as pl
from jax.experimental.pallas import tpu as pltpu

D_MODEL = 1024
D_FF = 2816
CHUNK = 128
A_GROUPS = 8
A_WIDTH = D_MODEL
N_HEADS = 8
HEAD_DIM = 128
N_KV_HEADS = 2
KV_GROUP = N_HEADS // N_KV_HEADS
IDX_HEADS = 8
IDX_DIM = 64
TOPK_MAX = 256
Q_BLOCK = 512
PAGE_SIZE = 128
EPS = 1e-6

KV_W = N_KV_HEADS * HEAD_DIM
QI_W = IDX_HEADS * IDX_DIM
WA_W = 3 * D_MODEL + 2 * KV_W + QI_W
SMALL_W = 128
F_CHUNK = 256
KEY_CHUNK = 256
ATT_CHUNK = 512
ONES_ROWS = 16
NEG = -1e30
INT_MIN = -2 ** 31
KEY_NEG_INF = INT_MIN + 0x7FFFFF
VMEM_LIMIT = 56 * 1024 * 1024

BF16 = jnp.bfloat16
F32 = jnp.float32


def _dot(a, b):
    return jnp.dot(a, b, preferred_element_type=F32)


def _dot_nt(a, b):
    return lax.dot_general(a, b, (((1,), (1,)), ((), ())), preferred_element_type=F32)


def _rms(x, g):
    return x * lax.rsqrt(jnp.mean(x * x, axis=-1, keepdims=True) + EPS) * g


def _sigmoid(x):
    return 1.0 / (1.0 + jnp.exp(-x))


def _gelu(x):
    c = np.float32(np.sqrt(2.0 / np.pi))
    return x * (0.5 * (1.0 + jnp.tanh(c * (x + 0.044715 * (x * x * x)))))


def _const_spec(shape):
    nd = len(shape)
    return pl.BlockSpec(shape, lambda *_: (0,) * nd, pipeline_mode=pl.Buffered(1))


def _params(n_axes):
    return pltpu.CompilerParams(dimension_semantics=("arbitrary",) * n_axes, vmem_limit_bytes=VMEM_LIMIT)


def _ffn_apply(x, g_ref, wup_ref, wdn_ref, h_ref):
    xb = _rms(x, g_ref[...]).astype(BF16)
    for j in range(D_FF // F_CHUNK):
        gate = _dot(xb, wup_ref[:, j * F_CHUNK:(j + 1) * F_CHUNK])
        up = _dot(xb, wup_ref[:, D_FF + j * F_CHUNK:D_FF + (j + 1) * F_CHUNK])
        h_ref[:, j * F_CHUNK:(j + 1) * F_CHUNK] = (gate * _sigmoid(gate) * up).astype(BF16)
    return x + 0.5 * _dot(h_ref[...], wdn_ref[...])


def _ffn_kernel(x_ref, g_ref, wup_ref, wdn_ref, o_ref, h_ref):
    o_ref[...] = _ffn_apply(x_ref[...], g_ref, wup_ref, wdn_ref, h_ref)


def _ffn(x, g, wup, wdn, tm):
    n = x.shape[0]
    return pl.pallas_call(
        _ffn_kernel,
        grid=(n // tm,),
        in_specs=[pl.BlockSpec((tm, D_MODEL), lambda i: (i, 0)),
                  _const_spec((1, D_MODEL)),
                  _const_spec((D_MODEL, 2 * D_FF)),
                  _const_spec((D_FF, D_MODEL))],
        out_specs=pl.BlockSpec((tm, D_MODEL), lambda i: (i, 0)),
        out_shape=jax.ShapeDtypeStruct((n, D_MODEL), F32),
        scratch_shapes=[pltpu.VMEM((tm, D_FF), BF16)],
        compiler_params=_params(1),
        name="ffn",
    )(x, g, wup, wdn)


def _inproj_kernel(x_ref, g_ref, wa_ref, ws_ref, wg_ref, gv_ref, gq_ref, gk_ref, *out_refs, prompt):
    if prompt:
        (u_ref, v_ref, k_ref, vv_ref, kit_ref, sga_ref, sgb_ref,
         qt_ref, qit_ref, wit_ref, vt_ref, kb_ref, smallb_ref) = out_refs
    else:
        (u_ref, v_ref, k_ref, vv_ref, small_ref, sga_ref, sgb_ref, q_ref, qi_ref) = out_refs
    tm = x_ref.shape[0]
    hb = _rms(x_ref[...], g_ref[...]).astype(BF16)

    u_ref[...] = _gelu(_dot(hb, wa_ref[:, 0:D_MODEL])).astype(u_ref.dtype)
    v = _gelu(_dot(hb, wa_ref[:, D_MODEL:2 * D_MODEL]))
    v_ref[...] = _rms(v, gv_ref[...]).astype(v_ref.dtype)

    q = _dot(hb, wa_ref[:, 2 * D_MODEL:3 * D_MODEL])
    gq = gq_ref[...]
    for h in range(N_HEADS):
        qh = _rms(q[:, h * HEAD_DIM:(h + 1) * HEAD_DIM], gq)
        if prompt:
            qt_ref[h * HEAD_DIM:(h + 1) * HEAD_DIM, :] = qh.T.astype(BF16)
        else:
            q_ref[:, h * HEAD_DIM:(h + 1) * HEAD_DIM] = qh.astype(BF16)

    o = 3 * D_MODEL
    kk = _dot(hb, wa_ref[:, o:o + KV_W])
    gk = gk_ref[...]
    vv = _dot(hb, wa_ref[:, o + KV_W:o + 2 * KV_W])
    for j in range(N_KV_HEADS):
        kj = _rms(kk[:, j * HEAD_DIM:(j + 1) * HEAD_DIM], gk)
        if prompt:
            k_ref[pl.ds(j, tm, stride=N_KV_HEADS), :] = kj
            vv_ref[pl.ds(j, tm, stride=N_KV_HEADS), :] = vv[:, j * HEAD_DIM:(j + 1) * HEAD_DIM]
            kb_ref[:, j * HEAD_DIM:(j + 1) * HEAD_DIM] = kj.astype(BF16)
        else:
            k_ref[:, j * HEAD_DIM:(j + 1) * HEAD_DIM] = kj
    if prompt:
        for r in range(tm // KEY_CHUNK):
            vt_ref[r] = vv[r * KEY_CHUNK:(r + 1) * KEY_CHUNK, :].T.astype(BF16)
    else:
        vv_ref[...] = vv

    qi = _dot(hb, wa_ref[:, o + 2 * KV_W:o + 2 * KV_W + QI_W])
    if prompt:
        qit_ref[...] = qi.T.astype(BF16)
    else:
        qi_ref[...] = qi.astype(BF16)

    small = _dot(hb, ws_ref[...])
    lane = lax.broadcasted_iota(jnp.int32, small.shape, 1)
    is_w = (lane >= IDX_DIM) & (lane < IDX_DIM + IDX_HEADS)
    small = small * jnp.where(is_w, np.float32(IDX_HEADS ** -0.5), np.float32(1.0))
    if prompt:
        smallb_ref[...] = small.astype(BF16)
        small_t = small.T
        kit_ref[0] = small_t[0:IDX_DIM, :]
        wit_ref[...] = small_t[IDX_DIM:IDX_DIM + IDX_HEADS, :]
    else:
        small_ref[...] = small

    sga_ref[...] = _sigmoid(_dot(hb, wg_ref[:, 0:D_MODEL])).astype(BF16)
    sgb_ref[...] = _sigmoid(_dot(hb, wg_ref[:, D_MODEL:2 * D_MODEL])).astype(BF16)


def _inproj(x, g, wa, ws, wg, gv, gq, gk, tm, prompt, seq=None):
    n = x.shape[0]
    steps = n // tm
    row = lambda w: pl.BlockSpec((tm, w), lambda i: (i, 0))
    col = lambda h: pl.BlockSpec((h, tm), lambda i: (0, i))
    if prompt:
        per_seq = seq // tm
        kv_shape = jax.ShapeDtypeStruct((n * N_KV_HEADS, HEAD_DIM), F32)
        kv_spec = pl.BlockSpec((tm * N_KV_HEADS, HEAD_DIM), lambda i: (i, 0))
        kit_shape = jax.ShapeDtypeStruct((n // seq, IDX_DIM, seq), F32)
        kit_spec = pl.BlockSpec((1, IDX_DIM, tm), lambda i: (i // per_seq, 0, i % per_seq))
        third = [(kv_shape, kv_spec), (kv_shape, kv_spec), (kit_shape, kit_spec)]
    else:
        third = [(jax.ShapeDtypeStruct((n, KV_W), F32), row(KV_W)),
                 (jax.ShapeDtypeStruct((n, KV_W), F32), row(KV_W)),
                 (jax.ShapeDtypeStruct((n, SMALL_W), F32), row(SMALL_W))]
    out_shape = [jax.ShapeDtypeStruct((n, D_MODEL), BF16),
                 jax.ShapeDtypeStruct((n, A_WIDTH), BF16 if prompt else F32),
                 *[s for s, _ in third],
                 jax.ShapeDtypeStruct((n, D_MODEL), BF16),
                 jax.ShapeDtypeStruct((n, D_MODEL), BF16)]
    out_specs = [row(D_MODEL), row(A_WIDTH), *[s for _, s in third], row(D_MODEL), row(D_MODEL)]
    if prompt:
        out_shape += [jax.ShapeDtypeStruct((D_MODEL, n), BF16),
                      jax.ShapeDtypeStruct((QI_W, n), BF16),
                      jax.ShapeDtypeStruct((IDX_HEADS, n), F32),
                      jax.ShapeDtypeStruct((n // KEY_CHUNK, KV_W, KEY_CHUNK), BF16),
                      jax.ShapeDtypeStruct((n, KV_W), BF16),
                      jax.ShapeDtypeStruct((n, SMALL_W), BF16)]
        out_specs += [col(D_MODEL), col(QI_W), col(IDX_HEADS),
                      pl.BlockSpec((tm // KEY_CHUNK, KV_W, KEY_CHUNK), lambda i: (i, 0, 0)),
                      row(KV_W), row(SMALL_W)]
    else:
        out_shape += [jax.ShapeDtypeStruct((n, D_MODEL), BF16),
                      jax.ShapeDtypeStruct((n, QI_W), BF16)]
        out_specs += [row(D_MODEL), row(QI_W)]
    return pl.pallas_call(
        functools.partial(_inproj_kernel, prompt=prompt),
        grid=(steps,),
        in_specs=[row(D_MODEL), _const_spec((1, D_MODEL)),
                  _const_spec((D_MODEL, WA_W)), _const_spec((D_MODEL, SMALL_W)), _const_spec((D_MODEL, 2 * D_MODEL)),
                  _const_spec((1, A_WIDTH)), _const_spec((1, HEAD_DIM)), _const_spec((1, HEAD_DIM))],
        out_specs=out_specs,
        out_shape=out_shape,
        compiler_params=_params(1),
        name="inproj_prompt" if prompt else "inproj_sample",
    )(x, g, wa, ws, wg, gv, gq, gk)


def _sort_key(x):
    bits = pltpu.bitcast(x, jnp.int32)
    bits = jnp.where(bits == INT_MIN, 0, bits)
    return jnp.where(bits < 0, bits ^ 0x7FFFFFFF, bits)


def _kth_threshold(count_ge, k, shape, total):
    def body(i, carry):
        t_u, cnt_t = carry
        cand_u = t_u | jnp.left_shift(jnp.int32(1), 31 - i)
        cnt = count_ge(cand_u ^ INT_MIN)
        ok = cnt >= k
        return jnp.where(ok, cand_u, t_u), jnp.where(ok, cnt, cnt_t)
    t_u, cnt_t = lax.fori_loop(0, 32, body, (jnp.zeros(shape, jnp.int32), jnp.full(shape, total, F32)))
    return t_u ^ INT_MIN, cnt_t


def _count_causal(keys_ref, rows, pred, t):
    lanes = keys_ref.shape[1]
    tiles = lanes // 128
    n_acc = 4
    cols = [slice(j * 128, (j + 1) * 128) for j in range(tiles)]
    ts = [t[:, c] for c in cols]
    accs = [[jnp.zeros((8, 128), F32) for _ in range(n_acc)] for _ in range(tiles)]
    for i in range(rows // 8):
        for j in range(tiles):
            if i * 8 < rows - lanes + (j + 1) * 128:
                hit = jnp.where(pred(keys_ref[i * 8:(i + 1) * 8, cols[j]], ts[j]), 1.0, 0.0)
                accs[j][i % n_acc] = accs[j][i % n_acc] + hit
    return jnp.concatenate([jnp.sum((a[0] + a[1]) + (a[2] + a[3]), axis=0, keepdims=True) for a in accs], axis=1)


def _mix_kernel(qt_ref, qit_ref, wit_ref, kb_ref, vt_ref, smallb_ref, o_ref,
                keys_ref, bias_ref, thr_ref, need_ref, flag_ref, m_ref, acc_ref, *, k_sel):
    seq = keys_ref.shape[0]
    n = pl.program_id(1)
    t0 = n * Q_BLOCK
    last = (n + 1) * Q_BLOCK - 1
    n_att = last // ATT_CHUNK + 1
    n_bias = n_att * (ATT_CHUNK // KEY_CHUNK)
    row = lax.broadcasted_iota(jnp.int32, (KEY_CHUNK, Q_BLOCK), 0)
    col = lax.broadcasted_iota(jnp.int32, (KEY_CHUNK, Q_BLOCK), 1)
    row_a = lax.broadcasted_iota(jnp.int32, (ATT_CHUNK, Q_BLOCK), 0)
    col_a = lax.broadcasted_iota(jnp.int32, (ATT_CHUNK, Q_BLOCK), 1)

    qcat = jnp.concatenate([qit_ref[h * IDX_DIM:(h + 1) * IDX_DIM, :] for h in range(IDX_HEADS)], axis=1)
    w = wit_ref[...] * np.float32(IDX_DIM ** -0.5)

    def score_chunk(c, carry):
        r0 = pl.multiple_of(c * ATT_CHUNK, ATT_CHUNK)
        ki = smallb_ref[pl.ds(r0, ATT_CHUNK), :][:, 0:IDX_DIM]
        dots = _dot(ki, qcat)
        sc = w[0:1, :] * jnp.maximum(dots[:, 0:Q_BLOCK], 0.0)
        for h in range(1, IDX_HEADS):
            sc = sc + w[h:h + 1, :] * jnp.maximum(dots[:, h * Q_BLOCK:(h + 1) * Q_BLOCK], 0.0)
        causal = (row_a + r0) <= (col_a + t0)
        keys_ref[pl.ds(r0, ATT_CHUNK), :] = _sort_key(jnp.where(causal, sc, -jnp.inf))
        return carry

    lax.fori_loop(0, n_att, score_chunk, 0)

    few = (n + 1) * Q_BLOCK <= k_sel

    @pl.when(few)
    def _():
        thr_ref[...] = jnp.full((1, Q_BLOCK), KEY_NEG_INF, jnp.int32)
        need_ref[...] = jnp.zeros((1, Q_BLOCK), F32)
        flag_ref[0] = 0.0

    ge = lambda kk, t: kk >= t
    gt = lambda kk, t: kk > t
    for nb in range(1, seq // Q_BLOCK + 1):
        @pl.when((n + 1 == nb) & jnp.logical_not(few))
        def _(nb=nb):
            rows = nb * Q_BLOCK
            thr, n_ge = _kth_threshold(lambda t: _count_causal(keys_ref, rows, ge, t), k_sel, (1, Q_BLOCK), rows)
            n_gt = _count_causal(keys_ref, rows, gt, thr)
            need = k_sel - n_gt
            tie = jnp.where((thr > KEY_NEG_INF) & (n_ge - n_gt > need), 1.0, 0.0)
            thr_ref[...] = thr
            need_ref[...] = need
            flag_ref[0] = jnp.max(tie)

    thr = thr_ref[...]
    need = need_ref[...]

    @pl.when(flag_ref[0] == 0.0)
    def _():
        thr_eff = jnp.maximum(thr, KEY_NEG_INF + 1)

        def body(c, carry):
            r0 = pl.multiple_of(c * KEY_CHUNK, KEY_CHUNK)
            bias_ref[pl.ds(r0, KEY_CHUNK), :] = jnp.where(keys_ref[pl.ds(r0, KEY_CHUNK), :] >= thr_eff, 0.0, NEG)
            return carry
        lax.fori_loop(0, n_bias, body, 0)

    @pl.when(flag_ref[0] != 0.0)
    def _():
        lower = (lax.broadcasted_iota(jnp.int32, (KEY_CHUNK, KEY_CHUNK), 1)
                 < lax.broadcasted_iota(jnp.int32, (KEY_CHUNK, KEY_CHUNK), 0))
        lower = jnp.where(lower, 1.0, 0.0).astype(BF16)

        def body(c, tie_carry):
            r0 = pl.multiple_of(c * KEY_CHUNK, KEY_CHUNK)
            kk = keys_ref[pl.ds(r0, KEY_CHUNK), :]
            eq = jnp.where(kk == thr, 1.0, 0.0)
            before = _dot(lower, eq.astype(BF16)) + tie_carry
            keep_tie = jnp.where(before < need, eq, 0.0)
            sel = jnp.where(kk > thr, 1.0, keep_tie)
            sel = jnp.where((row + r0) <= (col + t0), sel, 0.0)
            bias_ref[pl.ds(r0, KEY_CHUNK), :] = jnp.where(sel > 0.5, 0.0, NEG)
            return tie_carry + jnp.sum(eq, axis=0, keepdims=True)
        lax.fori_loop(0, n_bias, body, jnp.zeros((1, Q_BLOCK), F32))

    c2 = np.float32(HEAD_DIM ** -0.5 * np.log2(np.e))
    m_ref[...] = jnp.full(m_ref.shape, NEG, F32)
    acc_ref[...] = jnp.zeros(acc_ref.shape, F32)
    ones_rows = jnp.ones((ONES_ROWS, KEY_CHUNK), BF16)
    sub = ATT_CHUNK // KEY_CHUNK

    def attend(c, carry):
        r0 = pl.multiple_of(c * ATT_CHUNK, ATT_CHUNK)
        bias = bias_ref[pl.ds(r0, ATT_CHUNK), :]
        bias4 = jnp.concatenate([bias] * KV_GROUP, axis=1)
        for j in range(N_KV_HEADS):
            qj = jnp.concatenate(
                [qt_ref[(j * KV_GROUP + g) * HEAD_DIM:(j * KV_GROUP + g + 1) * HEAD_DIM, :]
                 for g in range(KV_GROUP)], axis=1)
            kj = kb_ref[pl.ds(r0, ATT_CHUNK), j * HEAD_DIM:(j + 1) * HEAD_DIM]
            s = _dot(kj, qj) + bias4
            m_old = m_ref[j]
            m_new = jnp.maximum(m_old, jnp.max(s, axis=0, keepdims=True))
            alpha = jnp.exp2((m_old - m_new) * c2)
            p = jnp.exp2((s - m_new) * c2).astype(BF16)
            acc = alpha * acc_ref[j]
            for r in range(sub):
                vj = jnp.concatenate([vt_ref[c * sub + r, j * HEAD_DIM:(j + 1) * HEAD_DIM, :], ones_rows], axis=0)
                acc = acc + _dot(vj, p[r * KEY_CHUNK:(r + 1) * KEY_CHUNK, :])
            acc_ref[j] = acc
            m_ref[j] = m_new
        return carry

    lax.fori_loop(0, n_att, attend, 0)

    for j in range(N_KV_HEADS):
        ot = acc_ref[j, 0:HEAD_DIM, :] / acc_ref[j, HEAD_DIM:HEAD_DIM + 1, :]
        for g in range(KV_GROUP):
            h = j * KV_GROUP + g
            o_ref[:, h * HEAD_DIM:(h + 1) * HEAD_DIM] = ot[:, g * Q_BLOCK:(g + 1) * Q_BLOCK].T.astype(BF16)


def _mix(qt, qit, wit, kb, vt, smallb, batch, seq):
    n = batch * seq
    nq = seq // Q_BLOCK
    k_sel = min(TOPK_MAX, seq // 4)
    gq = KV_GROUP * Q_BLOCK
    return pl.pallas_call(
        functools.partial(_mix_kernel, k_sel=k_sel),
        grid=(batch, nq),
        in_specs=[pl.BlockSpec((D_MODEL, Q_BLOCK), lambda b, i: (0, b * nq + i)),
                  pl.BlockSpec((QI_W, Q_BLOCK), lambda b, i: (0, b * nq + i)),
                  pl.BlockSpec((IDX_HEADS, Q_BLOCK), lambda b, i: (0, b * nq + i)),
                  pl.BlockSpec((seq, KV_W), lambda b, i: (b, 0)),
                  pl.BlockSpec((seq // KEY_CHUNK, KV_W, KEY_CHUNK), lambda b, i: (b, 0, 0)),
                  pl.BlockSpec((seq, SMALL_W), lambda b, i: (b, 0))],
        out_specs=pl.BlockSpec((Q_BLOCK, D_MODEL), lambda b, i: (b * nq + i, 0)),
        out_shape=jax.ShapeDtypeStruct((n, D_MODEL), BF16),
        scratch_shapes=[pltpu.VMEM((seq, Q_BLOCK), jnp.int32),
                        pltpu.VMEM((seq, Q_BLOCK), F32),
                        pltpu.VMEM((1, Q_BLOCK), jnp.int32),
                        pltpu.VMEM((1, Q_BLOCK), F32),
                        pltpu.SMEM((1,), F32),
                        pltpu.VMEM((N_KV_HEADS, 1, gq), F32),
                        pltpu.VMEM((N_KV_HEADS, HEAD_DIM + ONES_ROWS, gq), F32)],
        compiler_params=_params(2),
        name="mix_prompt",
    )(qt, qit, wit, kb, vt, smallb)


S_IDX_PAGES = 32
S_ATT_PAGES = 32
T_PAD = 8


def _page_copies(pt_ref, b, first, count, layer, cache_ref, buf_ref, slot, sem):
    return [pltpu.make_async_copy(cache_ref.at[layer, pt_ref[b, first + p]], buf_ref.at[slot, p], sem)
            for p in range(count)]


def _paged_loop(n_groups, copies, compute):
    def start(group):
        for i, c in enumerate(group):
            c.start(priority=i % 2)

    start(copies(0, 0))

    def body(g, carry):
        slot = g % 2

        @pl.when(g + 1 < n_groups)
        def _():
            start(copies(g + 1, 1 - slot))
        for c in copies(g, slot):
            c.wait()
        compute(g, slot)
        return carry

    lax.fori_loop(0, n_groups, body, 0)


def _s_idx_kernel(pt_ref, qi_ref, wb_ref, kin_ref, cache_ref, out_ref, outn_ref, buf_ref, sem_ref, *, layer):
    b = pl.program_id(0)
    n_pages = out_ref.shape[0]
    qi = qi_ref[0]
    wb = wb_ref[0]

    def score(kit, n_pg):
        r = jnp.maximum(_dot(qi, kit) * np.float32(IDX_DIM ** -0.5), 0.0) * jnp.concatenate([wb] * n_pg, axis=1)
        sc = r[0:T_PAD, :]
        for h in range(1, IDX_HEADS):
            sc = sc + r[h * T_PAD:(h + 1) * T_PAD, :]
        return sc

    outn_ref[...] = score(kin_ref[0], 1)

    def copies(g, slot):
        return _page_copies(pt_ref, b, g * S_IDX_PAGES, S_IDX_PAGES, layer, cache_ref, buf_ref, slot, sem_ref.at[slot])

    def compute(g, slot):
        kit = jnp.concatenate([buf_ref[slot, p] for p in range(S_IDX_PAGES)], axis=1).astype(BF16)
        sc = score(kit, S_IDX_PAGES)
        for p in range(S_IDX_PAGES):
            out_ref[g * S_IDX_PAGES + p] = sc[:, p * PAGE_SIZE:(p + 1) * PAGE_SIZE]

    _paged_loop(n_pages // S_IDX_PAGES, copies, compute)


def _s_idx(page_table, qi_rows, wb_rows, kit_new, cache_kidx_t, layer):
    db, n_pages = page_table.shape
    grid_spec = pltpu.PrefetchScalarGridSpec(
        num_scalar_prefetch=1,
        grid=(db,),
        in_specs=[pl.BlockSpec((1, IDX_HEADS * T_PAD, IDX_DIM), lambda b, pt: (b, 0, 0)),
                  pl.BlockSpec((1, IDX_HEADS * T_PAD, PAGE_SIZE), lambda b, pt: (b, 0, 0)),
                  pl.BlockSpec((1, IDX_DIM, PAGE_SIZE), lambda b, pt: (b, 0, 0)),
                  pl.BlockSpec(memory_space=pl.ANY)],
        out_specs=[pl.BlockSpec((n_pages, T_PAD, PAGE_SIZE), lambda b, pt: (0, b, 0)),
                   pl.BlockSpec((T_PAD, PAGE_SIZE), lambda b, pt: (b, 0))],
        scratch_shapes=[pltpu.VMEM((2, S_IDX_PAGES, IDX_DIM, PAGE_SIZE), F32),
                        pltpu.SemaphoreType.DMA((2,))])
    return pl.pallas_call(
        functools.partial(_s_idx_kernel, layer=layer),
        grid_spec=grid_spec,
        out_shape=[jax.ShapeDtypeStruct((n_pages, db * T_PAD, PAGE_SIZE), F32),
                   jax.ShapeDtypeStruct((db * T_PAD, PAGE_SIZE), F32)],
        compiler_params=_params(1),
        name="s_idx",
    )(page_table, qi_rows, wb_rows, kit_new, cache_kidx_t)


def _s_sel_kernel(sc_ref, scn_ref, mask_ref, maskn_ref, keys_ref, *, k_sel, n_new):
    n_pages, rows, _ = sc_ref.shape
    keys_ref[0:n_pages] = _sort_key(sc_ref[...])
    t = lax.broadcasted_iota(jnp.int32, (rows, PAGE_SIZE), 0) % T_PAD
    j = lax.broadcasted_iota(jnp.int32, (rows, PAGE_SIZE), 1)
    new_ok = (j <= t) & (j < n_new)
    keys_ref[n_pages] = _sort_key(jnp.where(new_ok, scn_ref[...], -jnp.inf))

    def count(pred):
        acc = jnp.zeros((rows, PAGE_SIZE), F32)
        for c in range(n_pages + 1):
            acc = acc + jnp.where(pred(keys_ref[c]), 1.0, 0.0)
        return jnp.sum(acc, axis=1, keepdims=True)

    thr, n_ge = _kth_threshold(lambda c: count(lambda kk: kk >= c), k_sel, (rows, 1), (n_pages + 1) * PAGE_SIZE)
    n_gt = count(lambda kk: kk > thr)
    need = k_sel - n_gt
    surplus = jnp.max(jnp.where((n_ge - n_gt > need) & (t[:, 0:1] < n_new), 1.0, 0.0))

    @pl.when(surplus == 0.0)
    def _():
        mask_ref[...] = jnp.where(keys_ref[0:n_pages] >= thr, 1.0, 0.0)
        maskn_ref[...] = jnp.where(new_ok & (keys_ref[n_pages] >= thr), 1.0, 0.0)

    @pl.when(surplus != 0.0)
    def _():
        upper = (lax.broadcasted_iota(jnp.int32, (PAGE_SIZE, PAGE_SIZE), 0)
                 < lax.broadcasted_iota(jnp.int32, (PAGE_SIZE, PAGE_SIZE), 1))
        upper = jnp.where(upper, 1.0, 0.0).astype(BF16)
        ones = jnp.ones((PAGE_SIZE, PAGE_SIZE), BF16)

        def tile(c, carry):
            kk = keys_ref[c]
            eq = jnp.where(kk == thr, 1.0, 0.0)
            eqb = eq.astype(BF16)
            before = _dot(eqb, upper) + carry
            keep_tie = jnp.where(before < need, eq, 0.0)
            return jnp.where(kk > thr, 1.0, keep_tie), carry + _dot(eqb, ones)

        def past_tile(c, carry):
            sel, carry = tile(c, carry)
            mask_ref[c] = sel
            return carry

        carry = lax.fori_loop(0, n_pages, past_tile, jnp.zeros((rows, PAGE_SIZE), F32))
        sel, _ = tile(n_pages, carry)
        maskn_ref[...] = jnp.where(new_ok, sel, 0.0)


def _s_sel(scores, scores_new, k_sel, n_new, rows_per_step):
    n_pages, rows, _ = scores.shape
    return pl.pallas_call(
        functools.partial(_s_sel_kernel, k_sel=k_sel, n_new=n_new),
        grid=(rows // rows_per_step,),
        in_specs=[pl.BlockSpec((n_pages, rows_per_step, PAGE_SIZE), lambda i: (0, i, 0)),
                  pl.BlockSpec((rows_per_step, PAGE_SIZE), lambda i: (i, 0))],
        out_specs=[pl.BlockSpec((n_pages, rows_per_step, PAGE_SIZE), lambda i: (0, i, 0)),
                   pl.BlockSpec((rows_per_step, PAGE_SIZE), lambda i: (i, 0))],
        out_shape=[jax.ShapeDtypeStruct((n_pages, rows, PAGE_SIZE), F32),
                   jax.ShapeDtypeStruct((rows, PAGE_SIZE), F32)],
        scratch_shapes=[pltpu.VMEM((n_pages + 1, rows_per_step, PAGE_SIZE), jnp.int32)],
        compiler_params=_params(1),
        name="s_sel",
    )(scores, scores_new)


def _s_att_kernel(pt_ref, q_ref, mask_ref, maskn_ref, kn_ref, vn_ref, *refs):
    k_refs = refs[:S_ATT_PAGES]
    v_refs = refs[S_ATT_PAGES:2 * S_ATT_PAGES]
    o_ref, m_ref, l_ref, acc_ref = refs[2 * S_ATT_PAGES:]
    i = pl.program_id(1)
    flat = N_KV_HEADS * PAGE_SIZE
    c2 = np.float32(HEAD_DIM ** -0.5 * np.log2(np.e))
    q = q_ref[0]
    spread = (lax.broadcasted_iota(jnp.int32, (PAGE_SIZE, flat), 1) // N_KV_HEADS
              == lax.broadcasted_iota(jnp.int32, (PAGE_SIZE, flat), 0))
    spread = jnp.where(spread, 1.0, 0.0).astype(BF16)

    def update(kf, vf, sel, n_pg):
        sel2 = _dot(sel.astype(BF16), spread)
        sel2 = jnp.concatenate([sel2[p * T_PAD:(p + 1) * T_PAD, :] for p in range(n_pg)], axis=1)
        bias_t = jnp.where(sel2 > 0.5, 0.0, NEG)
        bias_g = jnp.concatenate([bias_t] * KV_GROUP, axis=0)
        parity = lax.broadcasted_iota(jnp.int32, (1, n_pg * flat), 1) % N_KV_HEADS
        bias = jnp.concatenate([bias_g + jnp.where(parity == j, 0.0, NEG) for j in range(N_KV_HEADS)], axis=0)
        s = _dot_nt(q, kf) + bias
        m_old = m_ref[...]
        m_new = jnp.maximum(m_old, jnp.max(s, axis=1, keepdims=True))
        alpha = jnp.exp2((m_old - m_new) * c2)
        p = jnp.exp2((s - m_new) * c2)
        l_ref[...] = alpha * l_ref[...] + jnp.sum(p, axis=1, keepdims=True)
        acc_ref[...] = alpha * acc_ref[...] + _dot(p.astype(BF16), vf)
        m_ref[...] = m_new

    @pl.when(i == 0)
    def _():
        m_ref[...] = jnp.full(m_ref.shape, NEG, F32)
        l_ref[...] = jnp.zeros(l_ref.shape, F32)
        acc_ref[...] = jnp.zeros(acc_ref.shape, F32)
        update(kn_ref[0], vn_ref[0], maskn_ref[...], 1)

    kf = jnp.concatenate([k_refs[p][...] for p in range(S_ATT_PAGES)], axis=0).astype(BF16)
    vf = jnp.concatenate([v_refs[p][...] for p in range(S_ATT_PAGES)], axis=0).astype(BF16)
    update(kf, vf, mask_ref[...].reshape(S_ATT_PAGES * T_PAD, PAGE_SIZE), S_ATT_PAGES)

    @pl.when(i == pl.num_programs(1) - 1)
    def _():
        o_ref[0] = acc_ref[...] / l_ref[...]


def _s_att(page_table, q_rows, mask, mask_new, k_new, v_new, cache_k, cache_v, layer):
    db, n_pages = page_table.shape
    steps = n_pages // S_ATT_PAGES
    rows = N_KV_HEADS * KV_GROUP * T_PAD
    flat = N_KV_HEADS * PAGE_SIZE
    page_spec = lambda p: pl.BlockSpec(
        (None, None, flat, HEAD_DIM), lambda b, i, pt: (layer, pt[b, i * S_ATT_PAGES + p], 0, 0))
    grid_spec = pltpu.PrefetchScalarGridSpec(
        num_scalar_prefetch=1,
        grid=(db, steps),
        in_specs=[pl.BlockSpec((1, rows, HEAD_DIM), lambda b, i, pt: (b, 0, 0)),
                  pl.BlockSpec((S_ATT_PAGES, T_PAD, PAGE_SIZE), lambda b, i, pt: (i, b, 0)),
                  pl.BlockSpec((T_PAD, PAGE_SIZE), lambda b, i, pt: (b, 0)),
                  pl.BlockSpec((1, flat, HEAD_DIM), lambda b, i, pt: (b, 0, 0)),
                  pl.BlockSpec((1, flat, HEAD_DIM), lambda b, i, pt: (b, 0, 0))]
                 + [page_spec(p) for p in range(S_ATT_PAGES)] * 2,
        out_specs=pl.BlockSpec((1, rows, HEAD_DIM), lambda b, i, pt: (b, 0, 0)),
        scratch_shapes=[pltpu.VMEM((rows, 1), F32),
                        pltpu.VMEM((rows, 1), F32),
                        pltpu.VMEM((rows, HEAD_DIM), F32)])
    return pl.pallas_call(
        _s_att_kernel,
        grid_spec=grid_spec,
        out_shape=jax.ShapeDtypeStruct((db, rows, HEAD_DIM), F32),
        compiler_params=_params(2),
        name="s_att",
    )(page_table, q_rows, mask, mask_new, k_new, v_new,
      *([cache_k] * S_ATT_PAGES), *([cache_v] * S_ATT_PAGES))


def _sample_mixer(page_table, q_b, qi_b, small_f, k_f, vv_f, cache_k_flat, cache_v_flat, cache_kidx_t, layer, dec_seq):
    db, n_pages = page_table.shape
    t = dec_seq
    past = n_pages * PAGE_SIZE
    k_sel = min(TOPK_MAX, (past + t) // 4)
    pad_t = lambda a: jnp.pad(a, [(0, 0)] * (a.ndim - 2) + [(0, T_PAD - t), (0, 0)])

    qi_rows = pad_t(qi_b.reshape(db, t, IDX_HEADS, IDX_DIM).transpose(0, 2, 1, 3))
    qi_rows = qi_rows.reshape(db, IDX_HEADS * T_PAD, IDX_DIM)
    wi = small_f[:, IDX_DIM:IDX_DIM + IDX_HEADS].reshape(db, t, IDX_HEADS).transpose(0, 2, 1)
    wb_rows = jnp.pad(wi, ((0, 0), (0, 0), (0, T_PAD - t))).reshape(db, IDX_HEADS * T_PAD, 1)
    wb_rows = jnp.broadcast_to(wb_rows, (db, IDX_HEADS * T_PAD, PAGE_SIZE))
    kit_new = small_f[:, 0:IDX_DIM].reshape(db, t, IDX_DIM).transpose(0, 2, 1)
    kit_new = jnp.pad(kit_new, ((0, 0), (0, 0), (0, PAGE_SIZE - t))).astype(BF16)

    scores, scores_new = _s_idx(page_table, qi_rows, wb_rows, kit_new, cache_kidx_t, layer)
    rows_per_step = min(db * T_PAD, 64)
    mask, mask_new = _s_sel(scores, scores_new, k_sel, t, rows_per_step)

    q_rows = pad_t(q_b.reshape(db, t, N_KV_HEADS, KV_GROUP, HEAD_DIM).transpose(0, 2, 3, 1, 4))
    q_rows = q_rows.reshape(db, N_KV_HEADS * KV_GROUP * T_PAD, HEAD_DIM)
    flat_new = lambda a: jnp.pad(a.reshape(db, t * N_KV_HEADS, HEAD_DIM),
                                 ((0, 0), (0, (PAGE_SIZE - t) * N_KV_HEADS), (0, 0))).astype(BF16)
    o = _s_att(page_table, q_rows, mask, mask_new, flat_new(k_f), flat_new(vv_f), cache_k_flat, cache_v_flat, layer)
    o = o.reshape(db, N_KV_HEADS, KV_GROUP, T_PAD, HEAD_DIM)[:, :, :, :t]
    return o.transpose(0, 3, 1, 2, 4).reshape(db * t, N_HEADS * HEAD_DIM).astype(BF16)


def _post_kernel(x_ref, u_ref, v_ref, attn_ref, sga_ref, sgb_ref, ws_ref, bias_ref, wpa_ref, wpb_ref, wout_ref,
                 o_ref, a_ref, *, block):
    tm = x_ref.shape[0]
    r = lax.broadcasted_iota(jnp.int32, (CHUNK, CHUNK), 0)
    c = lax.broadcasted_iota(jnp.int32, (CHUNK, CHUNK), 1)
    allowed = (r >= c) & ((r // block) == (c // block))
    bias = bias_ref[...]
    for g in range(A_GROUPS):
        wg = jnp.where(allowed, ws_ref[g], 0.0).astype(BF16)
        for i in range(tm // CHUNK):
            rows = slice(i * CHUNK, (i + 1) * CHUNK)
            cols = slice(g * CHUNK, (g + 1) * CHUNK)
            sv = _dot(wg, v_ref[rows, cols]) + bias[:, cols]
            a_ref[rows, cols] = (u_ref[rows, cols].astype(F32) * sv).astype(BF16)
    merged = (sga_ref[...].astype(F32) * _dot(a_ref[...], wpa_ref[...])
              + sgb_ref[...].astype(F32) * _dot(attn_ref[...], wpb_ref[...]))
    o_ref[...] = x_ref[...] + _dot(merged.astype(BF16), wout_ref[...])


def _post(x, u, v, attn, sga, sgb, ws, bias, wpa, wpb, wout, tm, block):
    n = x.shape[0]
    row = lambda dt: pl.BlockSpec((tm, D_MODEL), lambda i: (i, 0))
    return pl.pallas_call(
        functools.partial(_post_kernel, block=block),
        grid=(n // tm,),
        in_specs=[row(F32), row(BF16), row(BF16), row(BF16), row(BF16), row(BF16),
                  _const_spec((A_GROUPS, CHUNK, CHUNK)), _const_spec((CHUNK, A_WIDTH)),
                  _const_spec((A_WIDTH, D_MODEL)), _const_spec((N_HEADS * HEAD_DIM, D_MODEL)),
                  _const_spec((D_MODEL, D_MODEL))],
        out_specs=row(F32),
        out_shape=jax.ShapeDtypeStruct((n, D_MODEL), F32),
        scratch_shapes=[pltpu.VMEM((tm, A_WIDTH), BF16)],
        compiler_params=_params(1),
        name="post",
    )(x, u, v, attn, sga, sgb, ws, bias, wpa, wpb, wout)


def kernel(x_prompt, x_sample, cache_k, cache_v, cache_kidx, page_table, g_ffn1, w_up1, w_down1, g_mix, w_in, g_v, g_q, g_k, w_s, b_s, w_pa, w_pb, w_out, g_ffn2, w_up2, w_down2):
    batch, seq, _ = x_prompt.shape
    db, dec_seq, _ = x_sample.shape
    depth = w_in.shape[0]
    n_p, n_s = batch * seq, db * dec_seq
    assert seq % ATT_CHUNK == 0 and seq % Q_BLOCK == 0 and n_s % CHUNK == 0 and CHUNK % dec_seq == 0 and dec_seq <= T_PAD
    tm_p = 512 if n_p % 512 == 0 else KEY_CHUNK
    tm_s = CHUNK

    xp = x_prompt.reshape(n_p, D_MODEL)
    xs = x_sample.reshape(n_s, D_MODEL)
    n_pool = cache_k.shape[1]
    cache_k_flat = cache_k.reshape(depth, n_pool, PAGE_SIZE * N_KV_HEADS, HEAD_DIM)
    cache_v_flat = cache_v.reshape(depth, n_pool, PAGE_SIZE * N_KV_HEADS, HEAD_DIM)
    cache_kidx_t = jnp.swapaxes(cache_kidx, 2, 3)
    outs_p = [[] for _ in range(3)]
    outs = [[] for _ in range(4)]
    for l in range(depth):
        row = lambda a: a[l].reshape(1, -1)
        wup1, wdn1 = w_up1[l].astype(BF16), w_down1[l].astype(BF16)
        wup2, wdn2 = w_up2[l].astype(BF16), w_down2[l].astype(BF16)
        wa = w_in[l, :, 0:WA_W].astype(BF16)
        n_small = IDX_DIM + IDX_HEADS
        ws = jnp.pad(w_in[l, :, WA_W:WA_W + n_small], ((0, 0), (0, SMALL_W - n_small))).astype(BF16)
        wg = w_in[l, :, WA_W + n_small:].astype(BF16)
        wpa, wpb, wout = w_pa[l].astype(BF16), w_pb[l].astype(BF16), w_out[l].astype(BF16)
        ws_p = w_s[l]
        bias_p = jnp.repeat(b_s[l].T, CHUNK, axis=1)
        ws_s = jnp.tile(w_s[l, :, 0:dec_seq, 0:dec_seq], (1, CHUNK // dec_seq, CHUNK // dec_seq))
        bias_s = jnp.repeat(jnp.tile(b_s[l, :, 0:dec_seq].T, (CHUNK // dec_seq, 1)), CHUNK, axis=1)

        x1 = _ffn(xp, row(g_ffn1), wup1, wdn1, tm_p)
        (u, v, k_p, v_p, kit_p, sga, sgb, qt, qit, wit, vt, kb, smallb) = _inproj(
            x1, row(g_mix), wa, ws, wg, row(g_v), row(g_q), row(g_k), tm_p, True, seq=seq)
        outs_p[0].append(k_p.reshape(batch, seq, N_KV_HEADS, HEAD_DIM))
        outs_p[1].append(v_p.reshape(batch, seq, N_KV_HEADS, HEAD_DIM))
        outs_p[2].append(jnp.swapaxes(kit_p, 1, 2))
        attn = _mix(qt, qit, wit, kb, vt, smallb, batch, seq)
        x2 = _post(x1, u, v, attn, sga, sgb, ws_p, bias_p, wpa, wpb, wout, tm_p, CHUNK)
        xp = _ffn(x2, row(g_ffn2), wup2, wdn2, tm_p)

        x1 = _ffn(xs, row(g_ffn1), wup1, wdn1, tm_s)
        (u, v, k_f, vv_f, small_f, sga, sgb, q_b, qi_b) = _inproj(
            x1, row(g_mix), wa, ws, wg, row(g_v), row(g_q), row(g_k), tm_s, False)
        attn = _sample_mixer(page_table, q_b, qi_b, small_f, k_f, vv_f,
                             cache_k_flat, cache_v_flat, cache_kidx_t, l, dec_seq)
        x2 = _post(x1, u.astype(BF16), v.astype(BF16), attn, sga, sgb, ws_s, bias_s, wpa, wpb, wout, tm_s, dec_seq)
        xs = _ffn(x2, row(g_ffn2), wup2, wdn2, tm_s)
        outs[0].append(k_f.reshape(db, dec_seq, N_KV_HEADS, HEAD_DIM))
        outs[1].append(vv_f.reshape(db, dec_seq, N_KV_HEADS, HEAD_DIM))
        outs[2].append(small_f[:, 0:IDX_DIM].reshape(db, dec_seq, IDX_DIM))
        outs[3].append(v.reshape(db, dec_seq, A_WIDTH))

    return (xp.reshape(batch, seq, D_MODEL), xs.reshape(db, dec_seq, D_MODEL),
            *[jnp.stack(o) for o in outs_p], *[jnp.stack(o) for o in outs])
```

```python
import functools

import jax
import jax.numpy as jnp
import numpy as np
from jax import lax
from jax.experimental import pallas as pl
from jax.experimental.pallas import tpu as pltpu

D_MODEL = 1024
D_FF = 2816
CHUNK = 128
A_GROUPS = 8
A_WIDTH = D_MODEL
N_HEADS = 8
HEAD_DIM = 128
N_KV_HEADS = 2
KV_GROUP = N_HEADS // N_KV_HEADS
IDX_HEADS = 8
IDX_DIM = 64
TOPK_MAX = 256
Q_BLOCK = 512
PAGE_SIZE = 128
EPS = 1e-6

KV_W = N_KV_HEADS * HEAD_DIM
QI_W = IDX_HEADS * IDX_DIM
WA_W = 3 * D_MODEL + 2 * KV_W + QI_W
SMALL_W = 128
F_CHUNK = 256
KEY_CHUNK = 256
ATT_CHUNK = 512
ONES_ROWS = 16
NEG = -1e30
INT_MIN = -2 ** 31
KEY_NEG_INF = INT_MIN + 0x7FFFFF
VMEM_LIMIT = 56 * 1024 * 1024

BF16 = jnp.bfloat16
F32 = jnp.float32


def _dot(a, b):
    return jnp.dot(a, b, preferred_element_type=F32)


def _dot_nt(a, b):
    return lax.dot_general(a, b, (((1,), (1,)), ((), ())), preferred_element_type=F32)


def _rms(x, g):
    return x * lax.rsqrt(jnp.mean(x * x, axis=-1, keepdims=True) + EPS) * g


def _sigmoid(x):
    return 1.0 / (1.0 + jnp.exp(-x))


def _gelu(x):
    c = np.float32(np.sqrt(2.0 / np.pi))
    return x * (0.5 * (1.0 + jnp.tanh(c * (x + 0.044715 * (x * x * x)))))


def _const_spec(shape):
    nd = len(shape)
    return pl.BlockSpec(shape, lambda *_: (0,) * nd, pipeline_mode=pl.Buffered(1))


def _params(n_axes):
    return pltpu.CompilerParams(dimension_semantics=("arbitrary",) * n_axes, vmem_limit_bytes=VMEM_LIMIT)


def _ffn_apply(x, g_ref, wup_ref, wdn_ref, h_ref):
    xb = _rms(x, g_ref[...]).astype(BF16)
    for j in range(D_FF // F_CHUNK):
        gate = _dot(xb, wup_ref[:, j * F_CHUNK:(j + 1) * F_CHUNK])
        up = _dot(xb, wup_ref[:, D_FF + j * F_CHUNK:D_FF + (j + 1) * F_CHUNK])
        h_ref[:, j * F_CHUNK:(j + 1) * F_CHUNK] = (gate * _sigmoid(gate) * up).astype(BF16)
    return x + 0.5 * _dot(h_ref[...], wdn_ref[...])


def _ffn_kernel(x_ref, g_ref, wup_ref, wdn_ref, o_ref, h_ref):
    o_ref[...] = _ffn_apply(x_ref[...], g_ref, wup_ref, wdn_ref, h_ref)


def _ffn(x, g, wup, wdn, tm):
    n = x.shape[0]
    return pl.pallas_call(
        _ffn_kernel,
        grid=(n // tm,),
        in_specs=[pl.BlockSpec((tm, D_MODEL), lambda i: (i, 0)),
                  _const_spec((1, D_MODEL)),
                  _const_spec((D_MODEL, 2 * D_FF)),
                  _const_spec((D_FF, D_MODEL))],
        out_specs=pl.BlockSpec((tm, D_MODEL), lambda i: (i, 0)),
        out_shape=jax.ShapeDtypeStruct((n, D_MODEL), F32),
        scratch_shapes=[pltpu.VMEM((tm, D_FF), BF16)],
        compiler_params=_params(1),
        name="ffn",
    )(x, g, wup, wdn)


def _inproj_kernel(x_ref, g_ref, wa_ref, ws_ref, wg_ref, gv_ref, gq_ref, gk_ref, *out_refs, prompt):
    if prompt:
        (u_ref, v_ref, k_ref, vv_ref, kit_ref, sga_ref, sgb_ref,
         qt_ref, qit_ref, wit_ref, vt_ref, kb_ref, smallb_ref) = out_refs
    else:
        (u_ref, v_ref, k_ref, vv_ref, small_ref, sga_ref, sgb_ref, q_ref, qi_ref) = out_refs
    tm = x_ref.shape[0]
    hb = _rms(x_ref[...], g_ref[...]).astype(BF16)

    u_ref[...] = _gelu(_dot(hb, wa_ref[:, 0:D_MODEL])).astype(u_ref.dtype)
    v = _gelu(_dot(hb, wa_ref[:, D_MODEL:2 * D_MODEL]))
    v_ref[...] = _rms(v, gv_ref[...]).astype(v_ref.dtype)

    q = _dot(hb, wa_ref[:, 2 * D_MODEL:3 * D_MODEL])
    gq = gq_ref[...]
    for h in range(N_HEADS):
        qh = _rms(q[:, h * HEAD_DIM:(h + 1) * HEAD_DIM], gq)
        if prompt:
            qt_ref[h * HEAD_DIM:(h + 1) * HEAD_DIM, :] = qh.T.astype(BF16)
        else:
            q_ref[:, h * HEAD_DIM:(h + 1) * HEAD_DIM] = qh.astype(BF16)

    o = 3 * D_MODEL
    kk = _dot(hb, wa_ref[:, o:o + KV_W])
    gk = gk_ref[...]
    vv = _dot(hb, wa_ref[:, o + KV_W:o + 2 * KV_W])
    for j in range(N_KV_HEADS):
        kj = _rms(kk[:, j * HEAD_DIM:(j + 1) * HEAD_DIM], gk)
        if prompt:
            k_ref[pl.ds(j, tm, stride=N_KV_HEADS), :] = kj
            vv_ref[pl.ds(j, tm, stride=N_KV_HEADS), :] = vv[:, j * HEAD_DIM:(j + 1) * HEAD_DIM]
            kb_ref[:, j * HEAD_DIM:(j + 1) * HEAD_DIM] = kj.astype(BF16)
        else:
            k_ref[:, j * HEAD_DIM:(j + 1) * HEAD_DIM] = kj
    if prompt:
        for r in range(tm // KEY_CHUNK):
            vt_ref[r] = vv[r * KEY_CHUNK:(r + 1) * KEY_CHUNK, :].T.astype(BF16)
    else:
        vv_ref[...] = vv

    qi = _dot(hb, wa_ref[:, o + 2 * KV_W:o + 2 * KV_W + QI_W])
    if prompt:
        qit_ref[...] = qi.T.astype(BF16)
    else:
        qi_ref[...] = qi.astype(BF16)

    small = _dot(hb, ws_ref[...])
    lane = lax.broadcasted_iota(jnp.int32, small.shape, 1)
    is_w = (lane >= IDX_DIM) & (lane < IDX_DIM + IDX_HEADS)
    small = small * jnp.where(is_w, np.float32(IDX_HEADS ** -0.5), np.float32(1.0))
    if prompt:
        smallb_ref[...] = small.astype(BF16)
        small_t = small.T
        kit_ref[0] = small_t[0:IDX_DIM, :]
        wit_ref[...] = small_t[IDX_DIM:IDX_DIM + IDX_HEADS, :]
    else:
        small_ref[...] = small

    sga_ref[...] = _sigmoid(_dot(hb, wg_ref[:, 0:D_MODEL])).astype(BF16)
    sgb_ref[...] = _sigmoid(_dot(hb, wg_ref[:, D_MODEL:2 * D_MODEL])).astype(BF16)


def _inproj(x, g, wa, ws, wg, gv, gq, gk, tm, prompt, seq=None):
    n = x.shape[0]
    steps = n // tm
    row = lambda w: pl.BlockSpec((tm, w), lambda i: (i, 0))
    col = lambda h: pl.BlockSpec((h, tm), lambda i: (0, i))
    if prompt:
        per_seq = seq // tm
        kv_shape = jax.ShapeDtypeStruct((n * N_KV_HEADS, HEAD_DIM), F32)
        kv_spec = pl.BlockSpec((tm * N_KV_HEADS, HEAD_DIM), lambda i: (i, 0))
        kit_shape = jax.ShapeDtypeStruct((n // seq, IDX_DIM, seq), F32)
        kit_spec = pl.BlockSpec((1, IDX_DIM, tm), lambda i: (i // per_seq, 0, i % per_seq))
        third = [(kv_shape, kv_spec), (kv_shape, kv_spec), (kit_shape, kit_spec)]
    else:
        third = [(jax.ShapeDtypeStruct((n, KV_W), F32), row(KV_W)),
                 (jax.ShapeDtypeStruct((n, KV_W), F32), row(KV_W)),
                 (jax.ShapeDtypeStruct((n, SMALL_W), F32), row(SMALL_W))]
    out_shape = [jax.ShapeDtypeStruct((n, D_MODEL), BF16),
                 jax.ShapeDtypeStruct((n, A_WIDTH), BF16 if prompt else F32),
                 *[s for s, _ in third],
                 jax.ShapeDtypeStruct((n, D_MODEL), BF16),
                 jax.ShapeDtypeStruct((n, D_MODEL), BF16)]
    out_specs = [row(D_MODEL), row(A_WIDTH), *[s for _, s in third], row(D_MODEL), row(D_MODEL)]
    if prompt:
        out_shape += [jax.ShapeDtypeStruct((D_MODEL, n), BF16),
                      jax.ShapeDtypeStruct((QI_W, n), BF16),
                      jax.ShapeDtypeStruct((IDX_HEADS, n), F32),
                      jax.ShapeDtypeStruct((n // KEY_CHUNK, KV_W, KEY_CHUNK), BF16),
                      jax.ShapeDtypeStruct((n, KV_W), BF16),
                      jax.ShapeDtypeStruct((n, SMALL_W), BF16)]
        out_specs += [col(D_MODEL), col(QI_W), col(IDX_HEADS),
                      pl.BlockSpec((tm // KEY_CHUNK, KV_W, KEY_CHUNK), lambda i: (i, 0, 0)),
                      row(KV_W), row(SMALL_W)]
    else:
        out_shape += [jax.ShapeDtypeStruct((n, D_MODEL), BF16),
                      jax.ShapeDtypeStruct((n, QI_W), BF16)]
        out_specs += [row(D_MODEL), row(QI_W)]
    return pl.pallas_call(
        functools.partial(_inproj_kernel, prompt=prompt),
        grid=(steps,),
        in_specs=[row(D_MODEL), _const_spec((1, D_MODEL)),
                  _const_spec((D_MODEL, WA_W)), _const_spec((D_MODEL, SMALL_W)), _const_spec((D_MODEL, 2 * D_MODEL)),
                  _const_spec((1, A_WIDTH)), _const_spec((1, HEAD_DIM)), _const_spec((1, HEAD_DIM))],
        out_specs=out_specs,
        out_shape=out_shape,
        compiler_params=_params(1),
        name="inproj_prompt" if prompt else "inproj_sample",
    )(x, g, wa, ws, wg, gv, gq, gk)


def _sort_key(x):
    bits = pltpu.bitcast(x, jnp.int32)
    bits = jnp.where(bits == INT_MIN, 0, bits)
    return jnp.where(bits < 0, bits ^ 0x7FFFFFFF, bits)


def _kth_threshold(count_ge, k, shape, total):
    def body(i, carry):
        t_u, cnt_t = carry
        cand_u = t_u | jnp.left_shift(jnp.int32(1), 31 - i)
        cnt = count_ge(cand_u ^ INT_MIN)
        ok = cnt >= k
        return jnp.where(ok, cand_u, t_u), jnp.where(ok, cnt, cnt_t)
    t_u, cnt_t = lax.fori_loop(0, 32, body, (jnp.zeros(shape, jnp.int32), jnp.full(shape, total, F32)))
    return t_u ^ INT_MIN, cnt_t


def _count_causal(keys_ref, rows, pred, t):
    lanes = keys_ref.shape[1]
    tiles = lanes // 128
    n_acc = 4
    cols = [slice(j * 128, (j + 1) * 128) for j in range(tiles)]
    ts = [t[:, c] for c in cols]
    accs = [[jnp.zeros((8, 128), F32) for _ in range(n_acc)] for _ in range(tiles)]
    for i in range(rows // 8):
        for j in range(tiles):
            if i * 8 < rows - lanes + (j + 1) * 128:
                hit = jnp.where(pred(keys_ref[i * 8:(i + 1) * 8, cols[j]], ts[j]), 1.0, 0.0)
                accs[j][i % n_acc] = accs[j][i % n_acc] + hit
    return jnp.concatenate([jnp.sum((a[0] + a[1]) + (a[2] + a[3]), axis=0, keepdims=True) for a in accs], axis=1)


def _mix_kernel(qt_ref, qit_ref, wit_ref, kb_ref, vt_ref, smallb_ref, o_ref,
                keys_ref, bias_ref, thr_ref, need_ref, flag_ref, m_ref, acc_ref, *, k_sel):
    seq = keys_ref.shape[0]
    n = pl.program_id(1)
    t0 = n * Q_BLOCK
    last = (n + 1) * Q_BLOCK - 1
    n_att = last // ATT_CHUNK + 1
    n_bias = n_att * (ATT_CHUNK // KEY_CHUNK)
    row = lax.broadcasted_iota(jnp.int32, (KEY_CHUNK, Q_BLOCK), 0)
    col = lax.broadcasted_iota(jnp.int32, (KEY_CHUNK, Q_BLOCK), 1)
    row_a = lax.broadcasted_iota(jnp.int32, (ATT_CHUNK, Q_BLOCK), 0)
    col_a = lax.broadcasted_iota(jnp.int32, (ATT_CHUNK, Q_BLOCK), 1)

    qcat = jnp.concatenate([qit_ref[h * IDX_DIM:(h + 1) * IDX_DIM, :] for h in range(IDX_HEADS)], axis=1)
    w = wit_ref[...] * np.float32(IDX_DIM ** -0.5)

    def score_chunk(c, carry):
        r0 = pl.multiple_of(c * ATT_CHUNK, ATT_CHUNK)
        ki = smallb_ref[pl.ds(r0, ATT_CHUNK), :][:, 0:IDX_DIM]
        dots = _dot(ki, qcat)
        sc = w[0:1, :] * jnp.maximum(dots[:, 0:Q_BLOCK], 0.0)
        for h in range(1, IDX_HEADS):
            sc = sc + w[h:h + 1, :] * jnp.maximum(dots[:, h * Q_BLOCK:(h + 1) * Q_BLOCK], 0.0)
        causal = (row_a + r0) <= (col_a + t0)
        keys_ref[pl.ds(r0, ATT_CHUNK), :] = _sort_key(jnp.where(causal, sc, -jnp.inf))
        return carry

    lax.fori_loop(0, n_att, score_chunk, 0)

    few = (n + 1) * Q_BLOCK <= k_sel

    @pl.when(few)
    def _():
        thr_ref[...] = jnp.full((1, Q_BLOCK), KEY_NEG_INF, jnp.int32)
        need_ref[...] = jnp.zeros((1, Q_BLOCK), F32)
        flag_ref[0] = 0.0

    ge = lambda kk, t: kk >= t
    gt = lambda kk, t: kk > t
    for nb in range(1, seq // Q_BLOCK + 1):
        @pl.when((n + 1 == nb) & jnp.logical_not(few))
        def _(nb=nb):
            rows = nb * Q_BLOCK
            thr, n_ge = _kth_threshold(lambda t: _count_causal(keys_ref, rows, ge, t), k_sel, (1, Q_BLOCK), rows)
            n_gt = _count_causal(keys_ref, rows, gt, thr)
            need = k_sel - n_gt
            tie = jnp.where((thr > KEY_NEG_INF) & (n_ge - n_gt > need), 1.0, 0.0)
            thr_ref[...] = thr
            need_ref[...] = need
            flag_ref[0] = jnp.max(tie)

    thr = thr_ref[...]
    need = need_ref[...]

    @pl.when(flag_ref[0] == 0.0)
    def _():
        thr_eff = jnp.maximum(thr, KEY_NEG_INF + 1)

        def body(c, carry):
            r0 = pl.multiple_of(c * KEY_CHUNK, KEY_CHUNK)
            bias_ref[pl.ds(r0, KEY_CHUNK), :] = jnp.where(keys_ref[pl.ds(r0, KEY_CHUNK), :] >= thr_eff, 0.0, NEG)
            return carry
        lax.fori_loop(0, n_bias, body, 0)

    @pl.when(flag_ref[0] != 0.0)
    def _():
        lower = (lax.broadcasted_iota(jnp.int32, (KEY_CHUNK, KEY_CHUNK), 1)
                 < lax.broadcasted_iota(jnp.int32, (KEY_CHUNK, KEY_CHUNK), 0))
        lower = jnp.where(lower, 1.0, 0.0).astype(BF16)

        def body(c, tie_carry):
            r0 = pl.multiple_of(c * KEY_CHUNK, KEY_CHUNK)
            kk = keys_ref[pl.ds(r0, KEY_CHUNK), :]
            eq = jnp.where(kk == thr, 1.0, 0.0)
            before = _dot(lower, eq.astype(BF16)) + tie_carry
            keep_tie = jnp.where(before < need, eq, 0.0)
            sel = jnp.where(kk > thr, 1.0, keep_tie)
            sel = jnp.where((row + r0) <= (col + t0), sel, 0.0)
            bias_ref[pl.ds(r0, KEY_CHUNK), :] = jnp.where(sel > 0.5, 0.0, NEG)
            return tie_carry + jnp.sum(eq, axis=0, keepdims=True)
        lax.fori_loop(0, n_bias, body, jnp.zeros((1, Q_BLOCK), F32))

    c2 = np.float32(HEAD_DIM ** -0.5 * np.log2(np.e))
    m_ref[...] = jnp.full(m_ref.shape, NEG, F32)
    acc_ref[...] = jnp.zeros(acc_ref.shape, F32)
    ones_rows = jnp.ones((ONES_ROWS, KEY_CHUNK), BF16)
    sub = ATT_CHUNK // KEY_CHUNK

    def attend(c, carry):
        r0 = pl.multiple_of(c * ATT_CHUNK, ATT_CHUNK)
        bias = bias_ref[pl.ds(r0, ATT_CHUNK), :]
        bias4 = jnp.concatenate([bias] * KV_GROUP, axis=1)
        for j in range(N_KV_HEADS):
            qj = jnp.concatenate(
                [qt_ref[(j * KV_GROUP + g) * HEAD_DIM:(j * KV_GROUP + g + 1) * HEAD_DIM, :]
                 for g in range(KV_GROUP)], axis=1)
            kj = kb_ref[pl.ds(r0, ATT_CHUNK), j * HEAD_DIM:(j + 1) * HEAD_DIM]
            s = _dot(kj, qj) + bias4
            m_old = m_ref[j]
            m_new = jnp.maximum(m_old, jnp.max(s, axis=0, keepdims=True))
            alpha = jnp.exp2((m_old - m_new) * c2)
            p = jnp.exp2((s - m_new) * c2).astype(BF16)
            acc = alpha * acc_ref[j]
            for r in range(sub):
                vj = jnp.concatenate([vt_ref[c * sub + r, j * HEAD_DIM:(j + 1) * HEAD_DIM, :], ones_rows], axis=0)
                acc = acc + _dot(vj, p[r * KEY_CHUNK:(r + 1) * KEY_CHUNK, :])
            acc_ref[j] = acc
            m_ref[j] = m_new
        return carry

    lax.fori_loop(0, n_att, attend, 0)

    for j in range(N_KV_HEADS):
        ot = acc_ref[j, 0:HEAD_DIM, :] / acc_ref[j, HEAD_DIM:HEAD_DIM + 1, :]
        for g in range(KV_GROUP):
            h = j * KV_GROUP + g
            o_ref[:, h * HEAD_DIM:(h + 1) * HEAD_DIM] = ot[:, g * Q_BLOCK:(g + 1) * Q_BLOCK].T.astype(BF16)


def _mix(qt, qit, wit, kb, vt, smallb, batch, seq):
    n = batch * seq
    nq = seq // Q_BLOCK
    k_sel = min(TOPK_MAX, seq // 4)
    gq = KV_GROUP * Q_BLOCK
    return pl.pallas_call(
        functools.partial(_mix_kernel, k_sel=k_sel),
        grid=(batch, nq),
        in_specs=[pl.BlockSpec((D_MODEL, Q_BLOCK), lambda b, i: (0, b * nq + i)),
                  pl.BlockSpec((QI_W, Q_BLOCK), lambda b, i: (0, b * nq + i)),
                  pl.BlockSpec((IDX_HEADS, Q_BLOCK), lambda b, i: (0, b * nq + i)),
                  pl.BlockSpec((seq, KV_W), lambda b, i: (b, 0)),
                  pl.BlockSpec((seq // KEY_CHUNK, KV_W, KEY_CHUNK), lambda b, i: (b, 0, 0)),
                  pl.BlockSpec((seq, SMALL_W), lambda b, i: (b, 0))],
        out_specs=pl.BlockSpec((Q_BLOCK, D_MODEL), lambda b, i: (b * nq + i, 0)),
        out_shape=jax.ShapeDtypeStruct((n, D_MODEL), BF16),
        scratch_shapes=[pltpu.VMEM((seq, Q_BLOCK), jnp.int32),
                        pltpu.VMEM((seq, Q_BLOCK), F32),
                        pltpu.VMEM((1, Q_BLOCK), jnp.int32),
                        pltpu.VMEM((1, Q_BLOCK), F32),
                        pltpu.SMEM((1,), F32),
                        pltpu.VMEM((N_KV_HEADS, 1, gq), F32),
                        pltpu.VMEM((N_KV_HEADS, HEAD_DIM + ONES_ROWS, gq), F32)],
        compiler_params=_params(2),
        name="mix_prompt",
    )(qt, qit, wit, kb, vt, smallb)


S_IDX_PAGES = 32
S_ATT_PAGES = 64
T_PAD = 8


def _page_copies(pt_ref, b, first, count, layer, cache_ref, buf_ref, slot, sem):
    return [pltpu.make_async_copy(cache_ref.at[layer, pt_ref[b, first + p]], buf_ref.at[slot, p], sem)
            for p in range(count)]


def _paged_loop(n_groups, copies, compute):
    def start(group):
        for i, c in enumerate(group):
            c.start(priority=i % 2)

    start(copies(0, 0))

    def body(g, carry):
        slot = g % 2

        @pl.when(g + 1 < n_groups)
        def _():
            start(copies(g + 1, 1 - slot))
        for c in copies(g, slot):
            c.wait()
        compute(g, slot)
        return carry

    lax.fori_loop(0, n_groups, body, 0)


def _s_idx_kernel(pt_ref, qi_ref, wb_ref, kin_ref, cache_ref, out_ref, outn_ref, buf_ref, sem_ref, *, layer):
    b = pl.program_id(0)
    n_pages = out_ref.shape[0]
    qi = qi_ref[0]
    wb = wb_ref[0]

    def score(kit, n_pg):
        r = jnp.maximum(_dot(qi, kit) * np.float32(IDX_DIM ** -0.5), 0.0) * jnp.concatenate([wb] * n_pg, axis=1)
        sc = r[0:T_PAD, :]
        for h in range(1, IDX_HEADS):
            sc = sc + r[h * T_PAD:(h + 1) * T_PAD, :]
        return sc

    outn_ref[...] = score(kin_ref[0], 1)

    def copies(g, slot):
        return _page_copies(pt_ref, b, g * S_IDX_PAGES, S_IDX_PAGES, layer, cache_ref, buf_ref, slot, sem_ref.at[slot])

    def compute(g, slot):
        kit = jnp.concatenate([buf_ref[slot, p] for p in range(S_IDX_PAGES)], axis=1).astype(BF16)
        sc = score(kit, S_IDX_PAGES)
        for p in range(S_IDX_PAGES):
            out_ref[g * S_IDX_PAGES + p] = sc[:, p * PAGE_SIZE:(p + 1) * PAGE_SIZE]

    _paged_loop(n_pages // S_IDX_PAGES, copies, compute)


def _s_idx(page_table, qi_rows, wb_rows, kit_new, cache_kidx_t, layer):
    db, n_pages = page_table.shape
    grid_spec = pltpu.PrefetchScalarGridSpec(
        num_scalar_prefetch=1,
        grid=(db,),
        in_specs=[pl.BlockSpec((1, IDX_HEADS * T_PAD, IDX_DIM), lambda b, pt: (b, 0, 0)),
                  pl.BlockSpec((1, IDX_HEADS * T_PAD, PAGE_SIZE), lambda b, pt: (b, 0, 0)),
                  pl.BlockSpec((1, IDX_DIM, PAGE_SIZE), lambda b, pt: (b, 0, 0)),
                  pl.BlockSpec(memory_space=pl.ANY)],
        out_specs=[pl.BlockSpec((n_pages, T_PAD, PAGE_SIZE), lambda b, pt: (0, b, 0)),
                   pl.BlockSpec((T_PAD, PAGE_SIZE), lambda b, pt: (b, 0))],
        scratch_shapes=[pltpu.VMEM((2, S_IDX_PAGES, IDX_DIM, PAGE_SIZE), F32),
                        pltpu.SemaphoreType.DMA((2,))])
    return pl.pallas_call(
        functools.partial(_s_idx_kernel, layer=layer),
        grid_spec=grid_spec,
        out_shape=[jax.ShapeDtypeStruct((n_pages, db * T_PAD, PAGE_SIZE), F32),
                   jax.ShapeDtypeStruct((db * T_PAD, PAGE_SIZE), F32)],
        compiler_params=_params(1),
        name="s_idx",
    )(page_table, qi_rows, wb_rows, kit_new, cache_kidx_t)


def _s_sel_kernel(sc_ref, scn_ref, mask_ref, maskn_ref, keys_ref, *, k_sel, n_new):
    n_pages, rows, _ = sc_ref.shape
    keys_ref[0:n_pages] = _sort_key(sc_ref[...])
    t = lax.broadcasted_iota(jnp.int32, (rows, PAGE_SIZE), 0) % T_PAD
    j = lax.broadcasted_iota(jnp.int32, (rows, PAGE_SIZE), 1)
    new_ok = (j <= t) & (j < n_new)
    keys_ref[n_pages] = _sort_key(jnp.where(new_ok, scn_ref[...], -jnp.inf))

    def count(pred):
        acc = jnp.zeros((rows, PAGE_SIZE), F32)
        for c in range(n_pages + 1):
            acc = acc + jnp.where(pred(keys_ref[c]), 1.0, 0.0)
        return jnp.sum(acc, axis=1, keepdims=True)

    thr, n_ge = _kth_threshold(lambda c: count(lambda kk: kk >= c), k_sel, (rows, 1), (n_pages + 1) * PAGE_SIZE)
    n_gt = count(lambda kk: kk > thr)
    need = k_sel - n_gt
    surplus = jnp.max(jnp.where((n_ge - n_gt > need) & (t[:, 0:1] < n_new), 1.0, 0.0))

    @pl.when(surplus == 0.0)
    def _():
        mask_ref[...] = jnp.where(keys_ref[0:n_pages] >= thr, 1.0, 0.0)
        maskn_ref[...] = jnp.where(new_ok & (keys_ref[n_pages] >= thr), 1.0, 0.0)

    @pl.when(surplus != 0.0)
    def _():
        upper = (lax.broadcasted_iota(jnp.int32, (PAGE_SIZE, PAGE_SIZE), 0)
                 < lax.broadcasted_iota(jnp.int32, (PAGE_SIZE, PAGE_SIZE), 1))
        upper = jnp.where(upper, 1.0, 0.0).astype(BF16)
        ones = jnp.ones((PAGE_SIZE, PAGE_SIZE), BF16)

        def tile(c, carry):
            kk = keys_ref[c]
            eq = jnp.where(kk == thr, 1.0, 0.0)
            eqb = eq.astype(BF16)
            before = _dot(eqb, upper) + carry
            keep_tie = jnp.where(before < need, eq, 0.0)
            return jnp.where(kk > thr, 1.0, keep_tie), carry + _dot(eqb, ones)

        def past_tile(c, carry):
            sel, carry = tile(c, carry)
            mask_ref[c] = sel
            return carry

        carry = lax.fori_loop(0, n_pages, past_tile, jnp.zeros((rows, PAGE_SIZE), F32))
        sel, _ = tile(n_pages, carry)
        maskn_ref[...] = jnp.where(new_ok, sel, 0.0)


def _s_sel(scores, scores_new, k_sel, n_new, rows_per_step):
    n_pages, rows, _ = scores.shape
    return pl.pallas_call(
        functools.partial(_s_sel_kernel, k_sel=k_sel, n_new=n_new),
        grid=(rows // rows_per_step,),
        in_specs=[pl.BlockSpec((n_pages, rows_per_step, PAGE_SIZE), lambda i: (0, i, 0)),
                  pl.BlockSpec((rows_per_step, PAGE_SIZE), lambda i: (i, 0))],
        out_specs=[pl.BlockSpec((n_pages, rows_per_step, PAGE_SIZE), lambda i: (0, i, 0)),
                   pl.BlockSpec((rows_per_step, PAGE_SIZE), lambda i: (i, 0))],
        out_shape=[jax.ShapeDtypeStruct((n_pages, rows, PAGE_SIZE), F32),
                   jax.ShapeDtypeStruct((rows, PAGE_SIZE), F32)],
        scratch_shapes=[pltpu.VMEM((n_pages + 1, rows_per_step, PAGE_SIZE), jnp.int32)],
        compiler_params=_params(1),
        name="s_sel",
    )(scores, scores_new)


def _s_att_kernel(pt_ref, q_ref, mask_ref, maskn_ref, kn_ref, vn_ref, *refs):
    k_refs = refs[:S_ATT_PAGES]
    v_refs = refs[S_ATT_PAGES:2 * S_ATT_PAGES]
    o_ref, m_ref, l_ref, acc_ref = refs[2 * S_ATT_PAGES:]
    i = pl.program_id(1)
    flat = N_KV_HEADS * PAGE_SIZE
    c2 = np.float32(HEAD_DIM ** -0.5 * np.log2(np.e))
    q = q_ref[0]
    spread = (lax.broadcasted_iota(jnp.int32, (PAGE_SIZE, flat), 1) // N_KV_HEADS
              == lax.broadcasted_iota(jnp.int32, (PAGE_SIZE, flat), 0))
    spread = jnp.where(spread, 1.0, 0.0).astype(BF16)

    def update(kf, vf, sel, n_pg):
        sel2 = _dot(sel.astype(BF16), spread)
        sel2 = jnp.concatenate([sel2[p * T_PAD:(p + 1) * T_PAD, :] for p in range(n_pg)], axis=1)
        bias_t = jnp.where(sel2 > 0.5, 0.0, NEG)
        bias_g = jnp.concatenate([bias_t] * KV_GROUP, axis=0)
        parity = lax.broadcasted_iota(jnp.int32, (1, n_pg * flat), 1) % N_KV_HEADS
        bias = jnp.concatenate([bias_g + jnp.where(parity == j, 0.0, NEG) for j in range(N_KV_HEADS)], axis=0)
        s = _dot_nt(q, kf) + bias
        m_old = m_ref[...]
        m_new = jnp.maximum(m_old, jnp.max(s, axis=1, keepdims=True))
        alpha = jnp.exp2((m_old - m_new) * c2)
        p = jnp.exp2((s - m_new) * c2)
        l_ref[...] = alpha * l_ref[...] + jnp.sum(p, axis=1, keepdims=True)
        acc_ref[...] = alpha * acc_ref[...] + _dot(p.astype(BF16), vf)
        m_ref[...] = m_new

    @pl.when(i == 0)
    def _():
        m_ref[...] = jnp.full(m_ref.shape, NEG, F32)
        l_ref[...] = jnp.zeros(l_ref.shape, F32)
        acc_ref[...] = jnp.zeros(acc_ref.shape, F32)
        update(kn_ref[0], vn_ref[0], maskn_ref[...], 1)

    kf = jnp.concatenate([k_refs[p][...] for p in range(S_ATT_PAGES)], axis=0).astype(BF16)
    vf = jnp.concatenate([v_refs[p][...] for p in range(S_ATT_PAGES)], axis=0).astype(BF16)
    update(kf, vf, mask_ref[...].reshape(S_ATT_PAGES * T_PAD, PAGE_SIZE), S_ATT_PAGES)

    @pl.when(i == pl.num_programs(1) - 1)
    def _():
        o_ref[0] = acc_ref[...] / l_ref[...]


def _s_att(page_table, q_rows, mask, mask_new, k_new, v_new, cache_k, cache_v, layer):
    db, n_pages = page_table.shape
    steps = n_pages // S_ATT_PAGES
    rows = N_KV_HEADS * KV_GROUP * T_PAD
    flat = N_KV_HEADS * PAGE_SIZE
    page_spec = lambda p: pl.BlockSpec(
        (None, None, flat, HEAD_DIM), lambda b, i, pt: (layer, pt[b, i * S_ATT_PAGES + p], 0, 0))
    grid_spec = pltpu.PrefetchScalarGridSpec(
        num_scalar_prefetch=1,
        grid=(db, steps),
        in_specs=[pl.BlockSpec((1, rows, HEAD_DIM), lambda b, i, pt: (b, 0, 0)),
                  pl.BlockSpec((S_ATT_PAGES, T_PAD, PAGE_SIZE), lambda b, i, pt: (i, b, 0)),
                  pl.BlockSpec((T_PAD, PAGE_SIZE), lambda b, i, pt: (b, 0)),
                  pl.BlockSpec((1, flat, HEAD_DIM), lambda b, i, pt: (b, 0, 0)),
                  pl.BlockSpec((1, flat, HEAD_DIM), lambda b, i, pt: (b, 0, 0))]
                 + [page_spec(p) for p in range(S_ATT_PAGES)] * 2,
        out_specs=pl.BlockSpec((1, rows, HEAD_DIM), lambda b, i, pt: (b, 0, 0)),
        scratch_shapes=[pltpu.VMEM((rows, 1), F32),
                        pltpu.VMEM((rows, 1), F32),
                        pltpu.VMEM((rows, HEAD_DIM), F32)])
    return pl.pallas_call(
        _s_att_kernel,
        grid_spec=grid_spec,
        out_shape=jax.ShapeDtypeStruct((db, rows, HEAD_DIM), F32),
        compiler_params=_params(2),
        name="s_att",
    )(page_table, q_rows, mask, mask_new, k_new, v_new,
      *([cache_k] * S_ATT_PAGES), *([cache_v] * S_ATT_PAGES))


def _sample_mixer(page_table, q_b, qi_b, small_f, k_f, vv_f, cache_k_flat, cache_v_flat, cache_kidx_t, layer, dec_seq):
    db, n_pages = page_table.shape
    t = dec_seq
    past = n_pages * PAGE_SIZE
    k_sel = min(TOPK_MAX, (past + t) // 4)
    pad_t = lambda a: jnp.pad(a, [(0, 0)] * (a.ndim - 2) + [(0, T_PAD - t), (0, 0)])

    qi_rows = pad_t(qi_b.reshape(db, t, IDX_HEADS, IDX_DIM).transpose(0, 2, 1, 3))
    qi_rows = qi_rows.reshape(db, IDX_HEADS * T_PAD, IDX_DIM)
    wi = small_f[:, IDX_DIM:IDX_DIM + IDX_HEADS].reshape(db, t, IDX_HEADS).transpose(0, 2, 1)
    wb_rows = jnp.pad(wi, ((0, 0), (0, 0), (0, T_PAD - t))).reshape(db, IDX_HEADS * T_PAD, 1)
    wb_rows = jnp.broadcast_to(wb_rows, (db, IDX_HEADS * T_PAD, PAGE_SIZE))
    kit_new = small_f[:, 0:IDX_DIM].reshape(db, t, IDX_DIM).transpose(0, 2, 1)
    kit_new = jnp.pad(kit_new, ((0, 0), (0, 0), (0, PAGE_SIZE - t))).astype(BF16)

    scores, scores_new = _s_idx(page_table, qi_rows, wb_rows, kit_new, cache_kidx_t, layer)
    rows_per_step = min(db * T_PAD, 64)
    mask, mask_new = _s_sel(scores, scores_new, k_sel, t, rows_per_step)

    q_rows = pad_t(q_b.reshape(db, t, N_KV_HEADS, KV_GROUP, HEAD_DIM).transpose(0, 2, 3, 1, 4))
    q_rows = q_rows.reshape(db, N_KV_HEADS * KV_GROUP * T_PAD, HEAD_DIM)
    flat_new = lambda a: jnp.pad(a.reshape(db, t * N_KV_HEADS, HEAD_DIM),
                                 ((0, 0), (0, (PAGE_SIZE - t) * N_KV_HEADS), (0, 0))).astype(BF16)
    o = _s_att(page_table, q_rows, mask, mask_new, flat_new(k_f), flat_new(vv_f), cache_k_flat, cache_v_flat, layer)
    o = o.reshape(db, N_KV_HEADS, KV_GROUP, T_PAD, HEAD_DIM)[:, :, :, :t]
    return o.transpose(0, 3, 1, 2, 4).reshape(db * t, N_HEADS * HEAD_DIM).astype(BF16)


def _post_kernel(x_ref, u_ref, v_ref, attn_ref, sga_ref, sgb_ref, ws_ref, bias_ref, wpa_ref, wpb_ref, wout_ref,
                 o_ref, a_ref, *, block):
    tm = x_ref.shape[0]
    r = lax.broadcasted_iota(jnp.int32, (CHUNK, CHUNK), 0)
    c = lax.broadcasted_iota(jnp.int32, (CHUNK, CHUNK), 1)
    allowed = (r >= c) & ((r // block) == (c // block))
    bias = bias_ref[...]
    for g in range(A_GROUPS):
        wg = jnp.where(allowed, ws_ref[g], 0.0).astype(BF16)
        for i in range(tm // CHUNK):
            rows = slice(i * CHUNK, (i + 1) * CHUNK)
            cols = slice(g * CHUNK, (g + 1) * CHUNK)
            sv = _dot(wg, v_ref[rows, cols]) + bias[:, cols]
            a_ref[rows, cols] = (u_ref[rows, cols].astype(F32) * sv).astype(BF16)
    merged = (sga_ref[...].astype(F32) * _dot(a_ref[...], wpa_ref[...])
              + sgb_ref[...].astype(F32) * _dot(attn_ref[...], wpb_ref[...]))
    o_ref[...] = x_ref[...] + _dot(merged.astype(BF16), wout_ref[...])


def _post(x, u, v, attn, sga, sgb, ws, bias, wpa, wpb, wout, tm, block):
    n = x.shape[0]
    row = lambda dt: pl.BlockSpec((tm, D_MODEL), lambda i: (i, 0))
    return pl.pallas_call(
        functools.partial(_post_kernel, block=block),
        grid=(n // tm,),
        in_specs=[row(F32), row(BF16), row(BF16), row(BF16), row(BF16), row(BF16),
                  _const_spec((A_GROUPS, CHUNK, CHUNK)), _const_spec((CHUNK, A_WIDTH)),
                  _const_spec((A_WIDTH, D_MODEL)), _const_spec((N_HEADS * HEAD_DIM, D_MODEL)),
                  _const_spec((D_MODEL, D_MODEL))],
        out_specs=row(F32),
        out_shape=jax.ShapeDtypeStruct((n, D_MODEL), F32),
        scratch_shapes=[pltpu.VMEM((tm, A_WIDTH), BF16)],
        compiler_params=_params(1),
        name="post",
    )(x, u, v, attn, sga, sgb, ws, bias, wpa, wpb, wout)


def kernel(x_prompt, x_sample, cache_k, cache_v, cache_kidx, page_table, g_ffn1, w_up1, w_down1, g_mix, w_in, g_v, g_q, g_k, w_s, b_s, w_pa, w_pb, w_out, g_ffn2, w_up2, w_down2):
    batch, seq, _ = x_prompt.shape
    db, dec_seq, _ = x_sample.shape
    depth = w_in.shape[0]
    n_p, n_s = batch * seq, db * dec_seq
    assert seq % ATT_CHUNK == 0 and seq % Q_BLOCK == 0 and n_s % CHUNK == 0 and CHUNK % dec_seq == 0 and dec_seq <= T_PAD
    tm_p = 512 if n_p % 512 == 0 else KEY_CHUNK
    tm_s = CHUNK

    xp = x_prompt.reshape(n_p, D_MODEL)
    xs = x_sample.reshape(n_s, D_MODEL)
    n_pool = cache_k.shape[1]
    cache_k_flat = cache_k.reshape(depth, n_pool, PAGE_SIZE * N_KV_HEADS, HEAD_DIM)
    cache_v_flat = cache_v.reshape(depth, n_pool, PAGE_SIZE * N_KV_HEADS, HEAD_DIM)
    cache_kidx_t = jnp.swapaxes(cache_kidx, 2, 3)
    outs_p = [[] for _ in range(3)]
    outs = [[] for _ in range(4)]
    for l in range(depth):
        row = lambda a: a[l].reshape(1, -1)
        wup1, wdn1 = w_up1[l].astype(BF16), w_down1[l].astype(BF16)
        wup2, wdn2 = w_up2[l].astype(BF16), w_down2[l].astype(BF16)
        wa = w_in[l, :, 0:WA_W].astype(BF16)
        n_small = IDX_DIM + IDX_HEADS
        ws = jnp.pad(w_in[l, :, WA_W:WA_W + n_small], ((0, 0), (0, SMALL_W - n_small))).astype(BF16)
        wg = w_in[l, :, WA_W + n_small:].astype(BF16)
        wpa, wpb, wout = w_pa[l].astype(BF16), w_pb[l].astype(BF16), w_out[l].astype(BF16)
        ws_p = w_s[l]
        bias_p = jnp.repeat(b_s[l].T, CHUNK, axis=1)
        ws_s = jnp.tile(w_s[l, :, 0:dec_seq, 0:dec_seq], (1, CHUNK // dec_seq, CHUNK // dec_seq))
        bias_s = jnp.repeat(jnp.tile(b_s[l, :, 0:dec_seq].T, (CHUNK // dec_seq, 1)), CHUNK, axis=1)

        x1 = _ffn(xp, row(g_ffn1), wup1, wdn1, tm_p)
        (u, v, k_p, v_p, kit_p, sga, sgb, qt, qit, wit, vt, kb, smallb) = _inproj(
            x1, row(g_mix), wa, ws, wg, row(g_v), row(g_q), row(g_k), tm_p, True, seq=seq)
        outs_p[0].append(k_p.reshape(batch, seq, N_KV_HEADS, HEAD_DIM))
        outs_p[1].append(v_p.reshape(batch, seq, N_KV_HEADS, HEAD_DIM))
        outs_p[2].append(jnp.swapaxes(kit_p, 1, 2))
        attn = _mix(qt, qit, wit, kb, vt, smallb, batch, seq)
        x2 = _post(x1, u, v, attn, sga, sgb, ws_p, bias_p, wpa, wpb, wout, tm_p, CHUNK)
        xp = _ffn(x2, row(g_ffn2), wup2, wdn2, tm_p)

        x1 = _ffn(xs, row(g_ffn1), wup1, wdn1, tm_s)
        (u, v, k_f, vv_f, small_f, sga, sgb, q_b, qi_b) = _inproj(
            x1, row(g_mix), wa, ws, wg, row(g_v), row(g_q), row(g_k), tm_s, False)
        attn = _sample_mixer(page_table, q_b, qi_b, small_f, k_f, vv_f,
                             cache_k_flat, cache_v_flat, cache_kidx_t, l, dec_seq)
        x2 = _post(x1, u.astype(BF16), v.astype(BF16), attn, sga, sgb, ws_s, bias_s, wpa, wpb, wout, tm_s, dec_seq)
        xs = _ffn(x2, row(g_ffn2), wup2, wdn2, tm_s)
        outs[0].append(k_f.reshape(db, dec_seq, N_KV_HEADS, HEAD_DIM))
        outs[1].append(vv_f.reshape(db, dec_seq, N_KV_HEADS, HEAD_DIM))
        outs[2].append(small_f[:, 0:IDX_DIM].reshape(db, dec_seq, IDX_DIM))
        outs[3].append(v.reshape(db, dec_seq, A_WIDTH))

    return (xp.reshape(batch, seq, D_MODEL), xs.reshape(db, dec_seq, D_MODEL),
            *[jnp.stack(o) for o in outs_p], *[jnp.stack(o) for o in outs])
```

```python
import functools

import jax
import jax.numpy as jnp
import numpy as np
from jax import lax
from jax.experimental import pallas as pl
from jax.experimental.pallas import tpu as pltpu

D_MODEL = 1024
D_FF = 2816
CHUNK = 128
A_GROUPS = 8
A_WIDTH = D_MODEL
N_HEADS = 8
HEAD_DIM = 128
N_KV_HEADS = 2
KV_GROUP = N_HEADS // N_KV_HEADS
IDX_HEADS = 8
IDX_DIM = 64
TOPK_MAX = 256
Q_BLOCK = 512
PAGE_SIZE = 128
EPS = 1e-6

KV_W = N_KV_HEADS * HEAD_DIM
QI_W = IDX_HEADS * IDX_DIM
WA_W = 3 * D_MODEL + 2 * KV_W + QI_W
SMALL_W = 128
F_CHUNK = 256
KEY_CHUNK = 256
ATT_CHUNK = 512
ONES_ROWS = 16
NEG = -1e30
INT_MIN = -2 ** 31
KEY_NEG_INF = INT_MIN + 0x7FFFFF
VMEM_LIMIT = 56 * 1024 * 1024

BF16 = jnp.bfloat16
F32 = jnp.float32


def _dot(a, b):
    return jnp.dot(a, b, preferred_element_type=F32)


def _dot_nt(a, b):
    return lax.dot_general(a, b, (((1,), (1,)), ((), ())), preferred_element_type=F32)


def _rms(x, g):
    return x * lax.rsqrt(jnp.mean(x * x, axis=-1, keepdims=True) + EPS) * g


def _sigmoid(x):
    return 1.0 / (1.0 + jnp.exp(-x))


def _gelu(x):
    c = np.float32(np.sqrt(2.0 / np.pi))
    return x * (0.5 * (1.0 + jnp.tanh(c * (x + 0.044715 * (x * x * x)))))


def _const_spec(shape):
    nd = len(shape)
    return pl.BlockSpec(shape, lambda *_: (0,) * nd, pipeline_mode=pl.Buffered(1))


def _params(n_axes):
    return pltpu.CompilerParams(dimension_semantics=("arbitrary",) * n_axes, vmem_limit_bytes=VMEM_LIMIT)


def _ffn_apply(x, g_ref, wup_ref, wdn_ref, h_ref):
    xb = _rms(x, g_ref[...]).astype(BF16)
    for j in range(D_FF // F_CHUNK):
        gate = _dot(xb, wup_ref[:, j * F_CHUNK:(j + 1) * F_CHUNK])
        up = _dot(xb, wup_ref[:, D_FF + j * F_CHUNK:D_FF + (j + 1) * F_CHUNK])
        h_ref[:, j * F_CHUNK:(j + 1) * F_CHUNK] = (gate * _sigmoid(gate) * up).astype(BF16)
    return x + 0.5 * _dot(h_ref[...], wdn_ref[...])


def _ffn_kernel(x_ref, g_ref, wup_ref, wdn_ref, o_ref, h_ref):
    o_ref[...] = _ffn_apply(x_ref[...], g_ref, wup_ref, wdn_ref, h_ref)


def _ffn(x, g, wup, wdn, tm):
    n = x.shape[0]
    return pl.pallas_call(
        _ffn_kernel,
        grid=(n // tm,),
        in_specs=[pl.BlockSpec((tm, D_MODEL), lambda i: (i, 0)),
                  _const_spec((1, D_MODEL)),
                  _const_spec((D_MODEL, 2 * D_FF)),
                  _const_spec((D_FF, D_MODEL))],
        out_specs=pl.BlockSpec((tm, D_MODEL), lambda i: (i, 0)),
        out_shape=jax.ShapeDtypeStruct((n, D_MODEL), F32),
        scratch_shapes=[pltpu.VMEM((tm, D_FF), BF16)],
        compiler_params=_params(1),
        name="ffn",
    )(x, g, wup, wdn)


def _inproj_kernel(x_ref, g_ref, wa_ref, ws_ref, wg_ref, gv_ref, gq_ref, gk_ref, *out_refs, prompt):
    if prompt:
        (u_ref, v_ref, k_ref, vv_ref, kit_ref, sga_ref, sgb_ref,
         qt_ref, qit_ref, wit_ref, vt_ref, kb_ref, smallb_ref) = out_refs
    else:
        (u_ref, v_ref, k_ref, vv_ref, small_ref, sga_ref, sgb_ref, q_ref, qi_ref) = out_refs
    tm = x_ref.shape[0]
    hb = _rms(x_ref[...], g_ref[...]).astype(BF16)

    u_ref[...] = _gelu(_dot(hb, wa_ref[:, 0:D_MODEL])).astype(u_ref.dtype)
    v = _gelu(_dot(hb, wa_ref[:, D_MODEL:2 * D_MODEL]))
    v_ref[...] = _rms(v, gv_ref[...]).astype(v_ref.dtype)

    q = _dot(hb, wa_ref[:, 2 * D_MODEL:3 * D_MODEL])
    gq = gq_ref[...]
    for h in range(N_HEADS):
        qh = _rms(q[:, h * HEAD_DIM:(h + 1) * HEAD_DIM], gq)
        if prompt:
            qt_ref[h * HEAD_DIM:(h + 1) * HEAD_DIM, :] = qh.T.astype(BF16)
        else:
            q_ref[:, h * HEAD_DIM:(h + 1) * HEAD_DIM] = qh.astype(BF16)

    o = 3 * D_MODEL
    kk = _dot(hb, wa_ref[:, o:o + KV_W])
    gk = gk_ref[...]
    vv = _dot(hb, wa_ref[:, o + KV_W:o + 2 * KV_W])
    for j in range(N_KV_HEADS):
        kj = _rms(kk[:, j * HEAD_DIM:(j + 1) * HEAD_DIM], gk)
        if prompt:
            k_ref[pl.ds(j, tm, stride=N_KV_HEADS), :] = kj
            vv_ref[pl.ds(j, tm, stride=N_KV_HEADS), :] = vv[:, j * HEAD_DIM:(j + 1) * HEAD_DIM]
            kb_ref[:, j * HEAD_DIM:(j + 1) * HEAD_DIM] = kj.astype(BF16)
        else:
            k_ref[:, j * HEAD_DIM:(j + 1) * HEAD_DIM] = kj
    if prompt:
        for r in range(tm // KEY_CHUNK):
            vt_ref[r] = vv[r * KEY_CHUNK:(r + 1) * KEY_CHUNK, :].T.astype(BF16)
    else:
        vv_ref[...] = vv

    qi = _dot(hb, wa_ref[:, o + 2 * KV_W:o + 2 * KV_W + QI_W])
    if prompt:
        qit_ref[...] = qi.T.astype(BF16)
    else:
        qi_ref[...] = qi.astype(BF16)

    small = _dot(hb, ws_ref[...])
    lane = lax.broadcasted_iota(jnp.int32, small.shape, 1)
    is_w = (lane >= IDX_DIM) & (lane < IDX_DIM + IDX_HEADS)
    small = small * jnp.where(is_w, np.float32(IDX_HEADS ** -0.5), np.float32(1.0))
    if prompt:
        smallb_ref[...] = small.astype(BF16)
        small_t = small.T
        kit_ref[0] = small_t[0:IDX_DIM, :]
        wit_ref[...] = small_t[IDX_DIM:IDX_DIM + IDX_HEADS, :]
    else:
        small_ref[...] = small

    sga_ref[...] = _sigmoid(_dot(hb, wg_ref[:, 0:D_MODEL])).astype(BF16)
    sgb_ref[...] = _sigmoid(_dot(hb, wg_ref[:, D_MODEL:2 * D_MODEL])).astype(BF16)


def _inproj(x, g, wa, ws, wg, gv, gq, gk, tm, prompt, seq=None):
    n = x.shape[0]
    steps = n // tm
    row = lambda w: pl.BlockSpec((tm, w), lambda i: (i, 0))
    col = lambda h: pl.BlockSpec((h, tm), lambda i: (0, i))
    if prompt:
        per_seq = seq // tm
        kv_shape = jax.ShapeDtypeStruct((n * N_KV_HEADS, HEAD_DIM), F32)
        kv_spec = pl.BlockSpec((tm * N_KV_HEADS, HEAD_DIM), lambda i: (i, 0))
        kit_shape = jax.ShapeDtypeStruct((n // seq, IDX_DIM, seq), F32)
        kit_spec = pl.BlockSpec((1, IDX_DIM, tm), lambda i: (i // per_seq, 0, i % per_seq))
        third = [(kv_shape, kv_spec), (kv_shape, kv_spec), (kit_shape, kit_spec)]
    else:
        third = [(jax.ShapeDtypeStruct((n, KV_W), F32), row(KV_W)),
                 (jax.ShapeDtypeStruct((n, KV_W), F32), row(KV_W)),
                 (jax.ShapeDtypeStruct((n, SMALL_W), F32), row(SMALL_W))]
    out_shape = [jax.ShapeDtypeStruct((n, D_MODEL), BF16),
                 jax.ShapeDtypeStruct((n, A_WIDTH), BF16 if prompt else F32),
                 *[s for s, _ in third],
                 jax.ShapeDtypeStruct((n, D_MODEL), BF16),
                 jax.ShapeDtypeStruct((n, D_MODEL), BF16)]
    out_specs = [row(D_MODEL), row(A_WIDTH), *[s for _, s in third], row(D_MODEL), row(D_MODEL)]
    if prompt:
        out_shape += [jax.ShapeDtypeStruct((D_MODEL, n), BF16),
                      jax.ShapeDtypeStruct((QI_W, n), BF16),
                      jax.ShapeDtypeStruct((IDX_HEADS, n), F32),
                      jax.ShapeDtypeStruct((n // KEY_CHUNK, KV_W, KEY_CHUNK), BF16),
                      jax.ShapeDtypeStruct((n, KV_W), BF16),
                      jax.ShapeDtypeStruct((n, SMALL_W), BF16)]
        out_specs += [col(D_MODEL), col(QI_W), col(IDX_HEADS),
                      pl.BlockSpec((tm // KEY_CHUNK, KV_W, KEY_CHUNK), lambda i: (i, 0, 0)),
                      row(KV_W), row(SMALL_W)]
    else:
        out_shape += [jax.ShapeDtypeStruct((n, D_MODEL), BF16),
                      jax.ShapeDtypeStruct((n, QI_W), BF16)]
        out_specs += [row(D_MODEL), row(QI_W)]
    return pl.pallas_call(
        functools.partial(_inproj_kernel, prompt=prompt),
        grid=(steps,),
        in_specs=[row(D_MODEL), _const_spec((1, D_MODEL)),
                  _const_spec((D_MODEL, WA_W)), _const_spec((D_MODEL, SMALL_W)), _const_spec((D_MODEL, 2 * D_MODEL)),
                  _const_spec((1, A_WIDTH)), _const_spec((1, HEAD_DIM)), _const_spec((1, HEAD_DIM))],
        out_specs=out_specs,
        out_shape=out_shape,
        compiler_params=_params(1),
        name="inproj_prompt" if prompt else "inproj_sample",
    )(x, g, wa, ws, wg, gv, gq, gk)


def _sort_key(x):
    bits = pltpu.bitcast(x, jnp.int32)
    bits = jnp.where(bits == INT_MIN, 0, bits)
    return jnp.where(bits < 0, bits ^ 0x7FFFFFFF, bits)


def _kth_threshold(count_ge, k, shape, total):
    def body(i, carry):
        t_u, cnt_t = carry
        cand_u = t_u | jnp.left_shift(jnp.int32(1), 31 - i)
        cnt = count_ge(cand_u ^ INT_MIN)
        ok = cnt >= k
        return jnp.where(ok, cand_u, t_u), jnp.where(ok, cnt, cnt_t)
    t_u, cnt_t = lax.fori_loop(0, 32, body, (jnp.zeros(shape, jnp.int32), jnp.full(shape, total, F32)))
    return t_u ^ INT_MIN, cnt_t


def _count_causal(keys_ref, rows, pred, t):
    lanes = keys_ref.shape[1]
    tiles = lanes // 128
    n_acc = 4
    cols = [slice(j * 128, (j + 1) * 128) for j in range(tiles)]
    ts = [t[:, c] for c in cols]
    accs = [[jnp.zeros((8, 128), F32) for _ in range(n_acc)] for _ in range(tiles)]
    for i in range(rows // 8):
        for j in range(tiles):
            if i * 8 < rows - lanes + (j + 1) * 128:
                hit = jnp.where(pred(keys_ref[i * 8:(i + 1) * 8, cols[j]], ts[j]), 1.0, 0.0)
                accs[j][i % n_acc] = accs[j][i % n_acc] + hit
    return jnp.concatenate([jnp.sum((a[0] + a[1]) + (a[2] + a[3]), axis=0, keepdims=True) for a in accs], axis=1)


def _mix_kernel(qt_ref, qit_ref, wit_ref, kb_ref, vt_ref, smallb_ref, o_ref,
                keys_ref, bias_ref, thr_ref, need_ref, flag_ref, m_ref, acc_ref, *, k_sel):
    seq = keys_ref.shape[0]
    n = pl.program_id(1)
    t0 = n * Q_BLOCK
    last = (n + 1) * Q_BLOCK - 1
    n_att = last // ATT_CHUNK + 1
    n_bias = n_att * (ATT_CHUNK // KEY_CHUNK)
    row = lax.broadcasted_iota(jnp.int32, (KEY_CHUNK, Q_BLOCK), 0)
    col = lax.broadcasted_iota(jnp.int32, (KEY_CHUNK, Q_BLOCK), 1)
    row_a = lax.broadcasted_iota(jnp.int32, (ATT_CHUNK, Q_BLOCK), 0)
    col_a = lax.broadcasted_iota(jnp.int32, (ATT_CHUNK, Q_BLOCK), 1)

    qcat = jnp.concatenate([qit_ref[h * IDX_DIM:(h + 1) * IDX_DIM, :] for h in range(IDX_HEADS)], axis=1)
    w = wit_ref[...] * np.float32(IDX_DIM ** -0.5)

    def score_chunk(c, carry):
        r0 = pl.multiple_of(c * ATT_CHUNK, ATT_CHUNK)
        ki = smallb_ref[pl.ds(r0, ATT_CHUNK), :][:, 0:IDX_DIM]
        dots = _dot(ki, qcat)
        sc = w[0:1, :] * jnp.maximum(dots[:, 0:Q_BLOCK], 0.0)
        for h in range(1, IDX_HEADS):
            sc = sc + w[h:h + 1, :] * jnp.maximum(dots[:, h * Q_BLOCK:(h + 1) * Q_BLOCK], 0.0)
        causal = (row_a + r0) <= (col_a + t0)
        keys_ref[pl.ds(r0, ATT_CHUNK), :] = _sort_key(jnp.where(causal, sc, -jnp.inf))
        return carry

    lax.fori_loop(0, n_att, score_chunk, 0)

    few = (n + 1) * Q_BLOCK <= k_sel

    @pl.when(few)
    def _():
        thr_ref[...] = jnp.full((1, Q_BLOCK), KEY_NEG_INF, jnp.int32)
        need_ref[...] = jnp.zeros((1, Q_BLOCK), F32)
        flag_ref[0] = 0.0

    ge = lambda kk, t: kk >= t
    gt = lambda kk, t: kk > t
    for nb in range(1, seq // Q_BLOCK + 1):
        @pl.when((n + 1 == nb) & jnp.logical_not(few))
        def _(nb=nb):
            rows = nb * Q_BLOCK
            thr, n_ge = _kth_threshold(lambda t: _count_causal(keys_ref, rows, ge, t), k_sel, (1, Q_BLOCK), rows)
            n_gt = _count_causal(keys_ref, rows, gt, thr)
            need = k_sel - n_gt
            tie = jnp.where((thr > KEY_NEG_INF) & (n_ge - n_gt > need), 1.0, 0.0)
            thr_ref[...] = thr
            need_ref[...] = need
            flag_ref[0] = jnp.max(tie)

    thr = thr_ref[...]
    need = need_ref[...]

    @pl.when(flag_ref[0] == 0.0)
    def _():
        thr_eff = jnp.maximum(thr, KEY_NEG_INF + 1)

        def body(c, carry):
            r0 = pl.multiple_of(c * KEY_CHUNK, KEY_CHUNK)
            bias_ref[pl.ds(r0, KEY_CHUNK), :] = jnp.where(keys_ref[pl.ds(r0, KEY_CHUNK), :] >= thr_eff, 0.0, NEG)
            return carry
        lax.fori_loop(0, n_bias, body, 0)

    @pl.when(flag_ref[0] != 0.0)
    def _():
        lower = (lax.broadcasted_iota(jnp.int32, (KEY_CHUNK, KEY_CHUNK), 1)
                 < lax.broadcasted_iota(jnp.int32, (KEY_CHUNK, KEY_CHUNK), 0))
        lower = jnp.where(lower, 1.0, 0.0).astype(BF16)

        def body(c, tie_carry):
            r0 = pl.multiple_of(c * KEY_CHUNK, KEY_CHUNK)
            kk = keys_ref[pl.ds(r0, KEY_CHUNK), :]
            eq = jnp.where(kk == thr, 1.0, 0.0)
            before = _dot(lower, eq.astype(BF16)) + tie_carry
            keep_tie = jnp.where(before < need, eq, 0.0)
            sel = jnp.where(kk > thr, 1.0, keep_tie)
            sel = jnp.where((row + r0) <= (col + t0), sel, 0.0)
            bias_ref[pl.ds(r0, KEY_CHUNK), :] = jnp.where(sel > 0.5, 0.0, NEG)
            return tie_carry + jnp.sum(eq, axis=0, keepdims=True)
        lax.fori_loop(0, n_bias, body, jnp.zeros((1, Q_BLOCK), F32))

    c2 = np.float32(HEAD_DIM ** -0.5 * np.log2(np.e))
    m_ref[...] = jnp.full(m_ref.shape, NEG, F32)
    acc_ref[...] = jnp.zeros(acc_ref.shape, F32)
    ones_rows = jnp.ones((ONES_ROWS, KEY_CHUNK), BF16)
    sub = ATT_CHUNK // KEY_CHUNK

    def attend(c, carry):
        r0 = pl.multiple_of(c * ATT_CHUNK, ATT_CHUNK)
        bias = bias_ref[pl.ds(r0, ATT_CHUNK), :]
        bias4 = jnp.concatenate([bias] * KV_GROUP, axis=1)
        for j in range(N_KV_HEADS):
            qj = jnp.concatenate(
                [qt_ref[(j * KV_GROUP + g) * HEAD_DIM:(j * KV_GROUP + g + 1) * HEAD_DIM, :]
                 for g in range(KV_GROUP)], axis=1)
            kj = kb_ref[pl.ds(r0, ATT_CHUNK), j * HEAD_DIM:(j + 1) * HEAD_DIM]
            s = _dot(kj, qj) + bias4
            m_old = m_ref[j]
            m_new = jnp.maximum(m_old, jnp.max(s, axis=0, keepdims=True))
            alpha = jnp.exp2((m_old - m_new) * c2)
            p = jnp.exp2((s - m_new) * c2).astype(BF16)
            acc = alpha * acc_ref[j]
            for r in range(sub):
                vj = jnp.concatenate([vt_ref[c * sub + r, j * HEAD_DIM:(j + 1) * HEAD_DIM, :], ones_rows], axis=0)
                acc = acc + _dot(vj, p[r * KEY_CHUNK:(r + 1) * KEY_CHUNK, :])
            acc_ref[j] = acc
            m_ref[j] = m_new
        return carry

    lax.fori_loop(0, n_att, attend, 0)

    for j in range(N_KV_HEADS):
        ot = acc_ref[j, 0:HEAD_DIM, :] / acc_ref[j, HEAD_DIM:HEAD_DIM + 1, :]
        for g in range(KV_GROUP):
            h = j * KV_GROUP + g
            o_ref[:, h * HEAD_DIM:(h + 1) * HEAD_DIM] = ot[:, g * Q_BLOCK:(g + 1) * Q_BLOCK].T.astype(BF16)


def _mix(qt, qit, wit, kb, vt, smallb, batch, seq):
    n = batch * seq
    nq = seq // Q_BLOCK
    k_sel = min(TOPK_MAX, seq // 4)
    gq = KV_GROUP * Q_BLOCK
    return pl.pallas_call(
        functools.partial(_mix_kernel, k_sel=k_sel),
        grid=(batch, nq),
        in_specs=[pl.BlockSpec((D_MODEL, Q_BLOCK), lambda b, i: (0, b * nq + i)),
                  pl.BlockSpec((QI_W, Q_BLOCK), lambda b, i: (0, b * nq + i)),
                  pl.BlockSpec((IDX_HEADS, Q_BLOCK), lambda b, i: (0, b * nq + i)),
                  pl.BlockSpec((seq, KV_W), lambda b, i: (b, 0)),
                  pl.BlockSpec((seq // KEY_CHUNK, KV_W, KEY_CHUNK), lambda b, i: (b, 0, 0)),
                  pl.BlockSpec((seq, SMALL_W), lambda b, i: (b, 0))],
        out_specs=pl.BlockSpec((Q_BLOCK, D_MODEL), lambda b, i: (b * nq + i, 0)),
        out_shape=jax.ShapeDtypeStruct((n, D_MODEL), BF16),
        scratch_shapes=[pltpu.VMEM((seq, Q_BLOCK), jnp.int32),
                        pltpu.VMEM((seq, Q_BLOCK), F32),
                        pltpu.VMEM((1, Q_BLOCK), jnp.int32),
                        pltpu.VMEM((1, Q_BLOCK), F32),
                        pltpu.SMEM((1,), F32),
                        pltpu.VMEM((N_KV_HEADS, 1, gq), F32),
                        pltpu.VMEM((N_KV_HEADS, HEAD_DIM + ONES_ROWS, gq), F32)],
        compiler_params=_params(2),
        name="mix_prompt",
    )(qt, qit, wit, kb, vt, smallb)


S_IDX_PAGES = 32
S_ATT_PAGES = 64
T_PAD = 8


def _page_copies(pt_ref, b, first, count, layer, cache_ref, buf_ref, slot, sem):
    return [pltpu.make_async_copy(cache_ref.at[layer, pt_ref[b, first + p]], buf_ref.at[slot, p], sem)
            for p in range(count)]


def _paged_loop(n_groups, copies, compute):
    def start(group):
        for i, c in enumerate(group):
            c.start(priority=i % 2)

    start(copies(0, 0))

    def body(g, carry):
        slot = g % 2

        @pl.when(g + 1 < n_groups)
        def _():
            start(copies(g + 1, 1 - slot))
        for c in copies(g, slot):
            c.wait()
        compute(g, slot)
        return carry

    lax.fori_loop(0, n_groups, body, 0)


def _s_idx_kernel(pt_ref, qi_ref, wb_ref, kin_ref, cache_ref, out_ref, outn_ref, buf_ref, sem_ref, *, layer):
    b = pl.program_id(0)
    n_pages = out_ref.shape[0]
    qi = qi_ref[0]
    wb = wb_ref[0]

    def score(kit, n_pg):
        r = jnp.maximum(_dot(qi, kit) * np.float32(IDX_DIM ** -0.5), 0.0) * jnp.concatenate([wb] * n_pg, axis=1)
        sc = r[0:T_PAD, :]
        for h in range(1, IDX_HEADS):
            sc = sc + r[h * T_PAD:(h + 1) * T_PAD, :]
        return sc

    outn_ref[...] = score(kin_ref[0], 1)

    def copies(g, slot):
        return _page_copies(pt_ref, b, g * S_IDX_PAGES, S_IDX_PAGES, layer, cache_ref, buf_ref, slot, sem_ref.at[slot])

    def compute(g, slot):
        kit = jnp.concatenate([buf_ref[slot, p] for p in range(S_IDX_PAGES)], axis=1).astype(BF16)
        sc = score(kit, S_IDX_PAGES)
        for p in range(S_IDX_PAGES):
            out_ref[g * S_IDX_PAGES + p] = sc[:, p * PAGE_SIZE:(p + 1) * PAGE_SIZE]

    _paged_loop(n_pages // S_IDX_PAGES, copies, compute)


def _s_idx(page_table, qi_rows, wb_rows, kit_new, cache_kidx_t, layer):
    db, n_pages = page_table.shape
    grid_spec = pltpu.PrefetchScalarGridSpec(
        num_scalar_prefetch=1,
        grid=(db,),
        in_specs=[pl.BlockSpec((1, IDX_HEADS * T_PAD, IDX_DIM), lambda b, pt: (b, 0, 0)),
                  pl.BlockSpec((1, IDX_HEADS * T_PAD, PAGE_SIZE), lambda b, pt: (b, 0, 0)),
                  pl.BlockSpec((1, IDX_DIM, PAGE_SIZE), lambda b, pt: (b, 0, 0)),
                  pl.BlockSpec(memory_space=pl.ANY)],
        out_specs=[pl.BlockSpec((n_pages, T_PAD, PAGE_SIZE), lambda b, pt: (0, b, 0)),
                   pl.BlockSpec((T_PAD, PAGE_SIZE), lambda b, pt: (b, 0))],
        scratch_shapes=[pltpu.VMEM((2, S_IDX_PAGES, IDX_DIM, PAGE_SIZE), F32),
                        pltpu.SemaphoreType.DMA((2,))])
    return pl.pallas_call(
        functools.partial(_s_idx_kernel, layer=layer),
        grid_spec=grid_spec,
        out_shape=[jax.ShapeDtypeStruct((n_pages, db * T_PAD, PAGE_SIZE), F32),
                   jax.ShapeDtypeStruct((db * T_PAD, PAGE_SIZE), F32)],
        compiler_params=_params(1),
        name="s_idx",
    )(page_table, qi_rows, wb_rows, kit_new, cache_kidx_t)


def _s_sel_kernel(sc_ref, scn_ref, mask_ref, maskn_ref, keys_ref, *, k_sel, n_new):
    n_pages, rows, _ = sc_ref.shape
    keys_ref[0:n_pages] = _sort_key(sc_ref[...])
    t = lax.broadcasted_iota(jnp.int32, (rows, PAGE_SIZE), 0) % T_PAD
    j = lax.broadcasted_iota(jnp.int32, (rows, PAGE_SIZE), 1)
    new_ok = (j <= t) & (j < n_new)
    keys_ref[n_pages] = _sort_key(jnp.where(new_ok, scn_ref[...], -jnp.inf))

    def count(pred):
        acc = jnp.zeros((rows, PAGE_SIZE), F32)
        for c in range(n_pages + 1):
            acc = acc + jnp.where(pred(keys_ref[c]), 1.0, 0.0)
        return jnp.sum(acc, axis=1, keepdims=True)

    thr, n_ge = _kth_threshold(lambda c: count(lambda kk: kk >= c), k_sel, (rows, 1), (n_pages + 1) * PAGE_SIZE)
    n_gt = count(lambda kk: kk > thr)
    need = k_sel - n_gt
    surplus = jnp.max(jnp.where((n_ge - n_gt > need) & (t[:, 0:1] < n_new), 1.0, 0.0))

    @pl.when(surplus == 0.0)
    def _():
        mask_ref[...] = jnp.where(keys_ref[0:n_pages] >= thr, 1.0, 0.0)
        maskn_ref[...] = jnp.where(new_ok & (keys_ref[n_pages] >= thr), 1.0, 0.0)

    @pl.when(surplus != 0.0)
    def _():
        upper = (lax.broadcasted_iota(jnp.int32, (PAGE_SIZE, PAGE_SIZE), 0)
                 < lax.broadcasted_iota(jnp.int32, (PAGE_SIZE, PAGE_SIZE), 1))
        upper = jnp.where(upper, 1.0, 0.0).astype(BF16)
        ones = jnp.ones((PAGE_SIZE, PAGE_SIZE), BF16)

        def tile(c, carry):
            kk = keys_ref[c]
            eq = jnp.where(kk == thr, 1.0, 0.0)
            eqb = eq.astype(BF16)
            before = _dot(eqb, upper) + carry
            keep_tie = jnp.where(before < need, eq, 0.0)
            return jnp.where(kk > thr, 1.0, keep_tie), carry + _dot(eqb, ones)

        def past_tile(c, carry):
            sel, carry = tile(c, carry)
            mask_ref[c] = sel
            return carry

        carry = lax.fori_loop(0, n_pages, past_tile, jnp.zeros((rows, PAGE_SIZE), F32))
        sel, _ = tile(n_pages, carry)
        maskn_ref[...] = jnp.where(new_ok, sel, 0.0)


def _s_sel(scores, scores_new, k_sel, n_new, rows_per_step):
    n_pages, rows, _ = scores.shape
    return pl.pallas_call(
        functools.partial(_s_sel_kernel, k_sel=k_sel, n_new=n_new),
        grid=(rows // rows_per_step,),
        in_specs=[pl.BlockSpec((n_pages, rows_per_step, PAGE_SIZE), lambda i: (0, i, 0)),
                  pl.BlockSpec((rows_per_step, PAGE_SIZE), lambda i: (i, 0))],
        out_specs=[pl.BlockSpec((n_pages, rows_per_step, PAGE_SIZE), lambda i: (0, i, 0)),
                   pl.BlockSpec((rows_per_step, PAGE_SIZE), lambda i: (i, 0))],
        out_shape=[jax.ShapeDtypeStruct((n_pages, rows, PAGE_SIZE), F32),
                   jax.ShapeDtypeStruct((rows, PAGE_SIZE), F32)],
        scratch_shapes=[pltpu.VMEM((n_pages + 1, rows_per_step, PAGE_SIZE), jnp.int32)],
        compiler_params=_params(1),
        name="s_sel",
    )(scores, scores_new)


def _s_att_kernel(pt_ref, q_ref, mask_ref, maskn_ref, kn_ref, vn_ref, *refs):
    k_refs = refs[:S_ATT_PAGES]
    v_refs = refs[S_ATT_PAGES:2 * S_ATT_PAGES]
    o_ref, m_ref, l_ref, acc_ref = refs[2 * S_ATT_PAGES:]
    i = pl.program_id(1)
    flat = N_KV_HEADS * PAGE_SIZE
    c2 = np.float32(HEAD_DIM ** -0.5 * np.log2(np.e))
    q = q_ref[0]
    spread = (lax.broadcasted_iota(jnp.int32, (PAGE_SIZE, flat), 1) // N_KV_HEADS
              == lax.broadcasted_iota(jnp.int32, (PAGE_SIZE, flat), 0))
    spread = jnp.where(spread, 1.0, 0.0).astype(BF16)

    def update(kf, vf, sel, n_pg):
        sel2 = _dot(sel.astype(BF16), spread)
        sel2 = jnp.concatenate([sel2[p * T_PAD:(p + 1) * T_PAD, :] for p in range(n_pg)], axis=1)
        bias_t = jnp.where(sel2 > 0.5, 0.0, NEG)
        bias_g = jnp.concatenate([bias_t] * KV_GROUP, axis=0)
        parity = lax.broadcasted_iota(jnp.int32, (1, n_pg * flat), 1) % N_KV_HEADS
        bias = jnp.concatenate([bias_g + jnp.where(parity == j, 0.0, NEG) for j in range(N_KV_HEADS)], axis=0)
        s = _dot_nt(q, kf) + bias
        m_old = m_ref[...]
        m_new = jnp.maximum(m_old, jnp.max(s, axis=1, keepdims=True))
        alpha = jnp.exp2((m_old - m_new) * c2)
        p = jnp.exp2((s - m_new) * c2)
        l_ref[...] = alpha * l_ref[...] + jnp.sum(p, axis=1, keepdims=True)
        acc_ref[...] = alpha * acc_ref[...] + _dot(p.astype(BF16), vf)
        m_ref[...] = m_new

    @pl.when(i == 0)
    def _():
        m_ref[...] = jnp.full(m_ref.shape, NEG, F32)
        l_ref[...] = jnp.zeros(l_ref.shape, F32)
        acc_ref[...] = jnp.zeros(acc_ref.shape, F32)
        update(kn_ref[0], vn_ref[0], maskn_ref[...], 1)

    kf = jnp.concatenate([k_refs[p][...] for p in range(S_ATT_PAGES)], axis=0).astype(BF16)
    vf = jnp.concatenate([v_refs[p][...] for p in range(S_ATT_PAGES)], axis=0).astype(BF16)
    update(kf, vf, mask_ref[...].reshape(S_ATT_PAGES * T_PAD, PAGE_SIZE), S_ATT_PAGES)

    @pl.when(i == pl.num_programs(1) - 1)
    def _():
        o_ref[0] = acc_ref[...] / l_ref[...]


def _s_att(page_table, q_rows, mask, mask_new, k_new, v_new, cache_k, cache_v, layer):
    db, n_pages = page_table.shape
    steps = n_pages // S_ATT_PAGES
    rows = N_KV_HEADS * KV_GROUP * T_PAD
    flat = N_KV_HEADS * PAGE_SIZE
    page_spec = lambda p: pl.BlockSpec(
        (None, None, flat, HEAD_DIM), lambda b, i, pt: (layer, pt[b, i * S_ATT_PAGES + p], 0, 0))
    grid_spec = pltpu.PrefetchScalarGridSpec(
        num_scalar_prefetch=1,
        grid=(db, steps),
        in_specs=[pl.BlockSpec((1, rows, HEAD_DIM), lambda b, i, pt: (b, 0, 0)),
                  pl.BlockSpec((S_ATT_PAGES, T_PAD, PAGE_SIZE), lambda b, i, pt: (i, b, 0)),
                  pl.BlockSpec((T_PAD, PAGE_SIZE), lambda b, i, pt: (b, 0)),
                  pl.BlockSpec((1, flat, HEAD_DIM), lambda b, i, pt: (b, 0, 0)),
                  pl.BlockSpec((1, flat, HEAD_DIM), lambda b, i, pt: (b, 0, 0))]
                 + [page_spec(p) for p in range(S_ATT_PAGES)] * 2,
        out_specs=pl.BlockSpec((1, rows, HEAD_DIM), lambda b, i, pt: (b, 0, 0)),
        scratch_shapes=[pltpu.VMEM((rows, 1), F32),
                        pltpu.VMEM((rows, 1), F32),
                        pltpu.VMEM((rows, HEAD_DIM), F32)])
    return pl.pallas_call(
        _s_att_kernel,
        grid_spec=grid_spec,
        out_shape=jax.ShapeDtypeStruct((db, rows, HEAD_DIM), F32),
        compiler_params=_params(2),
        name="s_att",
    )(page_table, q_rows, mask, mask_new, k_new, v_new,
      *([cache_k] * S_ATT_PAGES), *([cache_v] * S_ATT_PAGES))


def _sample_mixer(page_table, q_b, qi_b, small_f, k_f, vv_f, cache_k_flat, cache_v_flat, cache_kidx_t, layer, dec_seq):
    db, n_pages = page_table.shape
    t = dec_seq
    past = n_pages * PAGE_SIZE
    k_sel = min(TOPK_MAX, (past + t) // 4)
    pad_t = lambda a: jnp.pad(a, [(0, 0)] * (a.ndim - 2) + [(0, T_PAD - t), (0, 0)])

    qi_rows = pad_t(qi_b.reshape(db, t, IDX_HEADS, IDX_DIM).transpose(0, 2, 1, 3))
    qi_rows = qi_rows.reshape(db, IDX_HEADS * T_PAD, IDX_DIM)
    wi = small_f[:, IDX_DIM:IDX_DIM + IDX_HEADS].reshape(db, t, IDX_HEADS).transpose(0, 2, 1)
    wb_rows = jnp.pad(wi, ((0, 0), (0, 0), (0, T_PAD - t))).reshape(db, IDX_HEADS * T_PAD, 1)
    wb_rows = jnp.broadcast_to(wb_rows, (db, IDX_HEADS * T_PAD, PAGE_SIZE))
    kit_new = small_f[:, 0:IDX_DIM].reshape(db, t, IDX_DIM).transpose(0, 2, 1)
    kit_new = jnp.pad(kit_new, ((0, 0), (0, 0), (0, PAGE_SIZE - t))).astype(BF16)

    scores, scores_new = _s_idx(page_table, qi_rows, wb_rows, kit_new, cache_kidx_t, layer)
    rows_per_step = min(db * T_PAD, 64)
    mask, mask_new = _s_sel(scores, scores_new, k_sel, t, rows_per_step)

    q_rows = pad_t(q_b.reshape(db, t, N_KV_HEADS, KV_GROUP, HEAD_DIM).transpose(0, 2, 3, 1, 4))
    q_rows = q_rows.reshape(db, N_KV_HEADS * KV_GROUP * T_PAD, HEAD_DIM)
    flat_new = lambda a: jnp.pad(a.reshape(db, t * N_KV_HEADS, HEAD_DIM),
                                 ((0, 0), (0, (PAGE_SIZE - t) * N_KV_HEADS), (0, 0))).astype(BF16)
    o = _s_att(page_table, q_rows, mask, mask_new, flat_new(k_f), flat_new(vv_f), cache_k_flat, cache_v_flat, layer)
    o = o.reshape(db, N_KV_HEADS, KV_GROUP, T_PAD, HEAD_DIM)[:, :, :, :t]
    return o.transpose(0, 3, 1, 2, 4).reshape(db * t, N_HEADS * HEAD_DIM).astype(BF16)


def _post_kernel(x_ref, u_ref, v_ref, attn_ref, sga_ref, sgb_ref, ws_ref, bias_ref, wpa_ref, wpb_ref, wout_ref,
                 g2_ref, wup_ref, wdn_ref, o_ref, a_ref, h_ref, *, block):
    tm = x_ref.shape[0]
    r = lax.broadcasted_iota(jnp.int32, (CHUNK, CHUNK), 0)
    c = lax.broadcasted_iota(jnp.int32, (CHUNK, CHUNK), 1)
    allowed = (r >= c) & ((r // block) == (c // block))
    bias = bias_ref[...]
    for g in range(A_GROUPS):
        wg = jnp.where(allowed, ws_ref[g], 0.0).astype(BF16)
        for i in range(tm // CHUNK):
            rows = slice(i * CHUNK, (i + 1) * CHUNK)
            cols = slice(g * CHUNK, (g + 1) * CHUNK)
            sv = _dot(wg, v_ref[rows, cols]) + bias[:, cols]
            a_ref[rows, cols] = (u_ref[rows, cols].astype(F32) * sv).astype(BF16)
    merged = (sga_ref[...].astype(F32) * _dot(a_ref[...], wpa_ref[...])
              + sgb_ref[...].astype(F32) * _dot(attn_ref[...], wpb_ref[...]))
    x2 = x_ref[...] + _dot(merged.astype(BF16), wout_ref[...])
    o_ref[...] = _ffn_apply(x2, g2_ref, wup_ref, wdn_ref, h_ref)


def _post(x, u, v, attn, sga, sgb, ws, bias, wpa, wpb, wout, g2, wup, wdn, tm, block):
    n = x.shape[0]
    row = lambda dt: pl.BlockSpec((tm, D_MODEL), lambda i: (i, 0))
    return pl.pallas_call(
        functools.partial(_post_kernel, block=block),
        grid=(n // tm,),
        in_specs=[row(F32), row(BF16), row(BF16), row(BF16), row(BF16), row(BF16),
                  _const_spec((A_GROUPS, CHUNK, CHUNK)), _const_spec((CHUNK, A_WIDTH)),
                  _const_spec((A_WIDTH, D_MODEL)), _const_spec((N_HEADS * HEAD_DIM, D_MODEL)),
                  _const_spec((D_MODEL, D_MODEL)),
                  _const_spec((1, D_MODEL)), _const_spec((D_MODEL, 2 * D_FF)), _const_spec((D_FF, D_MODEL))],
        out_specs=row(F32),
        out_shape=jax.ShapeDtypeStruct((n, D_MODEL), F32),
        scratch_shapes=[pltpu.VMEM((tm, A_WIDTH), BF16), pltpu.VMEM((tm, D_FF), BF16)],
        compiler_params=_params(1),
        name="post",
    )(x, u, v, attn, sga, sgb, ws, bias, wpa, wpb, wout, g2, wup, wdn)


def kernel(x_prompt, x_sample, cache_k, cache_v, cache_kidx, page_table, g_ffn1, w_up1, w_down1, g_mix, w_in, g_v, g_q, g_k, w_s, b_s, w_pa, w_pb, w_out, g_ffn2, w_up2, w_down2):
    batch, seq, _ = x_prompt.shape
    db, dec_seq, _ = x_sample.shape
    depth = w_in.shape[0]
    n_p, n_s = batch * seq, db * dec_seq
    assert seq % ATT_CHUNK == 0 and seq % Q_BLOCK == 0 and n_s % CHUNK == 0 and CHUNK % dec_seq == 0 and dec_seq <= T_PAD
    tm_p = 512 if n_p % 512 == 0 else KEY_CHUNK
    tm_s = CHUNK

    xp = x_prompt.reshape(n_p, D_MODEL)
    xs = x_sample.reshape(n_s, D_MODEL)
    n_pool = cache_k.shape[1]
    cache_k_flat = cache_k.reshape(depth, n_pool, PAGE_SIZE * N_KV_HEADS, HEAD_DIM)
    cache_v_flat = cache_v.reshape(depth, n_pool, PAGE_SIZE * N_KV_HEADS, HEAD_DIM)
    cache_kidx_t = jnp.swapaxes(cache_kidx, 2, 3)
    outs_p = [[] for _ in range(3)]
    outs = [[] for _ in range(4)]
    for l in range(depth):
        row = lambda a: a[l].reshape(1, -1)
        wup1, wdn1 = w_up1[l].astype(BF16), w_down1[l].astype(BF16)
        wup2, wdn2 = w_up2[l].astype(BF16), w_down2[l].astype(BF16)
        wa = w_in[l, :, 0:WA_W].astype(BF16)
        n_small = IDX_DIM + IDX_HEADS
        ws = jnp.pad(w_in[l, :, WA_W:WA_W + n_small], ((0, 0), (0, SMALL_W - n_small))).astype(BF16)
        wg = w_in[l, :, WA_W + n_small:].astype(BF16)
        wpa, wpb, wout = w_pa[l].astype(BF16), w_pb[l].astype(BF16), w_out[l].astype(BF16)
        ws_p = w_s[l]
        bias_p = jnp.repeat(b_s[l].T, CHUNK, axis=1)
        ws_s = jnp.tile(w_s[l, :, 0:dec_seq, 0:dec_seq], (1, CHUNK // dec_seq, CHUNK // dec_seq))
        bias_s = jnp.repeat(jnp.tile(b_s[l, :, 0:dec_seq].T, (CHUNK // dec_seq, 1)), CHUNK, axis=1)

        x1 = _ffn(xp, row(g_ffn1), wup1, wdn1, tm_p)
        (u, v, k_p, v_p, kit_p, sga, sgb, qt, qit, wit, vt, kb, smallb) = _inproj(
            x1, row(g_mix), wa, ws, wg, row(g_v), row(g_q), row(g_k), tm_p, True, seq=seq)
        outs_p[0].append(k_p.reshape(batch, seq, N_KV_HEADS, HEAD_DIM))
        outs_p[1].append(v_p.reshape(batch, seq, N_KV_HEADS, HEAD_DIM))
        outs_p[2].append(jnp.swapaxes(kit_p, 1, 2))
        attn = _mix(qt, qit, wit, kb, vt, smallb, batch, seq)
        xp = _post(x1, u, v, attn, sga, sgb, ws_p, bias_p, wpa, wpb, wout, row(g_ffn2), wup2, wdn2, tm_p, CHUNK)

        x1 = _ffn(xs, row(g_ffn1), wup1, wdn1, tm_s)
        (u, v, k_f, vv_f, small_f, sga, sgb, q_b, qi_b) = _inproj(
            x1, row(g_mix), wa, ws, wg, row(g_v), row(g_q), row(g_k), tm_s, False)
        attn = _sample_mixer(page_table, q_b, qi_b, small_f, k_f, vv_f,
                             cache_k_flat, cache_v_flat, cache_kidx_t, l, dec_seq)
        xs = _post(x1, u.astype(BF16), v.astype(BF16), attn, sga, sgb, ws_s, bias_s, wpa, wpb, wout,
                   row(g_ffn2), wup2, wdn2, tm_s, dec_seq)
        outs[0].append(k_f.reshape(db, dec_seq, N_KV_HEADS, HEAD_DIM))
        outs[1].append(vv_f.reshape(db, dec_seq, N_KV_HEADS, HEAD_DIM))
        outs[2].append(small_f[:, 0:IDX_DIM].reshape(db, dec_seq, IDX_DIM))
        outs[3].append(v.reshape(db, dec_seq, A_WIDTH))

    return (xp.reshape(batch, seq, D_MODEL), xs.reshape(db, dec_seq, D_MODEL),
            *[jnp.stack(o) for o in outs_p], *[jnp.stack(o) for o in outs])
```
